```python
import math
import jax, jax.numpy as jnp
from jax import lax
import numpy as np

D_MODEL = 1024
BATCH = 8
SEQ = 4096
DEPTH = 4

GRID_W = 64
CTX_LEN = 256
N_MOD = 6
NORM_EPS = 1e-6
ROPE_BASE = 10000.0
Q_BLOCK = 128
MLA_HEADS = 8
MLA_HEAD_DIM = D_MODEL // 16
MLA_ROPE_DIM = D_MODEL // 32
MLA_Q_RANK = 3 * D_MODEL // 8
MLA_KV_RANK = D_MODEL // 4
MLA_W = MLA_HEADS * MLA_HEAD_DIM
DIFF_HEADS = 4
DIFF_HEAD_DIM = D_MODEL // 16
DIFF_W = DIFF_HEADS * 2 * DIFF_HEAD_DIM
MIX_W = MLA_W + DIFF_W
IN_SIZES = (MLA_Q_RANK, MLA_KV_RANK, MLA_ROPE_DIM, DIFF_W, DIFF_W, DIFF_W)
IN_SPLITS = (384, 640, 672, 1184, 1696)
PROJ_W = 2208
PEER_HEADS = 8
PEER_TOPK = 16
N_KEYS = 128
N_EXPERTS = N_KEYS * N_KEYS
PEER_KEY_DIM = D_MODEL // 4
PEER_KEY_HALF = PEER_KEY_DIM // 2
PEER_CHUNK = 128

kernel_name = 'hybrid_mla_diffattn_peer_dit'


def rmsnorm(x, g):
    x32 = x.astype(jnp.float32)
    y = x32 * lax.rsqrt(jnp.mean(x32 * x32, axis=-1, keepdims=True) + NORM_EPS)
    return y.astype(x.dtype) * g


def axial_angles(row, col, dim):
    quarter = dim // 4
    inv_freq = ROPE_BASE ** (-jnp.arange(quarter, dtype=jnp.float32) / quarter)
    return jnp.concatenate([row[:, None] * inv_freq, col[:, None] * inv_freq], axis=-1)


def apply_rope(x, ang):
    cos = jnp.cos(ang).astype(x.dtype)
    sin = jnp.sin(ang).astype(x.dtype)
    x1, x2 = x[..., 0::2], x[..., 1::2]
    return jnp.stack([x1 * cos - x2 * sin, x1 * sin + x2 * cos], axis=-1).reshape(x.shape)


def sweep_query_blocks(fn, q):
    B, S = q.shape[:2]
    nb = S // Q_BLOCK
    qb = jnp.moveaxis(q.reshape((B, nb, Q_BLOCK) + q.shape[2:]), 1, 0)
    out = lax.map(fn, qb)
    return jnp.moveaxis(out, 0, 1).reshape((B, S) + out.shape[3:])


def softmax_attend(q, k, v):
    s = jnp.einsum('bqhd,bkhd->bhqk', q, k).astype(jnp.float32) * (q.shape[-1] ** -0.5)
    p = jax.nn.softmax(s, axis=-1)
    return jnp.einsum('bhqk,bkhd->bqhd', p.astype(v.dtype), v)


def diff_attend(q, k, v, lam):
    s = jnp.einsum('bqhcd,bkhcd->bhcqk', q, k).astype(jnp.float32) * (q.shape[-1] ** -0.5)
    p = jax.nn.softmax(s, axis=-1)
    a = p[:, :, 0] - lam * p[:, :, 1]
    return jnp.einsum('bhqk,bkhd->bqhd', a.astype(v.dtype), v)


def mla_qkv(cq, ckv, kr, p, ang):
    B, L = cq.shape[:2]
    cq = rmsnorm(cq, p['mla_q_norm_g'])
    q_nope = (cq @ p['mla_wq_up']).reshape(B, L, MLA_HEADS, MLA_HEAD_DIM)
    q_rope = (cq @ p['mla_wq_rope']).reshape(B, L, MLA_HEADS, MLA_ROPE_DIM)
    ckv = rmsnorm(ckv, p['mla_kv_norm_g'])
    k_nope = (ckv @ p['mla_wk_up']).reshape(B, L, MLA_HEADS, MLA_HEAD_DIM)
    v = (ckv @ p['mla_wv_up']).reshape(B, L, MLA_HEADS, MLA_HEAD_DIM)
    if ang is not None:
        q_rope = apply_rope(q_rope, ang[:, None, :])
        kr = apply_rope(kr, ang)
    k_rope = jnp.broadcast_to(kr[:, :, None, :], (B, L, MLA_HEADS, MLA_ROPE_DIM))
    return (jnp.concatenate([q_nope, q_rope], axis=-1),
            jnp.concatenate([k_nope, k_rope], axis=-1), v)


def diff_qkv(q, k, v, ang):
    B, L = q.shape[:2]
    q = q.reshape(B, L, DIFF_HEADS, 2, DIFF_HEAD_DIM)
    k = k.reshape(B, L, DIFF_HEADS, 2, DIFF_HEAD_DIM)
    v = v.reshape(B, L, DIFF_HEADS, 2 * DIFF_HEAD_DIM)
    if ang is not None:
        q = apply_rope(q, ang[:, None, None, :])
        k = apply_rope(k, ang[:, None, None, :])
    return q, k, v


def merge_groups(o_mla, o_diff, p, lam_init):
    B, L = o_mla.shape[:2]
    o_diff = rmsnorm(o_diff, p['diff_subnorm_g']) * (1.0 - lam_init)
    cat = jnp.concatenate([o_mla.reshape(B, L, MLA_W), o_diff.reshape(B, L, DIFF_W)], axis=-1)
    return cat @ p['w_out']


def parallel_mixers(h, hc, p, ang_mla, ang_diff, lam, lam_init, need_ctx_out):
    parts = jnp.split(h @ p['w_in'], IN_SPLITS, axis=-1)
    parts_c = jnp.split(hc @ p['w_in'], IN_SPLITS, axis=-1)
    q_m, k_m, v_m = mla_qkv(parts[0], parts[1], parts[2], p, ang_mla)
    qc_m, kc_m, vc_m = mla_qkv(parts_c[0], parts_c[1], parts_c[2], p, None)
    q_d, k_d, v_d = diff_qkv(parts[3], parts[4], parts[5], ang_diff)
    qc_d, kc_d, vc_d = diff_qkv(parts_c[3], parts_c[4], parts_c[5], None)
    k_m_all = jnp.concatenate([kc_m, k_m], axis=1)
    v_m_all = jnp.concatenate([vc_m, v_m], axis=1)
    k_d_all = jnp.concatenate([kc_d, k_d], axis=1)
    v_d_all = jnp.concatenate([vc_d, v_d], axis=1)
    o_m = sweep_query_blocks(lambda qb: softmax_attend(qb, k_m_all, v_m_all), q_m)
    o_d = sweep_query_blocks(lambda qb: diff_attend(qb, k_d_all, v_d_all, lam), q_d)
    y = merge_groups(o_m, o_d, p, lam_init)
    yc = None
    if need_ctx_out:
        yc = merge_groups(softmax_attend(qc_m, kc_m, vc_m),
                          diff_attend(qc_d, kc_d, vc_d, lam), p, lam_init)
    return y, yc


def peer_ffn(h, p):
    B, L, D = h.shape
    T = B * L

    def expert_block(xb):
        q = (xb @ p['peer_wq']).reshape(PEER_CHUNK, PEER_HEADS, 2, PEER_KEY_HALF)
        s = jnp.einsum('thcd,ckd->thck', q, p['peer_subkeys']).astype(jnp.float32)
        sv, si = lax.top_k(s, PEER_TOPK)
        n_cand = PEER_TOPK * PEER_TOPK
        cand_s = (sv[:, :, 0, :, None] + sv[:, :, 1, None, :]).reshape(PEER_CHUNK, PEER_HEADS, n_cand)
        cand_i = (si[:, :, 0, :, None] * N_KEYS + si[:, :, 1, None, :]).reshape(PEER_CHUNK, PEER_HEADS, n_cand)
        top_s, pos = lax.top_k(cand_s, PEER_TOPK)
        eidx = jnp.take_along_axis(cand_i, pos, axis=-1).reshape(PEER_CHUNK, PEER_HEADS * PEER_TOPK)
        gate = jax.nn.softmax(top_s, axis=-1).reshape(PEER_CHUNK, PEER_HEADS * PEER_TOPK).astype(xb.dtype)
        u = jnp.take(p['peer_u'], eidx, axis=0)
        v = jnp.take(p['peer_v'], eidx, axis=0)
        act = jax.nn.gelu(jnp.einsum('ted,td->te', u, xb), approximate=False) * gate
        return jnp.einsum('te,ted->td', act, v)

    out = lax.map(expert_block, h.reshape(T // PEER_CHUNK, PEER_CHUNK, D))
    return out.reshape(B, L, D)


def setup_inputs(seed: int = 0) -> dict:
    key = jax.random.key(seed)
    ks = jax.random.split(key, 23)
    nrm = jax.random.normal
    f32 = jnp.float32
    L, D = DEPTH, D_MODEL

    def w(k, shape, fan_in):
        return nrm(k, shape, f32) * (fan_in ** -0.5)

    def gain(k, shape):
        return 1.0 + 0.02 * nrm(k, shape, f32)

    return dict(
        x=nrm(ks[0], (BATCH, SEQ, D), f32),
        c=nrm(ks[1], (BATCH, D), f32),
        ctx=nrm(ks[2], (BATCH, CTX_LEN, D), f32),
        c_ctx=nrm(ks[3], (D,), f32),
        norm_attn_g=gain(ks[4], (L, D)),
        norm_ffn_g=gain(ks[5], (L, D)),
        w_ada=0.5 * w(ks[6], (L, D, N_MOD * D), D),
        b_ada=0.01 * nrm(ks[7], (L, N_MOD * D), f32),
        w_in=w(ks[8], (L, D, PROJ_W), D),
        mla_q_norm_g=gain(ks[9], (L, MLA_Q_RANK)),
        mla_wq_up=w(ks[10], (L, MLA_Q_RANK, MLA_HEADS * MLA_HEAD_DIM), MLA_Q_RANK),
        mla_wq_rope=w(ks[11], (L, MLA_Q_RANK, MLA_HEADS * MLA_ROPE_DIM), MLA_Q_RANK),
        mla_kv_norm_g=gain(ks[12], (L, MLA_KV_RANK)),
        mla_wk_up=w(ks[13], (L, MLA_KV_RANK, MLA_W), MLA_KV_RANK),
        mla_wv_up=w(ks[14], (L, MLA_KV_RANK, MLA_W), MLA_KV_RANK),
        diff_lambda=0.1 * nrm(ks[15], (L, 4, DIFF_HEAD_DIM), f32),
        diff_subnorm_g=gain(ks[16], (L, 2 * DIFF_HEAD_DIM)),
        w_out=w(ks[17], (L, MIX_W, D), MIX_W),
        peer_wq=w(ks[18], (L, D, PEER_HEADS * PEER_KEY_DIM), D),
        peer_subkeys=w(ks[19], (L, 2, N_KEYS, PEER_KEY_HALF), PEER_KEY_HALF),
        peer_u=w(ks[20], (L, N_EXPERTS, D), D),
        peer_v=0.5 * nrm(ks[21], (L, N_EXPERTS, D), f32),
        final_norm_g=gain(ks[22], (D,)),
    )


def reference(x, c, ctx, c_ctx, norm_attn_g, norm_ffn_g, w_ada, b_ada, w_in,
              mla_q_norm_g, mla_wq_up, mla_wq_rope, mla_kv_norm_g, mla_wk_up, mla_wv_up,
              diff_lambda, diff_subnorm_g, w_out, peer_wq, peer_subkeys, peer_u, peer_v,
              final_norm_g):
    S = x.shape[1]
    ROWS = S // GRID_W
    row = jnp.repeat(jnp.arange(ROWS, dtype=jnp.float32), GRID_W)
    col = jnp.tile(jnp.arange(GRID_W, dtype=jnp.float32), ROWS)
    ang_mla = axial_angles(row, col, MLA_ROPE_DIM)
    ang_diff = axial_angles(row, col, DIFF_HEAD_DIM)
    s_c = jax.nn.silu(c)
    s_cc = jax.nn.silu(c_ctx)
    xc = ctx
    for l in range(DEPTH):
        last = l == DEPTH - 1
        p = dict(w_in=w_in[l], mla_q_norm_g=mla_q_norm_g[l], mla_wq_up=mla_wq_up[l],
                 mla_wq_rope=mla_wq_rope[l], mla_kv_norm_g=mla_kv_norm_g[l],
                 mla_wk_up=mla_wk_up[l], mla_wv_up=mla_wv_up[l],
                 diff_subnorm_g=diff_subnorm_g[l], w_out=w_out[l], peer_wq=peer_wq[l],
                 peer_subkeys=peer_subkeys[l], peer_u=peer_u[l], peer_v=peer_v[l])
        sh1, sc1, g1, sh2, sc2, g2 = jnp.split((s_c @ w_ada[l] + b_ada[l])[:, None, :], N_MOD, axis=-1)
        sh1c, sc1c, g1c, sh2c, sc2c, g2c = jnp.split(s_cc @ w_ada[l] + b_ada[l], N_MOD, axis=-1)
        lam_init = 0.8 - 0.6 * math.exp(-0.3 * l)
        dl = diff_lambda[l].astype(jnp.float32)
        lam = jnp.exp(jnp.sum(dl[0] * dl[1])) - jnp.exp(jnp.sum(dl[2] * dl[3])) + lam_init
        h = rmsnorm(x, norm_attn_g[l]) * (1 + sc1) + sh1
        hc = rmsnorm(xc, norm_attn_g[l]) * (1 + sc1c) + sh1c
        y, yc = parallel_mixers(h, hc, p, ang_mla, ang_diff, lam, lam_init, not last)
        x = x + g1 * y
        h = rmsnorm(x, norm_ffn_g[l]) * (1 + sc2) + sh2
        x = x + g2 * peer_ffn(h, p)
        if not last:
            xc = xc + g1c * yc
            hc = rmsnorm(xc, norm_ffn_g[l]) * (1 + sc2c) + sh2c
            xc = xc + g2c * peer_ffn(hc, p)
    return rmsnorm(x, final_norm_g)
```

```python
import functools
import math

import jax
import jax.numpy as jnp
import numpy as np
from jax import lax
from jax.experimental import pallas as pl
from jax.experimental.pallas import tpu as pltpu

F32 = jnp.float32
BF16 = jnp.bfloat16

D_MODEL = 1024
BATCH = 8
SEQ = 4096
DEPTH = 4
GRID_W = 64
CTX_LEN = 256
N_MOD = 6
NORM_EPS = 1e-6
ROPE_BASE = 10000.0
MLA_HEADS = 8
MLA_HEAD_DIM = 64
MLA_ROPE_DIM = 32
MLA_Q_RANK = 384
MLA_KV_RANK = 256
DIFF_HEADS = 4
DIFF_HEAD_DIM = 64
PEER_HEADS = 8
PEER_TOPK = 16
N_KEYS = 128
N_EXPERTS = N_KEYS * N_KEYS

T_LAT = BATCH * SEQ
T_CTX = BATCH * CTX_LEN
TT = T_LAT + T_CTX
TM = 256
NB_LAT = T_LAT // TM
NB_ALL = TT // TM
BLK_PER_BATCH = SEQ // TM
MOD_ROWS = 16
PROJ_PAD = 2304
TQ = 256
TK = 512
PEER_TB = 128
LANES = 128
SUBLANES = 8
VMEM_LIMIT = 56 * 1024 * 1024


def _cparams(sem, vmem=VMEM_LIMIT):
    return pltpu.CompilerParams(dimension_semantics=sem, vmem_limit_bytes=vmem)


def _rms(x):
    return x * lax.rsqrt(jnp.mean(x * x, axis=-1, keepdims=True) + NORM_EPS)


def _bid(i):
    return jnp.where(i < NB_LAT, i // BLK_PER_BATCH, BATCH)


def _posblk(i):
    return jnp.where(i < NB_LAT, i % BLK_PER_BATCH, BLK_PER_BATCH)


def _mod_body(cc_ref, w_ref, b_ref, o_ref):
    cc = cc_ref[...]
    s = cc / (1.0 + jnp.exp(-cc))
    o_ref[0] = jnp.dot(s, w_ref[0], precision=lax.Precision.HIGHEST,
                       preferred_element_type=F32) + b_ref[0]


def _modulation(cc, w_ada, b_ada):
    nj = N_MOD
    return pl.pallas_call(
        _mod_body,
        grid=(DEPTH, nj),
        in_specs=[
            pl.BlockSpec((MOD_ROWS, D_MODEL), lambda l, j: (0, 0)),
            pl.BlockSpec((1, D_MODEL, D_MODEL), lambda l, j: (l, 0, j)),
            pl.BlockSpec((1, 1, D_MODEL), lambda l, j: (l, 0, j)),
        ],
        out_specs=pl.BlockSpec((1, MOD_ROWS, D_MODEL), lambda l, j: (l, 0, j)),
        out_shape=jax.ShapeDtypeStruct((DEPTH, MOD_ROWS, N_MOD * D_MODEL), F32),
        compiler_params=_cparams(("arbitrary", "arbitrary")),
        name="modulation",
    )(cc, w_ada, b_ada.reshape(DEPTH, 1, N_MOD * D_MODEL))


def _lam_body(d0_ref, d1_ref, d2_ref, d3_ref, li_ref, o_ref):
    a = jnp.sum(d0_ref[...] * d1_ref[...], axis=-1, keepdims=True)
    b = jnp.sum(d2_ref[...] * d3_ref[...], axis=-1, keepdims=True)
    o_ref[...] = jnp.exp(a) - jnp.exp(b) + li_ref[...]


def _diff_lambda(diff_lambda, lam_init):
    dl = diff_lambda.astype(F32)
    return pl.pallas_call(
        _lam_body,
        out_shape=jax.ShapeDtypeStruct((DEPTH, LANES), F32),
        name="diff_lambda",
    )(dl[:, 0], dl[:, 1], dl[:, 2], dl[:, 3], lam_init)


def _rope(xb, c, sp, sm, shift):
    return xb * c + pltpu.roll(xb, shift, 1) * sp + pltpu.roll(xb, LANES - shift, 1) * sm


def _qkv_body(has_peer, *refs):
    if has_peer:
        x_ref, peer_ref, g2_ref = refs[:3]
        refs = refs[3:]
    else:
        x_ref = refs[0]
        refs = refs[1:]
    (ng_ref, sc_ref, sh_ref, win_ref, gq_ref, gkv_ref, wq_ref, wk_ref, wv_ref, tab_ref) = refs[:10]
    outs = refs[10:]
    if has_peer:
        xo_ref, outs = outs[0], outs[1:]
    qm_ref, km_ref, vm_ref, qd_ref, kd_ref, vd_ref = outs

    x = x_ref[...]
    if has_peer:
        x = x + g2_ref[0] * peer_ref[...]
        xo_ref[...] = x
    h = _rms(x) * ng_ref[...] * (1.0 + sc_ref[0]) + sh_ref[0]
    proj = jnp.dot(h.astype(BF16), win_ref[...], preferred_element_type=F32)
    cq = _rms(proj[:, 0:384]) * gq_ref[...]
    ckv = _rms(proj[:, 384:640]) * gkv_ref[...]
    kr = proj[:, 640:768]
    q = jnp.dot(cq.astype(BF16), wq_ref[...], preferred_element_type=F32)
    ckv16 = ckv.astype(BF16)
    k = jnp.dot(ckv16, wk_ref[...], preferred_element_type=F32)
    v = jnp.dot(ckv16, wv_ref[...], preferred_element_type=F32)
    tab = tab_ref[...]
    cm, spm, smm = tab[:, 0:128], tab[:, 128:256], tab[:, 256:384]
    cd, spd, smd = tab[:, 384:512], tab[:, 512:640], tab[:, 640:768]
    kr_rot = _rope(kr, cm, spm, smm, MLA_ROPE_DIM // 2)
    scale_m = (MLA_HEAD_DIM + MLA_ROPE_DIM) ** -0.5
    for hh in range(MLA_HEADS):
        sl = slice(hh * LANES, (hh + 1) * LANES)
        qm_ref[:, sl] = (_rope(q[:, sl], cm, spm, smm, MLA_ROPE_DIM // 2) * scale_m).astype(BF16)
        km_ref[:, sl] = (k[:, sl] + kr_rot).astype(BF16)
    vm_ref[...] = v.astype(BF16)
    scale_d = DIFF_HEAD_DIM ** -0.5
    for hh in range(DIFF_HEADS):
        sl = slice(hh * LANES, (hh + 1) * LANES)
        qd = proj[:, 768 + hh * LANES:768 + (hh + 1) * LANES]
        kd = proj[:, 1280 + hh * LANES:1280 + (hh + 1) * LANES]
        qd_ref[:, sl] = (_rope(qd, cd, spd, smd, DIFF_HEAD_DIM // 2) * scale_d).astype(BF16)
        kd_ref[:, sl] = _rope(kd, cd, spd, smd, DIFF_HEAD_DIM // 2).astype(BF16)
    vd_ref[...] = proj[:, 1792:2304].astype(BF16)


def _qkv(x, peer, g2, ng, sc, sh, win, gq, gkv, wq, wk, wv, tab):
    has_peer = peer is not None
    row = lambda i: (i, 0)
    modrow = lambda i: (_bid(i), 0, 0)
    const = lambda i: (0, 0)
    in_specs = [pl.BlockSpec((TM, D_MODEL), row)]
    args = [x]
    if has_peer:
        in_specs += [pl.BlockSpec((TM, D_MODEL), row), pl.BlockSpec((1, 1, D_MODEL), modrow)]
        args += [peer, g2]
    in_specs += [
        pl.BlockSpec((1, D_MODEL), const),
        pl.BlockSpec((1, 1, D_MODEL), modrow),
        pl.BlockSpec((1, 1, D_MODEL), modrow),
        pl.BlockSpec((D_MODEL, PROJ_PAD), const),
        pl.BlockSpec((1, MLA_Q_RANK), const),
        pl.BlockSpec((1, MLA_KV_RANK), const),
        pl.BlockSpec((MLA_Q_RANK, MLA_HEADS * LANES), const),
        pl.BlockSpec((MLA_KV_RANK, MLA_HEADS * LANES), const),
        pl.BlockSpec((MLA_KV_RANK, 512), const),
        pl.BlockSpec((TM, 768), lambda i: (_posblk(i), 0)),
    ]
    args += [ng, sc, sh, win, gq, gkv, wq, wk, wv, tab]
    out_specs, out_shape = [], []
    if has_peer:
        out_specs.append(pl.BlockSpec((TM, D_MODEL), row))
        out_shape.append(jax.ShapeDtypeStruct((TT, D_MODEL), F32))
    for width in (1024, 1024, 512, 512, 512, 512):
        out_specs.append(pl.BlockSpec((TM, width), row))
        out_shape.append(jax.ShapeDtypeStruct((TT, width), BF16))
    res = pl.pallas_call(
        functools.partial(_qkv_body, has_peer),
        grid=(NB_ALL,),
        in_specs=in_specs,
        out_specs=out_specs,
        out_shape=out_shape,
        compiler_params=_cparams(("parallel",)),
        name="qkv",
    )(*args)
    if has_peer:
        return res[0], res[1:]
    return x, res


def _flash_update(q, k, v, carry):
    m, l, acc = carry
    s = lax.dot_general(q, k, (((1,), (1,)), ((), ())), preferred_element_type=F32)
    m_new = jnp.maximum(m, jnp.max(s, axis=1, keepdims=True))
    alpha = jnp.exp(m - m_new)
    p = jnp.exp(s - m_new)
    l = alpha * l + jnp.sum(p, axis=1, keepdims=True)
    acc = alpha * acc + jnp.dot(p.astype(BF16), v, preferred_element_type=F32)
    return m_new, l, acc


def _flash_head(q, kc_ref, kl_ref, vc_ref, vl_ref, ksl, n_chunks):
    init = (jnp.full((TQ, 1), -jnp.inf, F32), jnp.zeros((TQ, 1), F32), jnp.zeros((TQ, LANES), F32))
    carry = _flash_update(q, kc_ref[:, ksl], vc_ref[...], init)

    def chunk(c, carry):
        r0 = pl.multiple_of(c * TK, TK)
        return _flash_update(q, kl_ref[pl.ds(r0, TK), ksl], vl_ref[pl.ds(r0, TK), :], carry)

    m, l, acc = lax.fori_loop(0, n_chunks, chunk, carry)
    return acc / l


def _n_chunks(nq):
    return jnp.where(pl.program_id(2) < nq, SEQ // TK, 0)


def _attn_mla_body(nq, q_ref, kc_ref, kl_ref, vc_ref, vl_ref, o_ref):
    n_chunks = _n_chunks(nq)
    lane = lax.broadcasted_iota(jnp.int32, (TQ, LANES), 1)
    o0 = _flash_head(q_ref[:, 0:LANES], kc_ref, kl_ref, vc_ref, vl_ref, slice(0, LANES), n_chunks)
    o1 = _flash_head(q_ref[:, LANES:2 * LANES], kc_ref, kl_ref, vc_ref, vl_ref,
                     slice(LANES, 2 * LANES), n_chunks)
    o_ref[...] = jnp.where(lane < MLA_HEAD_DIM, o0, o1).astype(o_ref.dtype)


def _attn_diff_body(nq, lam_scale, q_ref, kc_ref, kl_ref, vc_ref, vl_ref, lam_ref, g_ref, o_ref):
    n_chunks = _n_chunks(nq)
    lane = lax.broadcasted_iota(jnp.int32, (TQ, LANES), 1)
    q = q_ref[...]
    zero = jnp.zeros_like(q)
    q0 = jnp.where(lane < DIFF_HEAD_DIM, q, zero)
    q1 = jnp.where(lane < DIFF_HEAD_DIM, zero, q)
    o0 = _flash_head(q0, kc_ref, kl_ref, vc_ref, vl_ref, slice(0, LANES), n_chunks)
    o1 = _flash_head(q1, kc_ref, kl_ref, vc_ref, vl_ref, slice(0, LANES), n_chunks)
    o = o0 - lam_ref[...] * o1
    o_ref[...] = (_rms(o) * g_ref[...] * lam_scale).astype(o_ref.dtype)


def _attention(kind, q, k, v, with_ctx, lam=None, g=None, lam_scale=None):
    nq = SEQ // TQ
    steps = nq + 1 if with_ctx else nq
    qw = 2 * LANES if kind == "mla" else LANES
    ctx0 = T_LAT // CTX_LEN

    def qrow(b, p, i):
        return (jnp.where(i < nq, b * nq + i, (T_LAT // TQ) + b), p)

    in_specs = [
        pl.BlockSpec((TQ, qw), qrow),
        pl.BlockSpec((CTX_LEN, qw), lambda b, p, i: (ctx0 + b, p)),
        pl.BlockSpec((SEQ, qw), lambda b, p, i: (b, p)),
        pl.BlockSpec((CTX_LEN, LANES), lambda b, p, i: (ctx0 + b, p)),
        pl.BlockSpec((SEQ, LANES), lambda b, p, i: (b, p)),
    ]
    args = [q, k, k, v, v]
    if kind == "mla":
        body = functools.partial(_attn_mla_body, nq)
    else:
        body = functools.partial(_attn_diff_body, nq, lam_scale)
        in_specs += [pl.BlockSpec((1, LANES), lambda b, p, i: (0, 0)),
                     pl.BlockSpec((1, LANES), lambda b, p, i: (0, 0))]
        args += [lam, g]
    rows = TT if with_ctx else T_LAT
    return pl.pallas_call(
        body,
        grid=(BATCH, 4, steps),
        in_specs=in_specs,
        out_specs=pl.BlockSpec((TQ, LANES), qrow),
        out_shape=jax.ShapeDtypeStruct((rows, 512), BF16),
        compiler_params=_cparams(("parallel", "parallel", "arbitrary")),
        name="attn_" + kind,
    )(*args)


def _top16(s, n_rows):
    iota = lax.broadcasted_iota(jnp.int32, s.shape, 0)
    slot = lax.broadcasted_iota(jnp.int32, (PEER_TOPK, s.shape[1]), 0)
    vals = jnp.zeros((PEER_TOPK, s.shape[1]), F32)
    idxs = jnp.zeros((PEER_TOPK, s.shape[1]), jnp.int32)
    for r in range(PEER_TOPK):
        m = jnp.max(s, axis=0, keepdims=True)
        am = jnp.min(jnp.where(s == m, iota, n_rows), axis=0, keepdims=True)
        vals = jnp.where(slot == r, m, vals)
        idxs = jnp.where(slot == r, am, idxs)
        s = jnp.where(iota == am, -jnp.inf, s)
    return vals, idxs


def _mid_body(x_ref, om_ref, od_ref, wo_m_ref, wo_d_ref, g1_ref, ng_ref, sc_ref, sh_ref,
              wq_ref, sk_ref, xo_ref, h_ref, eidx_ref, gate_ref, q_scr):
    y = (jnp.dot(om_ref[...], wo_m_ref[...], preferred_element_type=F32)
         + jnp.dot(od_ref[...], wo_d_ref[...], preferred_element_type=F32))
    x = x_ref[...] + g1_ref[0] * y
    xo_ref[...] = x
    h = _rms(x) * ng_ref[...] * (1.0 + sc_ref[0]) + sh_ref[0]
    h_ref[...] = h
    q_scr[...] = jnp.dot(h.astype(BF16), wq_ref[...], preferred_element_type=F32).astype(BF16)

    def head(hh, _):
        sv, si = [], []
        for c in range(2):
            c0 = pl.multiple_of(hh * 2 * N_KEYS + c * N_KEYS, N_KEYS)
            qh = q_scr[:, pl.ds(c0, N_KEYS)]
            s = lax.dot_general(sk_ref[c], qh, (((1,), (1,)), ((), ())), preferred_element_type=F32)
            vals, idxs = _top16(s, N_KEYS)
            sv.append(vals)
            si.append(idxs)
        cand_s = jnp.concatenate([sv[0][a:a + 1, :] + sv[1] for a in range(PEER_TOPK)], axis=0)
        cand_i = jnp.concatenate([si[0][a:a + 1, :] * N_KEYS + si[1] for a in range(PEER_TOPK)], axis=0)
        n_cand = PEER_TOPK * PEER_TOPK
        iota = lax.broadcasted_iota(jnp.int32, cand_s.shape, 0)
        slot = lax.broadcasted_iota(jnp.int32, (PEER_TOPK, TM), 0)
        top_s = jnp.zeros((PEER_TOPK, TM), F32)
        top_i = jnp.zeros((PEER_TOPK, TM), jnp.int32)
        for r in range(PEER_TOPK):
            m = jnp.max(cand_s, axis=0, keepdims=True)
            am = jnp.min(jnp.where(cand_s == m, iota, n_cand), axis=0, keepdims=True)
            hit = iota == am
            e = jnp.max(jnp.where(hit, cand_i, -1), axis=0, keepdims=True)
            top_s = jnp.where(slot == r, m, top_s)
            top_i = jnp.where(slot == r, e, top_i)
            cand_s = jnp.where(hit, -jnp.inf, cand_s)
        ex = jnp.exp(top_s - jnp.max(top_s, axis=0, keepdims=True))
        gate = ex / jnp.sum(ex, axis=0, keepdims=True)
        r0 = pl.multiple_of(hh * PEER_TOPK, PEER_TOPK)
        eidx_ref[pl.ds(r0, PEER_TOPK), :] = top_i
        gate_ref[pl.ds(r0, PEER_TOPK), :] = gate
        return 0

    lax.fori_loop(0, PEER_HEADS, head, 0)


def _mid(n_blocks, x, om, od, wo_m, wo_d, g1, ng, sc, sh, wq, sk):
    n = n_blocks * TM
    row = lambda i: (i, 0)
    modrow = lambda i: (_bid(i), 0, 0)
    const = lambda i: (0, 0)
    return pl.pallas_call(
        _mid_body,
        grid=(n_blocks,),
        in_specs=[
            pl.BlockSpec((TM, D_MODEL), row),
            pl.BlockSpec((TM, 512), row),
            pl.BlockSpec((TM, 512), row),
            pl.BlockSpec((512, D_MODEL), const),
            pl.BlockSpec((512, D_MODEL), const),
            pl.BlockSpec((1, 1, D_MODEL), modrow),
            pl.BlockSpec((1, D_MODEL), const),
            pl.BlockSpec((1, 1, D_MODEL), modrow),
            pl.BlockSpec((1, 1, D_MODEL), modrow),
            pl.BlockSpec((D_MODEL, PEER_HEADS * 2 * N_KEYS), const),
            pl.BlockSpec((2, N_KEYS, N_KEYS), lambda i: (0, 0, 0)),
        ],
        out_specs=[
            pl.BlockSpec((TM, D_MODEL), row),
            pl.BlockSpec((TM, D_MODEL), row),
            pl.BlockSpec((PEER_HEADS * PEER_TOPK, TM), lambda i: (0, i)),
            pl.BlockSpec((PEER_HEADS * PEER_TOPK, TM), lambda i: (0, i)),
        ],
        out_shape=[
            jax.ShapeDtypeStruct((n, D_MODEL), F32),
            jax.ShapeDtypeStruct((n, D_MODEL), F32),
            jax.ShapeDtypeStruct((PEER_HEADS * PEER_TOPK, n), jnp.int32),
            jax.ShapeDtypeStruct((PEER_HEADS * PEER_TOPK, n), F32),
        ],
        scratch_shapes=[pltpu.VMEM((TM, PEER_HEADS * 2 * N_KEYS), BF16)],
        compiler_params=_cparams(("parallel",)),
        name="mid",
    )(x, om, od, wo_m, wo_d, g1, ng, sc, sh, wq, sk)


N_SEL = PEER_HEADS * PEER_TOPK
HALF = D_MODEL // 2
_BFLY_ORDER = (0, 4, 2, 6, 1, 5, 3, 7)


def _pack_table(tab):
    u = lax.bitcast_convert_type(tab.astype(BF16), jnp.uint16).astype(jnp.uint32)
    w = u[:, :HALF] | (u[:, HALF:] << 16)
    return w.reshape(tab.shape[0], HALF // LANES, LANES)


def _load_pair(tab_ref, ea, eb):
    w = jnp.concatenate([tab_ref[ea], tab_ref[eb]], axis=0)
    lo = pltpu.bitcast(w << 16, F32)
    hi = pltpu.bitcast(w & jnp.uint32(0xFFFF0000), F32)
    return lo, hi


def _peer_u_body(idx_ref, x_ref, gate_ref, tab_ref, act_ref):
    sub = lax.broadcasted_iota(jnp.int32, (SUBLANES, LANES), 0)
    lane = lax.broadcasted_iota(jnp.int32, (SUBLANES, LANES), 1)
    top = sub < 4

    def fold(p, q, width):
        tp = p + pltpu.roll(p, SUBLANES - width, 0)
        tq = q + pltpu.roll(q, SUBLANES - width, 0)
        return jnp.where((sub % (2 * width)) < width, tp, pltpu.roll(tq, width, 0))

    def tile(tb, _):
        def token(tt, a):
            t = tb * SUBLANES + tt
            xt = x_ref[t]
            xsw = pltpu.roll(xt, 4, 0)
            xlo = jnp.where(top, xt, xsw)
            xhi = jnp.where(top, xsw, xt)
            for g in range(N_SEL // SUBLANES):
                parts = []
                for pr in range(4):
                    ea = idx_ref[t, g * SUBLANES + _BFLY_ORDER[2 * pr]]
                    eb = idx_ref[t, g * SUBLANES + _BFLY_ORDER[2 * pr + 1]]
                    lo, hi = _load_pair(tab_ref, ea, eb)
                    parts.append(lo * xlo + hi * xhi)
                r = fold(fold(parts[0], parts[1], 2), fold(parts[2], parts[3], 2), 1)
                score = jnp.sum(r, axis=1, keepdims=True)
                a = jnp.where(lane == tt * (N_SEL // SUBLANES) + g, score, a)
            return a

        a = lax.fori_loop(0, SUBLANES, token, jnp.zeros((SUBLANES, LANES), F32))
        act_ref[tb] = 0.5 * a * (1.0 + lax.erf(a * (2.0 ** -0.5))) * gate_ref[tb]
        return 0

    lax.fori_loop(0, PEER_TB // SUBLANES, tile, 0)


def _peer_v_body(idx_ref, act_ref, tab_ref, o_ref):
    sub = lax.broadcasted_iota(jnp.int32, (SUBLANES, LANES), 0)
    top = sub < 4

    def token(t, _):
        acc = [jnp.zeros((SUBLANES, LANES), F32) for _ in range(4)]
        for pr in range(N_SEL // 2):
            ea = idx_ref[t, 2 * pr]
            eb = idx_ref[t, 2 * pr + 1]
            lo, hi = _load_pair(tab_ref, ea, eb)
            av = jnp.where(top, act_ref[t, 2 * pr], act_ref[t, 2 * pr + 1])
            k = 2 * (pr % 2)
            acc[k] = acc[k] + av * lo
            acc[k + 1] = acc[k + 1] + av * hi
        lo = acc[0] + acc[2]
        hi = acc[1] + acc[3]
        lo = lo + pltpu.roll(lo, 4, 0)
        hi = hi + pltpu.roll(hi, 4, 0)
        o_ref[t] = jnp.where(top, lo, hi)
        return 0

    lax.fori_loop(0, PEER_TB, token, 0)


def _table_spec():
    return pl.BlockSpec((N_EXPERTS, HALF // LANES, LANES), lambda i: (0, 0, 0),
                        pipeline_mode=pl.Buffered(1))


def _peer_u(n, idx, x3, gate3, tab):
    return pl.pallas_call(
        _peer_u_body,
        grid=(n // PEER_TB,),
        in_specs=[
            pl.BlockSpec((PEER_TB, N_SEL), lambda i: (i, 0), memory_space=pltpu.SMEM),
            pl.BlockSpec((PEER_TB, SUBLANES, LANES), lambda i: (i, 0, 0)),
            pl.BlockSpec((PEER_TB // SUBLANES, SUBLANES, LANES), lambda i: (i, 0, 0)),
            _table_spec(),
        ],
        out_specs=pl.BlockSpec((PEER_TB // SUBLANES, SUBLANES, LANES), lambda i: (i, 0, 0)),
        out_shape=jax.ShapeDtypeStruct((n // SUBLANES, SUBLANES, LANES), F32),
        compiler_params=_cparams(("arbitrary",)),
        name="peer_u",
    )(idx, x3, gate3, tab)


def _peer_v(n, idx, act, tab):
    return pl.pallas_call(
        _peer_v_body,
        grid=(n // PEER_TB,),
        in_specs=[
            pl.BlockSpec((PEER_TB, N_SEL), lambda i: (i, 0), memory_space=pltpu.SMEM),
            pl.BlockSpec((PEER_TB, N_SEL), lambda i: (i, 0), memory_space=pltpu.SMEM),
            _table_spec(),
        ],
        out_specs=pl.BlockSpec((PEER_TB, SUBLANES, LANES), lambda i: (i, 0, 0)),
        out_shape=jax.ShapeDtypeStruct((n, SUBLANES, LANES), F32),
        compiler_params=_cparams(("arbitrary",)),
        name="peer_v",
    )(idx, act, tab)


def _final_body(x_ref, peer_ref, g2_ref, g_ref, o_ref):
    x = x_ref[...] + g2_ref[0] * peer_ref[...]
    o_ref[...] = _rms(x) * g_ref[...]


def _final(x, peer, g2, g):
    row = lambda i: (i, 0)
    return pl.pallas_call(
        _final_body,
        grid=(NB_LAT,),
        in_specs=[
            pl.BlockSpec((TM, D_MODEL), row),
            pl.BlockSpec((TM, D_MODEL), row),
            pl.BlockSpec((1, 1, D_MODEL), lambda i: (_bid(i), 0, 0)),
            pl.BlockSpec((1, D_MODEL), lambda i: (0, 0)),
        ],
        out_specs=pl.BlockSpec((TM, D_MODEL), row),
        out_shape=jax.ShapeDtypeStruct((T_LAT, D_MODEL), F32),
        compiler_params=_cparams(("parallel",)),
        name="final_norm",
    )(x, peer, g2, g)


def _deinterleave(width):
    return np.concatenate([np.arange(0, width, 2), np.arange(1, width, 2)])


def _rope_tables():
    pos = np.arange(SEQ)
    rowp = jnp.asarray(pos // GRID_W, F32)
    colp = jnp.asarray(pos % GRID_W, F32)

    def angles(dim):
        quarter = dim // 4
        inv = ROPE_BASE ** (-jnp.arange(quarter, dtype=F32) / quarter)
        return jnp.concatenate([rowp[:, None] * inv, colp[:, None] * inv], axis=-1)

    am = angles(MLA_ROPE_DIM)
    ad = angles(DIFF_HEAD_DIM)
    cm, sm_ = jnp.cos(am), jnp.sin(am)
    cd, sd = jnp.cos(ad), jnp.sin(ad)
    one = lambda w: jnp.ones((SEQ, w), F32)
    zero = lambda w: jnp.zeros((SEQ, w), F32)
    t_cm = jnp.concatenate([one(64), cm, cm, one(32)], axis=1)
    t_spm = jnp.concatenate([zero(80), sm_, zero(32)], axis=1)
    t_smm = jnp.concatenate([zero(64), -sm_, zero(48)], axis=1)
    t_cd = jnp.concatenate([cd, cd, cd, cd], axis=1)
    t_spd = jnp.concatenate([zero(32), sd, zero(32), sd], axis=1)
    t_smd = jnp.concatenate([-sd, zero(32), -sd, zero(32)], axis=1)
    lat = jnp.concatenate([t_cm, t_spm, t_smm, t_cd, t_spd, t_smd], axis=1)
    ident = jnp.concatenate([jnp.ones((CTX_LEN, 128), F32), jnp.zeros((CTX_LEN, 256), F32),
                             jnp.ones((CTX_LEN, 128), F32), jnp.zeros((CTX_LEN, 256), F32)], axis=1)
    return jnp.concatenate([lat, ident], axis=0)


def _prep_layer_weights(w_in, wq_up, wq_rope, wk_up, wv_up, w_out, peer_wq, peer_subkeys):
    p32 = _deinterleave(MLA_ROPE_DIM)
    p64 = _deinterleave(DIFF_HEAD_DIM)
    z = lambda w: jnp.zeros((D_MODEL, w), F32)
    kr = w_in[:, 640:672][:, p32]
    qd = w_in[:, 672:1184].reshape(D_MODEL, 8, 64)[:, :, p64].reshape(D_MODEL, 512)
    kd = w_in[:, 1184:1696].reshape(D_MODEL, 8, 64)[:, :, p64].reshape(D_MODEL, 512)
    win = jnp.concatenate([w_in[:, 0:640], z(64), kr, z(32), qd, kd, w_in[:, 1696:2208]], axis=1)
    qn = wq_up.reshape(MLA_Q_RANK, MLA_HEADS, MLA_HEAD_DIM)
    qr = wq_rope.reshape(MLA_Q_RANK, MLA_HEADS, MLA_ROPE_DIM)[:, :, p32]
    wq = jnp.concatenate([qn, qr, jnp.zeros((MLA_Q_RANK, MLA_HEADS, 32), F32)], axis=2)
    kn = wk_up.reshape(MLA_KV_RANK, MLA_HEADS, MLA_HEAD_DIM)
    wk = jnp.concatenate([kn, jnp.zeros((MLA_KV_RANK, MLA_HEADS, 64), F32)], axis=2)
    return dict(
        win=win.astype(BF16),
        wq=wq.reshape(MLA_Q_RANK, MLA_HEADS * LANES).astype(BF16),
        wk=wk.reshape(MLA_KV_RANK, MLA_HEADS * LANES).astype(BF16),
        wv=wv_up.astype(BF16),
        wo_m=w_out[:512].astype(BF16),
        wo_d=w_out[512:].astype(BF16),
        pwq=peer_wq.astype(BF16),
        sk=peer_subkeys.astype(BF16),
    )


def kernel(x, c, ctx, c_ctx, norm_attn_g, norm_ffn_g, w_ada, b_ada, w_in, mla_q_norm_g, mla_wq_up, mla_wq_rope, mla_kv_norm_g, mla_wk_up, mla_wv_up, diff_lambda, diff_subnorm_g, w_out, peer_wq, peer_subkeys, peer_u, peer_v, final_norm_g):
    xs = jnp.concatenate([x.reshape(T_LAT, D_MODEL), ctx.reshape(T_CTX, D_MODEL)], axis=0)
    cc = jnp.concatenate([c, c_ctx[None, :], jnp.zeros((MOD_ROWS - BATCH - 1, D_MODEL), F32)], axis=0)
    mod = _modulation(cc, w_ada, b_ada)
    lam_inits = [0.8 - 0.6 * math.exp(-0.3 * l) for l in range(DEPTH)]
    lam_all = _diff_lambda(diff_lambda, jnp.broadcast_to(jnp.asarray(lam_inits, F32)[:, None], (DEPTH, LANES)))
    tab = _rope_tables()

    peer_out = None
    g2_prev = None
    for l in range(DEPTH):
        last = l == DEPTH - 1
        w = _prep_layer_weights(w_in[l], mla_wq_up[l], mla_wq_rope[l], mla_wk_up[l], mla_wv_up[l],
                                w_out[l], peer_wq[l], peer_subkeys[l])
        m = mod[l].reshape(MOD_ROWS, N_MOD, 1, D_MODEL)
        sh1, sc1, g1, sh2, sc2, g2 = (m[:, j] for j in range(N_MOD))
        xs, (qm, km, vm, qd, kd, vd) = _qkv(
            xs, peer_out, g2_prev, norm_attn_g[l][None, :], sc1, sh1, w["win"],
            mla_q_norm_g[l][None, :], mla_kv_norm_g[l][None, :], w["wq"], w["wk"], w["wv"], tab)
        om = _attention("mla", qm, km, vm, not last)
        od = _attention("diff", qd, kd, vd, not last, lam=lam_all[l][None, :],
                        g=diff_subnorm_g[l][None, :], lam_scale=1.0 - lam_inits[l])
        n_blocks = NB_LAT if last else NB_ALL
        n = n_blocks * TM
        xs, h2, eidx_t, gate_t = _mid(n_blocks, xs, om, od, w["wo_m"], w["wo_d"], g1,
                                      norm_ffn_g[l][None, :], sc2, sh2, w["pwq"], w["sk"])
        idx = eidx_t.T
        gate3 = gate_t.reshape(16, 8, n // 8, 8).transpose(2, 1, 3, 0).reshape(n // 8, 8, LANES)
        act3 = _peer_u(n, idx, h2.reshape(n, SUBLANES, LANES), gate3, _pack_table(peer_u[l]))
        act = act3.reshape(n // 8, 8, 8, 16).transpose(0, 2, 3, 1).reshape(n, N_SEL)
        peer_out = _peer_v(n, idx, act, _pack_table(peer_v[l])).reshape(n, D_MODEL)
        g2_prev = g2
    out = _final(xs, peer_out, g2_prev, final_norm_g[None, :])
    return out.reshape(BATCH, SEQ, D_MODEL)
```

```python
import functools
import math

import jax
import jax.numpy as jnp
import numpy as np
from jax import lax
from jax.experimental import pallas as pl
from jax.experimental.pallas import tpu as pltpu

F32 = jnp.float32
BF16 = jnp.bfloat16

D_MODEL = 1024
BATCH = 8
SEQ = 4096
DEPTH = 4
GRID_W = 64
CTX_LEN = 256
N_MOD = 6
NORM_EPS = 1e-6
ROPE_BASE = 10000.0
MLA_HEADS = 8
MLA_HEAD_DIM = 64
MLA_ROPE_DIM = 32
MLA_Q_RANK = 384
MLA_KV_RANK = 256
DIFF_HEADS = 4
DIFF_HEAD_DIM = 64
PEER_HEADS = 8
PEER_TOPK = 16
N_KEYS = 128
N_EXPERTS = N_KEYS * N_KEYS

T_LAT = BATCH * SEQ
T_CTX = BATCH * CTX_LEN
TT = T_LAT + T_CTX
TM = 256
NB_LAT = T_LAT // TM
NB_ALL = TT // TM
BLK_PER_BATCH = SEQ // TM
MOD_ROWS = 16
PROJ_PAD = 2304
TQ = 256
TK = 512
PEER_TB = 128
LANES = 128
SUBLANES = 8
VMEM_LIMIT = 56 * 1024 * 1024


def _cparams(sem, vmem=VMEM_LIMIT):
    return pltpu.CompilerParams(dimension_semantics=sem, vmem_limit_bytes=vmem)


def _rms(x):
    return x * lax.rsqrt(jnp.mean(x * x, axis=-1, keepdims=True) + NORM_EPS)


def _bid(i):
    return jnp.where(i < NB_LAT, i // BLK_PER_BATCH, BATCH)


def _posblk(i):
    return jnp.where(i < NB_LAT, i % BLK_PER_BATCH, BLK_PER_BATCH)


def _mod_body(cc_ref, w_ref, b_ref, o_ref):
    cc = cc_ref[...]
    s = cc / (1.0 + jnp.exp(-cc))
    o_ref[0] = jnp.dot(s, w_ref[0], precision=lax.Precision.HIGHEST,
                       preferred_element_type=F32) + b_ref[0]


def _modulation(cc, w_ada, b_ada):
    nj = N_MOD
    return pl.pallas_call(
        _mod_body,
        grid=(DEPTH, nj),
        in_specs=[
            pl.BlockSpec((MOD_ROWS, D_MODEL), lambda l, j: (0, 0)),
            pl.BlockSpec((1, D_MODEL, D_MODEL), lambda l, j: (l, 0, j)),
            pl.BlockSpec((1, 1, D_MODEL), lambda l, j: (l, 0, j)),
        ],
        out_specs=pl.BlockSpec((1, MOD_ROWS, D_MODEL), lambda l, j: (l, 0, j)),
        out_shape=jax.ShapeDtypeStruct((DEPTH, MOD_ROWS, N_MOD * D_MODEL), F32),
        compiler_params=_cparams(("arbitrary", "arbitrary")),
        name="modulation",
    )(cc, w_ada, b_ada.reshape(DEPTH, 1, N_MOD * D_MODEL))


def _lam_body(d0_ref, d1_ref, d2_ref, d3_ref, li_ref, o_ref):
    a = jnp.sum(d0_ref[...] * d1_ref[...], axis=-1, keepdims=True)
    b = jnp.sum(d2_ref[...] * d3_ref[...], axis=-1, keepdims=True)
    o_ref[...] = jnp.exp(a) - jnp.exp(b) + li_ref[...]


def _diff_lambda(diff_lambda, lam_init):
    dl = diff_lambda.astype(F32)
    return pl.pallas_call(
        _lam_body,
        out_shape=jax.ShapeDtypeStruct((DEPTH, LANES), F32),
        name="diff_lambda",
    )(dl[:, 0], dl[:, 1], dl[:, 2], dl[:, 3], lam_init)


def _rope(xb, c, sp, sm, shift):
    return xb * c + pltpu.roll(xb, shift, 1) * sp + pltpu.roll(xb, LANES - shift, 1) * sm


def _qkv_body(has_peer, *refs):
    if has_peer:
        x_ref, peer_ref, g2_ref = refs[:3]
        refs = refs[3:]
    else:
        x_ref = refs[0]
        refs = refs[1:]
    (ng_ref, sc_ref, sh_ref, win_ref, gq_ref, gkv_ref, wq_ref, wk_ref, wv_ref, tab_ref) = refs[:10]
    outs = refs[10:]
    if has_peer:
        xo_ref, outs = outs[0], outs[1:]
    qm_ref, km_ref, vm_ref, qd_ref, kd_ref, vd_ref = outs

    x = x_ref[...]
    if has_peer:
        x = x + g2_ref[0] * peer_ref[...]
        xo_ref[...] = x
    h = _rms(x) * ng_ref[...] * (1.0 + sc_ref[0]) + sh_ref[0]
    proj = jnp.dot(h.astype(BF16), win_ref[...], preferred_element_type=F32)
    cq = _rms(proj[:, 0:384]) * gq_ref[...]
    ckv = _rms(proj[:, 384:640]) * gkv_ref[...]
    kr = proj[:, 640:768]
    q = jnp.dot(cq.astype(BF16), wq_ref[...], preferred_element_type=F32)
    ckv16 = ckv.astype(BF16)
    k = jnp.dot(ckv16, wk_ref[...], preferred_element_type=F32)
    v = jnp.dot(ckv16, wv_ref[...], preferred_element_type=F32)
    tab = tab_ref[...]
    cm, spm, smm = tab[:, 0:128], tab[:, 128:256], tab[:, 256:384]
    cd, spd, smd = tab[:, 384:512], tab[:, 512:640], tab[:, 640:768]
    kr_rot = _rope(kr, cm, spm, smm, MLA_ROPE_DIM // 2)
    scale_m = (MLA_HEAD_DIM + MLA_ROPE_DIM) ** -0.5
    for hh in range(MLA_HEADS):
        sl = slice(hh * LANES, (hh + 1) * LANES)
        qm_ref[:, sl] = (_rope(q[:, sl], cm, spm, smm, MLA_ROPE_DIM // 2) * scale_m).astype(BF16)
        km_ref[:, sl] = (k[:, sl] + kr_rot).astype(BF16)
    vm_ref[...] = v.astype(BF16)
    scale_d = DIFF_HEAD_DIM ** -0.5
    for hh in range(DIFF_HEADS):
        sl = slice(hh * LANES, (hh + 1) * LANES)
        qd = proj[:, 768 + hh * LANES:768 + (hh + 1) * LANES]
        kd = proj[:, 1280 + hh * LANES:1280 + (hh + 1) * LANES]
        qd_ref[:, sl] = (_rope(qd, cd, spd, smd, DIFF_HEAD_DIM // 2) * scale_d).astype(BF16)
        kd_ref[:, sl] = _rope(kd, cd, spd, smd, DIFF_HEAD_DIM // 2).astype(BF16)
    vd_ref[...] = proj[:, 1792:2304].astype(BF16)


def _qkv(x, peer, g2, ng, sc, sh, win, gq, gkv, wq, wk, wv, tab):
    has_peer = peer is not None
    row = lambda i: (i, 0)
    modrow = lambda i: (_bid(i), 0, 0)
    const = lambda i: (0, 0)
    in_specs = [pl.BlockSpec((TM, D_MODEL), row)]
    args = [x]
    if has_peer:
        in_specs += [pl.BlockSpec((TM, D_MODEL), row), pl.BlockSpec((1, 1, D_MODEL), modrow)]
        args += [peer, g2]
    in_specs += [
        pl.BlockSpec((1, D_MODEL), const),
        pl.BlockSpec((1, 1, D_MODEL), modrow),
        pl.BlockSpec((1, 1, D_MODEL), modrow),
        pl.BlockSpec((D_MODEL, PROJ_PAD), const),
        pl.BlockSpec((1, MLA_Q_RANK), const),
        pl.BlockSpec((1, MLA_KV_RANK), const),
        pl.BlockSpec((MLA_Q_RANK, MLA_HEADS * LANES), const),
        pl.BlockSpec((MLA_KV_RANK, MLA_HEADS * LANES), const),
        pl.BlockSpec((MLA_KV_RANK, 512), const),
        pl.BlockSpec((TM, 768), lambda i: (_posblk(i), 0)),
    ]
    args += [ng, sc, sh, win, gq, gkv, wq, wk, wv, tab]
    out_specs, out_shape = [], []
    if has_peer:
        out_specs.append(pl.BlockSpec((TM, D_MODEL), row))
        out_shape.append(jax.ShapeDtypeStruct((TT, D_MODEL), F32))
    for width in (1024, 1024, 512, 512, 512, 512):
        out_specs.append(pl.BlockSpec((TM, width), row))
        out_shape.append(jax.ShapeDtypeStruct((TT, width), BF16))
    res = pl.pallas_call(
        functools.partial(_qkv_body, has_peer),
        grid=(NB_ALL,),
        in_specs=in_specs,
        out_specs=out_specs,
        out_shape=out_shape,
        compiler_params=_cparams(("parallel",)),
        name="qkv",
    )(*args)
    if has_peer:
        return res[0], res[1:]
    return x, res


def _flash_update(q, k, v, carry):
    m, l, acc = carry
    s = lax.dot_general(q, k, (((1,), (1,)), ((), ())), preferred_element_type=F32)
    m_new = jnp.maximum(m, jnp.max(s, axis=1, keepdims=True))
    alpha = jnp.exp(m - m_new)
    p = jnp.exp(s - m_new)
    l = alpha * l + jnp.sum(p, axis=1, keepdims=True)
    acc = alpha * acc + jnp.dot(p.astype(BF16), v, preferred_element_type=F32)
    return m_new, l, acc


def _flash_head(q, kc_ref, kl_ref, vc_ref, vl_ref, ksl, n_chunks):
    init = (jnp.full((TQ, 1), -jnp.inf, F32), jnp.zeros((TQ, 1), F32), jnp.zeros((TQ, LANES), F32))
    carry = _flash_update(q, kc_ref[:, ksl], vc_ref[...], init)

    def chunk(c, carry):
        r0 = pl.multiple_of(c * TK, TK)
        return _flash_update(q, kl_ref[pl.ds(r0, TK), ksl], vl_ref[pl.ds(r0, TK), :], carry)

    m, l, acc = lax.fori_loop(0, n_chunks, chunk, carry)
    return acc / l


def _n_chunks(nq):
    return jnp.where(pl.program_id(2) < nq, SEQ // TK, 0)


def _attn_mla_body(nq, q_ref, kc_ref, kl_ref, vc_ref, vl_ref, o_ref):
    n_chunks = _n_chunks(nq)
    lane = lax.broadcasted_iota(jnp.int32, (TQ, LANES), 1)
    o0 = _flash_head(q_ref[:, 0:LANES], kc_ref, kl_ref, vc_ref, vl_ref, slice(0, LANES), n_chunks)
    o1 = _flash_head(q_ref[:, LANES:2 * LANES], kc_ref, kl_ref, vc_ref, vl_ref,
                     slice(LANES, 2 * LANES), n_chunks)
    o_ref[...] = jnp.where(lane < MLA_HEAD_DIM, o0, o1).astype(o_ref.dtype)


def _attn_diff_body(nq, lam_scale, q_ref, kc_ref, kl_ref, vc_ref, vl_ref, lam_ref, g_ref, o_ref):
    n_chunks = _n_chunks(nq)
    lane = lax.broadcasted_iota(jnp.int32, (TQ, LANES), 1)
    q = q_ref[...]
    zero = jnp.zeros_like(q)
    q0 = jnp.where(lane < DIFF_HEAD_DIM, q, zero)
    q1 = jnp.where(lane < DIFF_HEAD_DIM, zero, q)
    o0 = _flash_head(q0, kc_ref, kl_ref, vc_ref, vl_ref, slice(0, LANES), n_chunks)
    o1 = _flash_head(q1, kc_ref, kl_ref, vc_ref, vl_ref, slice(0, LANES), n_chunks)
    o = o0 - lam_ref[...] * o1
    o_ref[...] = (_rms(o) * g_ref[...] * lam_scale).astype(o_ref.dtype)


def _attention(kind, q, k, v, with_ctx, lam=None, g=None, lam_scale=None):
    nq = SEQ // TQ
    steps = nq + 1 if with_ctx else nq
    qw = 2 * LANES if kind == "mla" else LANES
    ctx0 = T_LAT // CTX_LEN

    def qrow(b, p, i):
        return (jnp.where(i < nq, b * nq + i, (T_LAT // TQ) + b), p)

    in_specs = [
        pl.BlockSpec((TQ, qw), qrow),
        pl.BlockSpec((CTX_LEN, qw), lambda b, p, i: (ctx0 + b, p)),
        pl.BlockSpec((SEQ, qw), lambda b, p, i: (b, p)),
        pl.BlockSpec((CTX_LEN, LANES), lambda b, p, i: (ctx0 + b, p)),
        pl.BlockSpec((SEQ, LANES), lambda b, p, i: (b, p)),
    ]
    args = [q, k, k, v, v]
    if kind == "mla":
        body = functools.partial(_attn_mla_body, nq)
    else:
        body = functools.partial(_attn_diff_body, nq, lam_scale)
        in_specs += [pl.BlockSpec((1, LANES), lambda b, p, i: (0, 0)),
                     pl.BlockSpec((1, LANES), lambda b, p, i: (0, 0))]
        args += [lam, g]
    rows = TT if with_ctx else T_LAT
    return pl.pallas_call(
        body,
        grid=(BATCH, 4, steps),
        in_specs=in_specs,
        out_specs=pl.BlockSpec((TQ, LANES), qrow),
        out_shape=jax.ShapeDtypeStruct((rows, 512), BF16),
        compiler_params=_cparams(("parallel", "parallel", "arbitrary")),
        name="attn_" + kind,
    )(*args)


def _top16(s, n_rows):
    iota = lax.broadcasted_iota(jnp.int32, s.shape, 0)
    slot = lax.broadcasted_iota(jnp.int32, (PEER_TOPK, s.shape[1]), 0)
    vals = jnp.zeros((PEER_TOPK, s.shape[1]), F32)
    idxs = jnp.zeros((PEER_TOPK, s.shape[1]), jnp.int32)
    for r in range(PEER_TOPK):
        m = jnp.max(s, axis=0, keepdims=True)
        am = jnp.min(jnp.where(s == m, iota, n_rows), axis=0, keepdims=True)
        vals = jnp.where(slot == r, m, vals)
        idxs = jnp.where(slot == r, am, idxs)
        s = jnp.where(iota == am, -jnp.inf, s)
    return vals, idxs


def _mid_body(x_ref, om_ref, od_ref, wo_m_ref, wo_d_ref, g1_ref, ng_ref, sc_ref, sh_ref,
              wq_ref, sk_ref, xo_ref, h_ref, eidx_ref, gate_ref, q_scr):
    y = (jnp.dot(om_ref[...], wo_m_ref[...], preferred_element_type=F32)
         + jnp.dot(od_ref[...], wo_d_ref[...], preferred_element_type=F32))
    x = x_ref[...] + g1_ref[0] * y
    xo_ref[...] = x
    h = _rms(x) * ng_ref[...] * (1.0 + sc_ref[0]) + sh_ref[0]
    h_ref[...] = h
    q_scr[...] = jnp.dot(h.astype(BF16), wq_ref[...], preferred_element_type=F32).astype(BF16)

    def head(hh, _):
        sv, si = [], []
        for c in range(2):
            c0 = pl.multiple_of(hh * 2 * N_KEYS + c * N_KEYS, N_KEYS)
            qh = q_scr[:, pl.ds(c0, N_KEYS)]
            s = lax.dot_general(sk_ref[c], qh, (((1,), (1,)), ((), ())), preferred_element_type=F32)
            vals, idxs = _top16(s, N_KEYS)
            sv.append(vals)
            si.append(idxs)
        cand_s = jnp.concatenate([sv[0][a:a + 1, :] + sv[1] for a in range(PEER_TOPK)], axis=0)
        cand_i = jnp.concatenate([si[0][a:a + 1, :] * N_KEYS + si[1] for a in range(PEER_TOPK)], axis=0)
        n_cand = PEER_TOPK * PEER_TOPK
        iota = lax.broadcasted_iota(jnp.int32, cand_s.shape, 0)
        slot = lax.broadcasted_iota(jnp.int32, (PEER_TOPK, TM), 0)
        top_s = jnp.zeros((PEER_TOPK, TM), F32)
        top_i = jnp.zeros((PEER_TOPK, TM), jnp.int32)
        for r in range(PEER_TOPK):
            m = jnp.max(cand_s, axis=0, keepdims=True)
            am = jnp.min(jnp.where(cand_s == m, iota, n_cand), axis=0, keepdims=True)
            hit = iota == am
            e = jnp.max(jnp.where(hit, cand_i, -1), axis=0, keepdims=True)
            top_s = jnp.where(slot == r, m, top_s)
            top_i = jnp.where(slot == r, e, top_i)
            cand_s = jnp.where(hit, -jnp.inf, cand_s)
        ex = jnp.exp(top_s - jnp.max(top_s, axis=0, keepdims=True))
        gate = ex / jnp.sum(ex, axis=0, keepdims=True)
        r0 = pl.multiple_of(hh * PEER_TOPK, PEER_TOPK)
        eidx_ref[pl.ds(r0, PEER_TOPK), :] = top_i
        gate_ref[pl.ds(r0, PEER_TOPK), :] = gate
        return 0

    lax.fori_loop(0, PEER_HEADS, head, 0)


def _mid(n_blocks, x, om, od, wo_m, wo_d, g1, ng, sc, sh, wq, sk):
    n = n_blocks * TM
    row = lambda i: (i, 0)
    modrow = lambda i: (_bid(i), 0, 0)
    const = lambda i: (0, 0)
    return pl.pallas_call(
        _mid_body,
        grid=(n_blocks,),
        in_specs=[
            pl.BlockSpec((TM, D_MODEL), row),
            pl.BlockSpec((TM, 512), row),
            pl.BlockSpec((TM, 512), row),
            pl.BlockSpec((512, D_MODEL), const),
            pl.BlockSpec((512, D_MODEL), const),
            pl.BlockSpec((1, 1, D_MODEL), modrow),
            pl.BlockSpec((1, D_MODEL), const),
            pl.BlockSpec((1, 1, D_MODEL), modrow),
            pl.BlockSpec((1, 1, D_MODEL), modrow),
            pl.BlockSpec((D_MODEL, PEER_HEADS * 2 * N_KEYS), const),
            pl.BlockSpec((2, N_KEYS, N_KEYS), lambda i: (0, 0, 0)),
        ],
        out_specs=[
            pl.BlockSpec((TM, D_MODEL), row),
            pl.BlockSpec((TM, D_MODEL), row),
            pl.BlockSpec((PEER_HEADS * PEER_TOPK, TM), lambda i: (0, i)),
            pl.BlockSpec((PEER_HEADS * PEER_TOPK, TM), lambda i: (0, i)),
        ],
        out_shape=[
            jax.ShapeDtypeStruct((n, D_MODEL), F32),
            jax.ShapeDtypeStruct((n, D_MODEL), F32),
            jax.ShapeDtypeStruct((PEER_HEADS * PEER_TOPK, n), jnp.int32),
            jax.ShapeDtypeStruct((PEER_HEADS * PEER_TOPK, n), F32),
        ],
        scratch_shapes=[pltpu.VMEM((TM, PEER_HEADS * 2 * N_KEYS), BF16)],
        compiler_params=_cparams(("parallel",)),
        name="mid",
    )(x, om, od, wo_m, wo_d, g1, ng, sc, sh, wq, sk)


N_SEL = PEER_HEADS * PEER_TOPK
HALF = D_MODEL // 2
_BFLY_ORDER = (0, 4, 2, 6, 1, 5, 3, 7)


def _pack_table(tab):
    u = lax.bitcast_convert_type(tab.astype(BF16), jnp.uint16).astype(jnp.uint32)
    w = u[:, :HALF] | (u[:, HALF:] << 16)
    return w.reshape(tab.shape[0], HALF // LANES, LANES)


def _load_pair(tab_ref, ea, eb):
    w = jnp.concatenate([tab_ref[ea], tab_ref[eb]], axis=0)
    lo = pltpu.bitcast(w << 16, F32)
    hi = pltpu.bitcast(w & jnp.uint32(0xFFFF0000), F32)
    return lo, hi


def _peer_u_body(idx_ref, x_ref, gate_ref, tab_ref, act_ref, r_scr):
    sub = lax.broadcasted_iota(jnp.int32, (SUBLANES, LANES), 0)
    lane = lax.broadcasted_iota(jnp.int32, (SUBLANES, LANES), 1)
    top = sub < 4
    n_grp = N_SEL // SUBLANES

    def fold(p, q, width):
        tp = p + pltpu.roll(p, SUBLANES - width, 0)
        tq = q + pltpu.roll(q, SUBLANES - width, 0)
        return jnp.where((sub % (2 * width)) < width, tp, pltpu.roll(tq, width, 0))

    def tile(tb, _):
        def token(tt, _):
            t = tb * SUBLANES + tt
            xt = x_ref[t]
            xsw = pltpu.roll(xt, 4, 0)
            xlo = jnp.where(top, xt, xsw)
            xhi = jnp.where(top, xsw, xt)
            for g in range(n_grp):
                parts = []
                for pr in range(4):
                    ea = idx_ref[t, g * SUBLANES + _BFLY_ORDER[2 * pr]]
                    eb = idx_ref[t, g * SUBLANES + _BFLY_ORDER[2 * pr + 1]]
                    lo, hi = _load_pair(tab_ref, ea, eb)
                    parts.append(lo * xlo + hi * xhi)
                r_scr[tt * n_grp + g] = fold(fold(parts[0], parts[1], 2), fold(parts[2], parts[3], 2), 1)
            return 0

        lax.fori_loop(0, SUBLANES, token, 0)
        a = jnp.zeros((SUBLANES, LANES), F32)
        for i in range(SUBLANES * n_grp):
            a = jnp.where(lane == i, jnp.sum(r_scr[i], axis=1, keepdims=True), a)
        act_ref[tb] = 0.5 * a * (1.0 + lax.erf(a * (2.0 ** -0.5))) * gate_ref[tb]
        return 0

    lax.fori_loop(0, PEER_TB // SUBLANES, tile, 0)


def _peer_v_body(idx_ref, act_ref, tab_ref, o_ref):
    sub = lax.broadcasted_iota(jnp.int32, (SUBLANES, LANES), 0)
    lane = lax.broadcasted_iota(jnp.int32, (SUBLANES, LANES), 1)
    top = sub < 4
    n_grp = N_SEL // SUBLANES
    pair_rows = [jnp.where(top, 2 * p, 2 * p + 1) for p in range(4)]

    def tile(tb, _):
        a_tile = act_ref[tb]

        def token(tt, _):
            t = tb * SUBLANES + tt
            acc = [jnp.zeros((SUBLANES, LANES), F32) for _ in range(4)]
            for g in range(n_grp):
                col = jnp.sum(jnp.where(lane == tt * n_grp + g, a_tile, 0.0), axis=1, keepdims=True)
                actg = jnp.broadcast_to(col, (SUBLANES, LANES))
                for p in range(4):
                    ea = idx_ref[t, g * SUBLANES + 2 * p]
                    eb = idx_ref[t, g * SUBLANES + 2 * p + 1]
                    lo, hi = _load_pair(tab_ref, ea, eb)
                    av = jnp.take_along_axis(actg, pair_rows[p], axis=0)
                    k = 2 * (p % 2)
                    acc[k] = acc[k] + av * lo
                    acc[k + 1] = acc[k + 1] + av * hi
            lo = acc[0] + acc[2]
            hi = acc[1] + acc[3]
            lo = lo + pltpu.roll(lo, 4, 0)
            hi = hi + pltpu.roll(hi, 4, 0)
            o_ref[t] = jnp.where(top, lo, hi)
            return 0

        lax.fori_loop(0, SUBLANES, token, 0)
        return 0

    lax.fori_loop(0, PEER_TB // SUBLANES, tile, 0)


def _table_spec():
    return pl.BlockSpec((N_EXPERTS, HALF // LANES, LANES), lambda i: (0, 0, 0),
                        pipeline_mode=pl.Buffered(1))


def _peer_u(n, idx, x3, gate3, tab):
    return pl.pallas_call(
        _peer_u_body,
        grid=(n // PEER_TB,),
        in_specs=[
            pl.BlockSpec((PEER_TB, N_SEL), lambda i: (i, 0), memory_space=pltpu.SMEM),
            pl.BlockSpec((PEER_TB, SUBLANES, LANES), lambda i: (i, 0, 0)),
            pl.BlockSpec((PEER_TB // SUBLANES, SUBLANES, LANES), lambda i: (i, 0, 0)),
            _table_spec(),
        ],
        out_specs=pl.BlockSpec((PEER_TB // SUBLANES, SUBLANES, LANES), lambda i: (i, 0, 0)),
        out_shape=jax.ShapeDtypeStruct((n // SUBLANES, SUBLANES, LANES), F32),
        scratch_shapes=[pltpu.VMEM((N_SEL, SUBLANES, LANES), F32)],
        compiler_params=_cparams(("arbitrary",)),
        name="peer_u",
    )(idx, x3, gate3, tab)


def _peer_v(n, idx, act3, tab):
    return pl.pallas_call(
        _peer_v_body,
        grid=(n // PEER_TB,),
        in_specs=[
            pl.BlockSpec((PEER_TB, N_SEL), lambda i: (i, 0), memory_space=pltpu.SMEM),
            pl.BlockSpec((PEER_TB // SUBLANES, SUBLANES, LANES), lambda i: (i, 0, 0)),
            _table_spec(),
        ],
        out_specs=pl.BlockSpec((PEER_TB, SUBLANES, LANES), lambda i: (i, 0, 0)),
        out_shape=jax.ShapeDtypeStruct((n, SUBLANES, LANES), F32),
        compiler_params=_cparams(("arbitrary",)),
        name="peer_v",
    )(idx, act3, tab)


def _final_body(x_ref, peer_ref, g2_ref, g_ref, o_ref):
    x = x_ref[...] + g2_ref[0] * peer_ref[...]
    o_ref[...] = _rms(x) * g_ref[...]


def _final(x, peer, g2, g):
    row = lambda i: (i, 0)
    return pl.pallas_call(
        _final_body,
        grid=(NB_LAT,),
        in_specs=[
            pl.BlockSpec((TM, D_MODEL), row),
            pl.BlockSpec((TM, D_MODEL), row),
            pl.BlockSpec((1, 1, D_MODEL), lambda i: (_bid(i), 0, 0)),
            pl.BlockSpec((1, D_MODEL), lambda i: (0, 0)),
        ],
        out_specs=pl.BlockSpec((TM, D_MODEL), row),
        out_shape=jax.ShapeDtypeStruct((T_LAT, D_MODEL), F32),
        compiler_params=_cparams(("parallel",)),
        name="final_norm",
    )(x, peer, g2, g)


def _deinterleave(width):
    return np.concatenate([np.arange(0, width, 2), np.arange(1, width, 2)])


def _rope_tables():
    pos = np.arange(SEQ)
    rowp = jnp.asarray(pos // GRID_W, F32)
    colp = jnp.asarray(pos % GRID_W, F32)

    def angles(dim):
        quarter = dim // 4
        inv = ROPE_BASE ** (-jnp.arange(quarter, dtype=F32) / quarter)
        return jnp.concatenate([rowp[:, None] * inv, colp[:, None] * inv], axis=-1)

    am = angles(MLA_ROPE_DIM)
    ad = angles(DIFF_HEAD_DIM)
    cm, sm_ = jnp.cos(am), jnp.sin(am)
    cd, sd = jnp.cos(ad), jnp.sin(ad)
    one = lambda w: jnp.ones((SEQ, w), F32)
    zero = lambda w: jnp.zeros((SEQ, w), F32)
    t_cm = jnp.concatenate([one(64), cm, cm, one(32)], axis=1)
    t_spm = jnp.concatenate([zero(80), sm_, zero(32)], axis=1)
    t_smm = jnp.concatenate([zero(64), -sm_, zero(48)], axis=1)
    t_cd = jnp.concatenate([cd, cd, cd, cd], axis=1)
    t_spd = jnp.concatenate([zero(32), sd, zero(32), sd], axis=1)
    t_smd = jnp.concatenate([-sd, zero(32), -sd, zero(32)], axis=1)
    lat = jnp.concatenate([t_cm, t_spm, t_smm, t_cd, t_spd, t_smd], axis=1)
    ident = jnp.concatenate([jnp.ones((CTX_LEN, 128), F32), jnp.zeros((CTX_LEN, 256), F32),
                             jnp.ones((CTX_LEN, 128), F32), jnp.zeros((CTX_LEN, 256), F32)], axis=1)
    return jnp.concatenate([lat, ident], axis=0)


def _prep_layer_weights(w_in, wq_up, wq_rope, wk_up, wv_up, w_out, peer_wq, peer_subkeys):
    p32 = _deinterleave(MLA_ROPE_DIM)
    p64 = _deinterleave(DIFF_HEAD_DIM)
    z = lambda w: jnp.zeros((D_MODEL, w), F32)
    kr = w_in[:, 640:672][:, p32]
    qd = w_in[:, 672:1184].reshape(D_MODEL, 8, 64)[:, :, p64].reshape(D_MODEL, 512)
    kd = w_in[:, 1184:1696].reshape(D_MODEL, 8, 64)[:, :, p64].reshape(D_MODEL, 512)
    win = jnp.concatenate([w_in[:, 0:640], z(64), kr, z(32), qd, kd, w_in[:, 1696:2208]], axis=1)
    qn = wq_up.reshape(MLA_Q_RANK, MLA_HEADS, MLA_HEAD_DIM)
    qr = wq_rope.reshape(MLA_Q_RANK, MLA_HEADS, MLA_ROPE_DIM)[:, :, p32]
    wq = jnp.concatenate([qn, qr, jnp.zeros((MLA_Q_RANK, MLA_HEADS, 32), F32)], axis=2)
    kn = wk_up.reshape(MLA_KV_RANK, MLA_HEADS, MLA_HEAD_DIM)
    wk = jnp.concatenate([kn, jnp.zeros((MLA_KV_RANK, MLA_HEADS, 64), F32)], axis=2)
    return dict(
        win=win.astype(BF16),
        wq=wq.reshape(MLA_Q_RANK, MLA_HEADS * LANES).astype(BF16),
        wk=wk.reshape(MLA_KV_RANK, MLA_HEADS * LANES).astype(BF16),
        wv=wv_up.astype(BF16),
        wo_m=w_out[:512].astype(BF16),
        wo_d=w_out[512:].astype(BF16),
        pwq=peer_wq.astype(BF16),
        sk=peer_subkeys.astype(BF16),
    )


def kernel(x, c, ctx, c_ctx, norm_attn_g, norm_ffn_g, w_ada, b_ada, w_in, mla_q_norm_g, mla_wq_up, mla_wq_rope, mla_kv_norm_g, mla_wk_up, mla_wv_up, diff_lambda, diff_subnorm_g, w_out, peer_wq, peer_subkeys, peer_u, peer_v, final_norm_g):
    xs = jnp.concatenate([x.reshape(T_LAT, D_MODEL), ctx.reshape(T_CTX, D_MODEL)], axis=0)
    cc = jnp.concatenate([c, c_ctx[None, :], jnp.zeros((MOD_ROWS - BATCH - 1, D_MODEL), F32)], axis=0)
    mod = _modulation(cc, w_ada, b_ada)
    lam_inits = [0.8 - 0.6 * math.exp(-0.3 * l) for l in range(DEPTH)]
    lam_all = _diff_lambda(diff_lambda, jnp.broadcast_to(jnp.asarray(lam_inits, F32)[:, None], (DEPTH, LANES)))
    tab = _rope_tables()

    peer_out = None
    g2_prev = None
    for l in range(DEPTH):
        last = l == DEPTH - 1
        w = _prep_layer_weights(w_in[l], mla_wq_up[l], mla_wq_rope[l], mla_wk_up[l], mla_wv_up[l],
                                w_out[l], peer_wq[l], peer_subkeys[l])
        m = mod[l].reshape(MOD_ROWS, N_MOD, 1, D_MODEL)
        sh1, sc1, g1, sh2, sc2, g2 = (m[:, j] for j in range(N_MOD))
        xs, (qm, km, vm, qd, kd, vd) = _qkv(
            xs, peer_out, g2_prev, norm_attn_g[l][None, :], sc1, sh1, w["win"],
            mla_q_norm_g[l][None, :], mla_kv_norm_g[l][None, :], w["wq"], w["wk"], w["wv"], tab)
        om = _attention("mla", qm, km, vm, not last)
        od = _attention("diff", qd, kd, vd, not last, lam=lam_all[l][None, :],
                        g=diff_subnorm_g[l][None, :], lam_scale=1.0 - lam_inits[l])
        n_blocks = NB_LAT if last else NB_ALL
        n = n_blocks * TM
        xs, h2, eidx_t, gate_t = _mid(n_blocks, xs, om, od, w["wo_m"], w["wo_d"], g1,
                                      norm_ffn_g[l][None, :], sc2, sh2, w["pwq"], w["sk"])
        idx = eidx_t.T
        gate3 = gate_t.reshape(16, 8, n // 8, 8).transpose(2, 1, 3, 0).reshape(n // 8, 8, LANES)
        act3 = _peer_u(n, idx, h2.reshape(n, SUBLANES, LANES), gate3, _pack_table(peer_u[l]))
        peer_out = _peer_v(n, idx, act3, _pack_table(peer_v[l])).reshape(n, D_MODEL)
        g2_prev = g2
    out = _final(xs, peer_out, g2_prev, final_norm_g[None, :])
    return out.reshape(BATCH, SEQ, D_MODEL)
```

```python
import functools
import math

import jax
import jax.numpy as jnp
import numpy as np
from jax import lax
from jax.experimental import pallas as pl
from jax.experimental.pallas import tpu as pltpu

F32 = jnp.float32
BF16 = jnp.bfloat16

D_MODEL = 1024
BATCH = 8
SEQ = 4096
DEPTH = 4
GRID_W = 64
CTX_LEN = 256
N_MOD = 6
NORM_EPS = 1e-6
ROPE_BASE = 10000.0
MLA_HEADS = 8
MLA_HEAD_DIM = 64
MLA_ROPE_DIM = 32
MLA_Q_RANK = 384
MLA_KV_RANK = 256
DIFF_HEADS = 4
DIFF_HEAD_DIM = 64
PEER_HEADS = 8
PEER_TOPK = 16
N_KEYS = 128
N_EXPERTS = N_KEYS * N_KEYS

T_LAT = BATCH * SEQ
T_CTX = BATCH * CTX_LEN
TT = T_LAT + T_CTX
TM = 256
NB_LAT = T_LAT // TM
NB_ALL = TT // TM
BLK_PER_BATCH = SEQ // TM
MOD_ROWS = 16
PROJ_PAD = 2304
TQ = 256
TK = 512
PEER_TB = 128
LANES = 128
SUBLANES = 8
VMEM_LIMIT = 56 * 1024 * 1024


def _cparams(sem, vmem=VMEM_LIMIT):
    return pltpu.CompilerParams(dimension_semantics=sem, vmem_limit_bytes=vmem)


def _rms(x):
    return x * lax.rsqrt(jnp.mean(x * x, axis=-1, keepdims=True) + NORM_EPS)


def _bid(i):
    return jnp.where(i < NB_LAT, i // BLK_PER_BATCH, BATCH)


def _posblk(i):
    return jnp.where(i < NB_LAT, i % BLK_PER_BATCH, BLK_PER_BATCH)


def _mod_body(cc_ref, w_ref, b_ref, o_ref):
    cc = cc_ref[...]
    s = cc / (1.0 + jnp.exp(-cc))
    o_ref[0] = jnp.dot(s, w_ref[0], precision=lax.Precision.HIGHEST,
                       preferred_element_type=F32) + b_ref[0]


def _modulation(cc, w_ada, b_ada):
    nj = N_MOD
    return pl.pallas_call(
        _mod_body,
        grid=(DEPTH, nj),
        in_specs=[
            pl.BlockSpec((MOD_ROWS, D_MODEL), lambda l, j: (0, 0)),
            pl.BlockSpec((1, D_MODEL, D_MODEL), lambda l, j: (l, 0, j)),
            pl.BlockSpec((1, 1, D_MODEL), lambda l, j: (l, 0, j)),
        ],
        out_specs=pl.BlockSpec((1, MOD_ROWS, D_MODEL), lambda l, j: (l, 0, j)),
        out_shape=jax.ShapeDtypeStruct((DEPTH, MOD_ROWS, N_MOD * D_MODEL), F32),
        compiler_params=_cparams(("arbitrary", "arbitrary")),
        name="modulation",
    )(cc, w_ada, b_ada.reshape(DEPTH, 1, N_MOD * D_MODEL))


def _lam_body(d0_ref, d1_ref, d2_ref, d3_ref, li_ref, o_ref):
    a = jnp.sum(d0_ref[...] * d1_ref[...], axis=-1, keepdims=True)
    b = jnp.sum(d2_ref[...] * d3_ref[...], axis=-1, keepdims=True)
    o_ref[...] = jnp.exp(a) - jnp.exp(b) + li_ref[...]


def _diff_lambda(diff_lambda, lam_init):
    dl = diff_lambda.astype(F32)
    return pl.pallas_call(
        _lam_body,
        out_shape=jax.ShapeDtypeStruct((DEPTH, LANES), F32),
        name="diff_lambda",
    )(dl[:, 0], dl[:, 1], dl[:, 2], dl[:, 3], lam_init)


def _rope(xb, c, sp, sm, shift):
    return xb * c + pltpu.roll(xb, shift, 1) * sp + pltpu.roll(xb, LANES - shift, 1) * sm


def _qkv_body(has_peer, *refs):
    if has_peer:
        x_ref, peer_ref, g2_ref = refs[:3]
        refs = refs[3:]
    else:
        x_ref = refs[0]
        refs = refs[1:]
    (ng_ref, sc_ref, sh_ref, win_ref, gq_ref, gkv_ref, wq_ref, wk_ref, wv_ref, tab_ref) = refs[:10]
    outs = refs[10:]
    if has_peer:
        xo_ref, outs = outs[0], outs[1:]
    qm_ref, km_ref, vm_ref, qd_ref, kd_ref, vd_ref = outs

    x = x_ref[...]
    if has_peer:
        x = x + g2_ref[0] * peer_ref[...]
        xo_ref[...] = x
    h = _rms(x) * ng_ref[...] * (1.0 + sc_ref[0]) + sh_ref[0]
    proj = jnp.dot(h.astype(BF16), win_ref[...], preferred_element_type=F32)
    cq = _rms(proj[:, 0:384]) * gq_ref[...]
    ckv = _rms(proj[:, 384:640]) * gkv_ref[...]
    kr = proj[:, 640:768]
    q = jnp.dot(cq.astype(BF16), wq_ref[...], preferred_element_type=F32)
    ckv16 = ckv.astype(BF16)
    k = jnp.dot(ckv16, wk_ref[...], preferred_element_type=F32)
    v = jnp.dot(ckv16, wv_ref[...], preferred_element_type=F32)
    tab = tab_ref[...]
    cm, spm, smm = tab[:, 0:128], tab[:, 128:256], tab[:, 256:384]
    cd, spd, smd = tab[:, 384:512], tab[:, 512:640], tab[:, 640:768]
    kr_rot = _rope(kr, cm, spm, smm, MLA_ROPE_DIM // 2)
    scale_m = (MLA_HEAD_DIM + MLA_ROPE_DIM) ** -0.5
    for hh in range(MLA_HEADS):
        sl = slice(hh * LANES, (hh + 1) * LANES)
        qm_ref[:, sl] = (_rope(q[:, sl], cm, spm, smm, MLA_ROPE_DIM // 2) * scale_m).astype(BF16)
        km_ref[:, sl] = (k[:, sl] + kr_rot).astype(BF16)
    vm_ref[...] = v.T.astype(BF16)
    scale_d = DIFF_HEAD_DIM ** -0.5
    for hh in range(DIFF_HEADS):
        sl = slice(hh * LANES, (hh + 1) * LANES)
        qd = proj[:, 768 + hh * LANES:768 + (hh + 1) * LANES]
        kd = proj[:, 1280 + hh * LANES:1280 + (hh + 1) * LANES]
        qd_ref[:, sl] = (_rope(qd, cd, spd, smd, DIFF_HEAD_DIM // 2) * scale_d).astype(BF16)
        kd_ref[:, sl] = _rope(kd, cd, spd, smd, DIFF_HEAD_DIM // 2).astype(BF16)
    vd_ref[...] = proj[:, 1792:2304].T.astype(BF16)


def _qkv(x, peer, g2, ng, sc, sh, win, gq, gkv, wq, wk, wv, tab):
    has_peer = peer is not None
    row = lambda i: (i, 0)
    modrow = lambda i: (_bid(i), 0, 0)
    const = lambda i: (0, 0)
    in_specs = [pl.BlockSpec((TM, D_MODEL), row)]
    args = [x]
    if has_peer:
        in_specs += [pl.BlockSpec((TM, D_MODEL), row), pl.BlockSpec((1, 1, D_MODEL), modrow)]
        args += [peer, g2]
    in_specs += [
        pl.BlockSpec((1, D_MODEL), const),
        pl.BlockSpec((1, 1, D_MODEL), modrow),
        pl.BlockSpec((1, 1, D_MODEL), modrow),
        pl.BlockSpec((D_MODEL, PROJ_PAD), const),
        pl.BlockSpec((1, MLA_Q_RANK), const),
        pl.BlockSpec((1, MLA_KV_RANK), const),
        pl.BlockSpec((MLA_Q_RANK, MLA_HEADS * LANES), const),
        pl.BlockSpec((MLA_KV_RANK, MLA_HEADS * LANES), const),
        pl.BlockSpec((MLA_KV_RANK, 512), const),
        pl.BlockSpec((TM, 768), lambda i: (_posblk(i), 0)),
    ]
    args += [ng, sc, sh, win, gq, gkv, wq, wk, wv, tab]
    out_specs, out_shape = [], []
    if has_peer:
        out_specs.append(pl.BlockSpec((TM, D_MODEL), row))
        out_shape.append(jax.ShapeDtypeStruct((TT, D_MODEL), F32))
    for width, transposed in ((1024, False), (1024, False), (512, True),
                              (512, False), (512, False), (512, True)):
        if transposed:
            out_specs.append(pl.BlockSpec((width, TM), lambda i: (0, i)))
            out_shape.append(jax.ShapeDtypeStruct((width, TT), BF16))
        else:
            out_specs.append(pl.BlockSpec((TM, width), row))
            out_shape.append(jax.ShapeDtypeStruct((TT, width), BF16))
    res = pl.pallas_call(
        functools.partial(_qkv_body, has_peer),
        grid=(NB_ALL,),
        in_specs=in_specs,
        out_specs=out_specs,
        out_shape=out_shape,
        compiler_params=_cparams(("parallel",)),
        name="qkv",
    )(*args)
    if has_peer:
        return res[0], res[1:]
    return x, res


def _flash_pair(qs, ksls, kc_ref, kl_ref, vtc_ref, vtl_ref, n_pairs, sa_scr, sb_scr):
    n_lat = SEQ // TK

    def col_reduce(parts, op, final):
        while len(parts) > 1:
            parts = [op(parts[i], parts[i + 1]) for i in range(0, len(parts), 2)]
        return final(parts[0], axis=0, keepdims=True)

    def row_groups(x):
        return [x[i:i + SUBLANES] for i in range(0, x.shape[0], SUBLANES)]

    def scores(k_of):
        return [lax.dot_general(k_of(ksl), q, (((1,), (1,)), ((), ())), preferred_element_type=F32)
                for q, ksl in zip(qs, ksls)]

    def softmax_pv(read_s, vt, carries):
        mid = []
        for h, (m, l, acc) in enumerate(carries):
            m_new = jnp.maximum(m, col_reduce(row_groups(read_s(h)), jnp.maximum, jnp.max))
            alpha = jnp.exp(m - m_new)
            p = jnp.exp(read_s(h) - m_new)
            l = alpha * l + col_reduce(row_groups(p), jnp.add, jnp.sum)
            mid.append((m_new, l, alpha * acc, p.astype(BF16)))
        return tuple((m_new, l, acc + jnp.dot(vt, p, preferred_element_type=F32))
                     for (m_new, l, acc, p) in mid)

    def k_lat(c):
        r0 = pl.multiple_of(c * TK, TK)
        return lambda ksl: kl_ref[pl.ds(r0, TK), ksl]

    def vt_lat(c):
        return vtl_ref[:, pl.ds(pl.multiple_of(c * TK, TK), TK)]

    def stage(scr, sc):
        for h, x in enumerate(sc):
            scr[h] = x

    init = tuple((jnp.full((1, TQ), -jnp.inf, F32), jnp.zeros((1, TQ), F32),
                  jnp.zeros((LANES, TQ), F32)) for _ in qs)
    ctx_scores = scores(lambda ksl: kc_ref[:, ksl])
    carries = softmax_pv(lambda h: ctx_scores[h], vtc_ref[...], init)
    stage(sa_scr, scores(k_lat(0)))

    def pair(j, carries):
        c = 2 * j
        stage(sb_scr, scores(k_lat(c + 1)))
        carries = softmax_pv(lambda h: sa_scr[h], vt_lat(c), carries)
        stage(sa_scr, scores(k_lat(jnp.minimum(c + 2, n_lat - 1))))
        return softmax_pv(lambda h: sb_scr[h], vt_lat(c + 1), carries)

    carries = lax.fori_loop(0, n_pairs, pair, carries)
    return [acc / l for (m, l, acc) in carries]


def _n_pairs(nq):
    return jnp.where(pl.program_id(2) < nq, SEQ // (2 * TK), 0)


def _attn_mla_body(nq, q_ref, kc_ref, kl_ref, vtc_ref, vtl_ref, o_ref, sa_scr, sb_scr):
    qs = [q_ref[:, 0:LANES], q_ref[:, LANES:2 * LANES]]
    ksls = [slice(0, LANES), slice(LANES, 2 * LANES)]
    ot0, ot1 = _flash_pair(qs, ksls, kc_ref, kl_ref, vtc_ref, vtl_ref, _n_pairs(nq), sa_scr, sb_scr)
    row = lax.broadcasted_iota(jnp.int32, (LANES, TQ), 0)
    o_ref[...] = jnp.where(row < MLA_HEAD_DIM, ot0, ot1).T.astype(o_ref.dtype)


def _attn_diff_body(nq, lam_scale, q_ref, kc_ref, kl_ref, vtc_ref, vtl_ref, lam_ref, g_ref, o_ref,
                    sa_scr, sb_scr):
    lane = lax.broadcasted_iota(jnp.int32, (TQ, LANES), 1)
    q = q_ref[...]
    zero = jnp.zeros_like(q)
    qs = [jnp.where(lane < DIFF_HEAD_DIM, q, zero), jnp.where(lane < DIFF_HEAD_DIM, zero, q)]
    ksls = [slice(0, LANES), slice(0, LANES)]
    ot0, ot1 = _flash_pair(qs, ksls, kc_ref, kl_ref, vtc_ref, vtl_ref, _n_pairs(nq), sa_scr, sb_scr)
    o = ot0.T - lam_ref[...] * ot1.T
    o_ref[...] = (_rms(o) * g_ref[...] * lam_scale).astype(o_ref.dtype)


def _attention(kind, q, k, vt, with_ctx, lam=None, g=None, lam_scale=None):
    nq = SEQ // TQ
    steps = nq + 1 if with_ctx else nq
    qw = 2 * LANES if kind == "mla" else LANES
    ctx0 = T_LAT // CTX_LEN

    def qrow(b, p, i):
        return (jnp.where(i < nq, b * nq + i, (T_LAT // TQ) + b), p)

    in_specs = [
        pl.BlockSpec((TQ, qw), qrow),
        pl.BlockSpec((CTX_LEN, qw), lambda b, p, i: (ctx0 + b, p)),
        pl.BlockSpec((SEQ, qw), lambda b, p, i: (b, p)),
        pl.BlockSpec((LANES, CTX_LEN), lambda b, p, i: (p, ctx0 + b)),
        pl.BlockSpec((LANES, SEQ), lambda b, p, i: (p, b)),
    ]
    args = [q, k, k, vt, vt]
    if kind == "mla":
        body = functools.partial(_attn_mla_body, nq)
    else:
        body = functools.partial(_attn_diff_body, nq, lam_scale)
        in_specs += [pl.BlockSpec((1, LANES), lambda b, p, i: (0, 0)),
                     pl.BlockSpec((1, LANES), lambda b, p, i: (0, 0))]
        args += [lam, g]
    rows = TT if with_ctx else T_LAT
    return pl.pallas_call(
        body,
        grid=(BATCH, 4, steps),
        in_specs=in_specs,
        out_specs=pl.BlockSpec((TQ, LANES), qrow),
        out_shape=jax.ShapeDtypeStruct((rows, 512), BF16),
        scratch_shapes=[pltpu.VMEM((2, TK, TQ), F32), pltpu.VMEM((2, TK, TQ), F32)],
        compiler_params=_cparams(("parallel", "parallel", "arbitrary")),
        name="attn_" + kind,
    )(*args)


def _top16(s, n_rows):
    iota = lax.broadcasted_iota(jnp.int32, s.shape, 0)
    slot = lax.broadcasted_iota(jnp.int32, (PEER_TOPK, s.shape[1]), 0)
    vals = jnp.zeros((PEER_TOPK, s.shape[1]), F32)
    idxs = jnp.zeros((PEER_TOPK, s.shape[1]), jnp.int32)
    for r in range(PEER_TOPK):
        m = jnp.max(s, axis=0, keepdims=True)
        am = jnp.min(jnp.where(s == m, iota, n_rows), axis=0, keepdims=True)
        vals = jnp.where(slot == r, m, vals)
        idxs = jnp.where(slot == r, am, idxs)
        s = jnp.where(iota == am, -jnp.inf, s)
    return vals, idxs


def _mid_body(x_ref, om_ref, od_ref, wo_m_ref, wo_d_ref, g1_ref, ng_ref, sc_ref, sh_ref,
              wq_ref, sk_ref, xo_ref, h_ref, eidx_ref, gate_ref, q_scr):
    y = (jnp.dot(om_ref[...], wo_m_ref[...], preferred_element_type=F32)
         + jnp.dot(od_ref[...], wo_d_ref[...], preferred_element_type=F32))
    x = x_ref[...] + g1_ref[0] * y
    xo_ref[...] = x
    h = _rms(x) * ng_ref[...] * (1.0 + sc_ref[0]) + sh_ref[0]
    h_ref[...] = h
    q_scr[...] = jnp.dot(h.astype(BF16), wq_ref[...], preferred_element_type=F32).astype(BF16)

    def head(hh, _):
        sv, si = [], []
        for c in range(2):
            c0 = pl.multiple_of(hh * 2 * N_KEYS + c * N_KEYS, N_KEYS)
            qh = q_scr[:, pl.ds(c0, N_KEYS)]
            s = lax.dot_general(sk_ref[c], qh, (((1,), (1,)), ((), ())), preferred_element_type=F32)
            vals, idxs = _top16(s, N_KEYS)
            sv.append(vals)
            si.append(idxs)
        cand_s = jnp.concatenate([sv[0][a:a + 1, :] + sv[1] for a in range(PEER_TOPK)], axis=0)
        cand_i = jnp.concatenate([si[0][a:a + 1, :] * N_KEYS + si[1] for a in range(PEER_TOPK)], axis=0)
        n_cand = PEER_TOPK * PEER_TOPK
        iota = lax.broadcasted_iota(jnp.int32, cand_s.shape, 0)
        slot = lax.broadcasted_iota(jnp.int32, (PEER_TOPK, TM), 0)
        top_s = jnp.zeros((PEER_TOPK, TM), F32)
        top_i = jnp.zeros((PEER_TOPK, TM), jnp.int32)
        for r in range(PEER_TOPK):
            m = jnp.max(cand_s, axis=0, keepdims=True)
            am = jnp.min(jnp.where(cand_s == m, iota, n_cand), axis=0, keepdims=True)
            hit = iota == am
            e = jnp.max(jnp.where(hit, cand_i, -1), axis=0, keepdims=True)
            top_s = jnp.where(slot == r, m, top_s)
            top_i = jnp.where(slot == r, e, top_i)
            cand_s = jnp.where(hit, -jnp.inf, cand_s)
        ex = jnp.exp(top_s - jnp.max(top_s, axis=0, keepdims=True))
        gate = ex / jnp.sum(ex, axis=0, keepdims=True)
        r0 = pl.multiple_of(hh * PEER_TOPK, PEER_TOPK)
        eidx_ref[pl.ds(r0, PEER_TOPK), :] = top_i * ROWS_PER_EXPERT
        gate_ref[pl.ds(r0, PEER_TOPK), :] = gate
        return 0

    lax.fori_loop(0, PEER_HEADS, head, 0)


def _mid(n_blocks, x, om, od, wo_m, wo_d, g1, ng, sc, sh, wq, sk):
    n = n_blocks * TM
    row = lambda i: (i, 0)
    modrow = lambda i: (_bid(i), 0, 0)
    const = lambda i: (0, 0)
    return pl.pallas_call(
        _mid_body,
        grid=(n_blocks,),
        in_specs=[
            pl.BlockSpec((TM, D_MODEL), row),
            pl.BlockSpec((TM, 512), row),
            pl.BlockSpec((TM, 512), row),
            pl.BlockSpec((512, D_MODEL), const),
            pl.BlockSpec((512, D_MODEL), const),
            pl.BlockSpec((1, 1, D_MODEL), modrow),
            pl.BlockSpec((1, D_MODEL), const),
            pl.BlockSpec((1, 1, D_MODEL), modrow),
            pl.BlockSpec((1, 1, D_MODEL), modrow),
            pl.BlockSpec((D_MODEL, PEER_HEADS * 2 * N_KEYS), const),
            pl.BlockSpec((2, N_KEYS, N_KEYS), lambda i: (0, 0, 0)),
        ],
        out_specs=[
            pl.BlockSpec((TM, D_MODEL), row),
            pl.BlockSpec((TM, D_MODEL), row),
            pl.BlockSpec((PEER_HEADS * PEER_TOPK, TM), lambda i: (0, i)),
            pl.BlockSpec((PEER_HEADS * PEER_TOPK, TM), lambda i: (0, i)),
        ],
        out_shape=[
            jax.ShapeDtypeStruct((n, D_MODEL), F32),
            jax.ShapeDtypeStruct((n, D_MODEL), F32),
            jax.ShapeDtypeStruct((PEER_HEADS * PEER_TOPK, n), jnp.int32),
            jax.ShapeDtypeStruct((PEER_HEADS * PEER_TOPK, n), F32),
        ],
        scratch_shapes=[pltpu.VMEM((TM, PEER_HEADS * 2 * N_KEYS), BF16)],
        compiler_params=_cparams(("parallel",)),
        name="mid",
    )(x, om, od, wo_m, wo_d, g1, ng, sc, sh, wq, sk)


N_SEL = PEER_HEADS * PEER_TOPK
HALF = D_MODEL // 2
ROWS_PER_EXPERT = HALF // LANES
_BFLY_ORDER = (0, 4, 2, 6, 1, 5, 3, 7)


def _pack_table(tab):
    u = lax.bitcast_convert_type(tab.astype(BF16), jnp.uint16).astype(jnp.uint32)
    w = u[:, :HALF] | (u[:, HALF:] << 16)
    return w.reshape(tab.shape[0] * ROWS_PER_EXPERT, LANES)


def _load_pair(tab_ref, ra, rb):
    w = jnp.concatenate([tab_ref[pl.ds(pl.multiple_of(ra, ROWS_PER_EXPERT), ROWS_PER_EXPERT), :],
                         tab_ref[pl.ds(pl.multiple_of(rb, ROWS_PER_EXPERT), ROWS_PER_EXPERT), :]], axis=0)
    lo = pltpu.bitcast(w << 16, F32)
    hi = pltpu.bitcast(w & jnp.uint32(0xFFFF0000), F32)
    return lo, hi


def _peer_u_body(idx_ref, x_ref, gate_ref, tab_ref, act_ref, r_scr):
    sub = lax.broadcasted_iota(jnp.int32, (SUBLANES, LANES), 0)
    lane = lax.broadcasted_iota(jnp.int32, (SUBLANES, LANES), 1)
    top = sub < 4
    n_grp = N_SEL // SUBLANES

    def fold(p, q, width):
        tp = p + pltpu.roll(p, SUBLANES - width, 0)
        tq = q + pltpu.roll(q, SUBLANES - width, 0)
        return jnp.where((sub % (2 * width)) < width, tp, pltpu.roll(tq, width, 0))

    def tile(tb, _):
        def token(tt, _):
            t = tb * SUBLANES + tt
            xt = x_ref[t]
            xsw = pltpu.roll(xt, 4, 0)
            xlo = jnp.where(top, xt, xsw)
            xhi = jnp.where(top, xsw, xt)
            for g in range(n_grp):
                parts = []
                for pr in range(4):
                    ea = idx_ref[t, g * SUBLANES + _BFLY_ORDER[2 * pr]]
                    eb = idx_ref[t, g * SUBLANES + _BFLY_ORDER[2 * pr + 1]]
                    lo, hi = _load_pair(tab_ref, ea, eb)
                    parts.append(lo * xlo + hi * xhi)
                r_scr[tt * n_grp + g] = fold(fold(parts[0], parts[1], 2), fold(parts[2], parts[3], 2), 1)
            return 0

        lax.fori_loop(0, SUBLANES, token, 0)
        a = jnp.zeros((SUBLANES, LANES), F32)
        for i in range(SUBLANES * n_grp):
            a = jnp.where(lane == i, jnp.sum(r_scr[i], axis=1, keepdims=True), a)
        act_ref[tb] = 0.5 * a * (1.0 + lax.erf(a * (2.0 ** -0.5))) * gate_ref[tb]
        return 0

    lax.fori_loop(0, PEER_TB // SUBLANES, tile, 0)


def _peer_v_body(idx_ref, act_ref, tab_ref, o_ref):
    sub = lax.broadcasted_iota(jnp.int32, (SUBLANES, LANES), 0)
    lane = lax.broadcasted_iota(jnp.int32, (SUBLANES, LANES), 1)
    top = sub < 4
    n_grp = N_SEL // SUBLANES
    pair_rows = [jnp.where(top, 2 * p, 2 * p + 1) for p in range(4)]

    def tile(tb, _):
        a_tile = act_ref[tb]

        def token(tt, _):
            t = tb * SUBLANES + tt
            acc = [jnp.zeros((SUBLANES, LANES), F32) for _ in range(4)]
            for g in range(n_grp):
                col = jnp.sum(jnp.where(lane == tt * n_grp + g, a_tile, 0.0), axis=1, keepdims=True)
                actg = jnp.broadcast_to(col, (SUBLANES, LANES))
                for p in range(4):
                    ea = idx_ref[t, g * SUBLANES + 2 * p]
                    eb = idx_ref[t, g * SUBLANES + 2 * p + 1]
                    lo, hi = _load_pair(tab_ref, ea, eb)
                    av = jnp.take_along_axis(actg, pair_rows[p], axis=0)
                    k = 2 * (p % 2)
                    acc[k] = acc[k] + av * lo
                    acc[k + 1] = acc[k + 1] + av * hi
            lo = acc[0] + acc[2]
            hi = acc[1] + acc[3]
            lo = lo + pltpu.roll(lo, 4, 0)
            hi = hi + pltpu.roll(hi, 4, 0)
            o_ref[t] = jnp.where(top, lo, hi)
            return 0

        lax.fori_loop(0, SUBLANES, token, 0)
        return 0

    lax.fori_loop(0, PEER_TB // SUBLANES, tile, 0)


def _table_spec():
    return pl.BlockSpec((N_EXPERTS * ROWS_PER_EXPERT, LANES), lambda i: (0, 0),
                        pipeline_mode=pl.Buffered(1))


def _peer_u(n, idx, x3, gate3, tab):
    return pl.pallas_call(
        _peer_u_body,
        grid=(n // PEER_TB,),
        in_specs=[
            pl.BlockSpec((PEER_TB, N_SEL), lambda i: (i, 0), memory_space=pltpu.SMEM),
            pl.BlockSpec((PEER_TB, SUBLANES, LANES), lambda i: (i, 0, 0)),
            pl.BlockSpec((PEER_TB // SUBLANES, SUBLANES, LANES), lambda i: (i, 0, 0)),
            _table_spec(),
        ],
        out_specs=pl.BlockSpec((PEER_TB // SUBLANES, SUBLANES, LANES), lambda i: (i, 0, 0)),
        out_shape=jax.ShapeDtypeStruct((n // SUBLANES, SUBLANES, LANES), F32),
        scratch_shapes=[pltpu.VMEM((N_SEL, SUBLANES, LANES), F32)],
        compiler_params=_cparams(("arbitrary",)),
        name="peer_u",
    )(idx, x3, gate3, tab)


def _peer_v(n, idx, act3, tab):
    return pl.pallas_call(
        _peer_v_body,
        grid=(n // PEER_TB,),
        in_specs=[
            pl.BlockSpec((PEER_TB, N_SEL), lambda i: (i, 0), memory_space=pltpu.SMEM),
            pl.BlockSpec((PEER_TB // SUBLANES, SUBLANES, LANES), lambda i: (i, 0, 0)),
            _table_spec(),
        ],
        out_specs=pl.BlockSpec((PEER_TB, SUBLANES, LANES), lambda i: (i, 0, 0)),
        out_shape=jax.ShapeDtypeStruct((n, SUBLANES, LANES), F32),
        compiler_params=_cparams(("arbitrary",)),
        name="peer_v",
    )(idx, act3, tab)


def _final_body(x_ref, peer_ref, g2_ref, g_ref, o_ref):
    x = x_ref[...] + g2_ref[0] * peer_ref[...]
    o_ref[...] = _rms(x) * g_ref[...]


def _final(x, peer, g2, g):
    row = lambda i: (i, 0)
    return pl.pallas_call(
        _final_body,
        grid=(NB_LAT,),
        in_specs=[
            pl.BlockSpec((TM, D_MODEL), row),
            pl.BlockSpec((TM, D_MODEL), row),
            pl.BlockSpec((1, 1, D_MODEL), lambda i: (_bid(i), 0, 0)),
            pl.BlockSpec((1, D_MODEL), lambda i: (0, 0)),
        ],
        out_specs=pl.BlockSpec((TM, D_MODEL), row),
        out_shape=jax.ShapeDtypeStruct((T_LAT, D_MODEL), F32),
        compiler_params=_cparams(("parallel",)),
        name="final_norm",
    )(x, peer, g2, g)


def _deinterleave(width):
    return np.concatenate([np.arange(0, width, 2), np.arange(1, width, 2)])


def _rope_tables():
    pos = np.arange(SEQ)
    rowp = jnp.asarray(pos // GRID_W, F32)
    colp = jnp.asarray(pos % GRID_W, F32)

    def angles(dim):
        quarter = dim // 4
        inv = ROPE_BASE ** (-jnp.arange(quarter, dtype=F32) / quarter)
        return jnp.concatenate([rowp[:, None] * inv, colp[:, None] * inv], axis=-1)

    am = angles(MLA_ROPE_DIM)
    ad = angles(DIFF_HEAD_DIM)
    cm, sm_ = jnp.cos(am), jnp.sin(am)
    cd, sd = jnp.cos(ad), jnp.sin(ad)
    one = lambda w: jnp.ones((SEQ, w), F32)
    zero = lambda w: jnp.zeros((SEQ, w), F32)
    t_cm = jnp.concatenate([one(64), cm, cm, one(32)], axis=1)
    t_spm = jnp.concatenate([zero(80), sm_, zero(32)], axis=1)
    t_smm = jnp.concatenate([zero(64), -sm_, zero(48)], axis=1)
    t_cd = jnp.concatenate([cd, cd, cd, cd], axis=1)
    t_spd = jnp.concatenate([zero(32), sd, zero(32), sd], axis=1)
    t_smd = jnp.concatenate([-sd, zero(32), -sd, zero(32)], axis=1)
    lat = jnp.concatenate([t_cm, t_spm, t_smm, t_cd, t_spd, t_smd], axis=1)
    ident = jnp.concatenate([jnp.ones((CTX_LEN, 128), F32), jnp.zeros((CTX_LEN, 256), F32),
                             jnp.ones((CTX_LEN, 128), F32), jnp.zeros((CTX_LEN, 256), F32)], axis=1)
    return jnp.concatenate([lat, ident], axis=0)


def _prep_layer_weights(w_in, wq_up, wq_rope, wk_up, wv_up, w_out, peer_wq, peer_subkeys):
    p32 = _deinterleave(MLA_ROPE_DIM)
    p64 = _deinterleave(DIFF_HEAD_DIM)
    z = lambda w: jnp.zeros((D_MODEL, w), F32)
    kr = w_in[:, 640:672][:, p32]
    qd = w_in[:, 672:1184].reshape(D_MODEL, 8, 64)[:, :, p64].reshape(D_MODEL, 512)
    kd = w_in[:, 1184:1696].reshape(D_MODEL, 8, 64)[:, :, p64].reshape(D_MODEL, 512)
    win = jnp.concatenate([w_in[:, 0:640], z(64), kr, z(32), qd, kd, w_in[:, 1696:2208]], axis=1)
    qn = wq_up.reshape(MLA_Q_RANK, MLA_HEADS, MLA_HEAD_DIM)
    qr = wq_rope.reshape(MLA_Q_RANK, MLA_HEADS, MLA_ROPE_DIM)[:, :, p32]
    wq = jnp.concatenate([qn, qr, jnp.zeros((MLA_Q_RANK, MLA_HEADS, 32), F32)], axis=2)
    kn = wk_up.reshape(MLA_KV_RANK, MLA_HEADS, MLA_HEAD_DIM)
    wk = jnp.concatenate([kn, jnp.zeros((MLA_KV_RANK, MLA_HEADS, 64), F32)], axis=2)
    return dict(
        win=win.astype(BF16),
        wq=wq.reshape(MLA_Q_RANK, MLA_HEADS * LANES).astype(BF16),
        wk=wk.reshape(MLA_KV_RANK, MLA_HEADS * LANES).astype(BF16),
        wv=wv_up.astype(BF16),
        wo_m=w_out[:512].astype(BF16),
        wo_d=w_out[512:].astype(BF16),
        pwq=peer_wq.astype(BF16),
        sk=peer_subkeys.astype(BF16),
    )


def kernel(x, c, ctx, c_ctx, norm_attn_g, norm_ffn_g, w_ada, b_ada, w_in, mla_q_norm_g, mla_wq_up, mla_wq_rope, mla_kv_norm_g, mla_wk_up, mla_wv_up, diff_lambda, diff_subnorm_g, w_out, peer_wq, peer_subkeys, peer_u, peer_v, final_norm_g):
    xs = jnp.concatenate([x.reshape(T_LAT, D_MODEL), ctx.reshape(T_CTX, D_MODEL)], axis=0)
    cc = jnp.concatenate([c, c_ctx[None, :], jnp.zeros((MOD_ROWS - BATCH - 1, D_MODEL), F32)], axis=0)
    mod = _modulation(cc, w_ada, b_ada)
    lam_inits = [0.8 - 0.6 * math.exp(-0.3 * l) for l in range(DEPTH)]
    lam_all = _diff_lambda(diff_lambda, jnp.broadcast_to(jnp.asarray(lam_inits, F32)[:, None], (DEPTH, LANES)))
    tab = _rope_tables()

    peer_out = None
    g2_prev = None
    for l in range(DEPTH):
        last = l == DEPTH - 1
        w = _prep_layer_weights(w_in[l], mla_wq_up[l], mla_wq_rope[l], mla_wk_up[l], mla_wv_up[l],
                                w_out[l], peer_wq[l], peer_subkeys[l])
        m = mod[l].reshape(MOD_ROWS, N_MOD, 1, D_MODEL)
        sh1, sc1, g1, sh2, sc2, g2 = (m[:, j] for j in range(N_MOD))
        xs, (qm, km, vm, qd, kd, vd) = _qkv(
            xs, peer_out, g2_prev, norm_attn_g[l][None, :], sc1, sh1, w["win"],
            mla_q_norm_g[l][None, :], mla_kv_norm_g[l][None, :], w["wq"], w["wk"], w["wv"], tab)
        om = _attention("mla", qm, km, vm, not last)
        od = _attention("diff", qd, kd, vd, not last, lam=lam_all[l][None, :],
                        g=diff_subnorm_g[l][None, :], lam_scale=1.0 - lam_inits[l])
        n_blocks = NB_LAT if last else NB_ALL
        n = n_blocks * TM
        xs, h2, eidx_t, gate_t = _mid(n_blocks, xs, om, od, w["wo_m"], w["wo_d"], g1,
                                      norm_ffn_g[l][None, :], sc2, sh2, w["pwq"], w["sk"])
        idx = eidx_t.T
        gate3 = gate_t.reshape(16, 8, n // 8, 8).transpose(2, 1, 3, 0).reshape(n // 8, 8, LANES)
        act3 = _peer_u(n, idx, h2.reshape(n, SUBLANES, LANES), gate3, _pack_table(peer_u[l]))
        peer_out = _peer_v(n, idx, act3, _pack_table(peer_v[l])).reshape(n, D_MODEL)
        g2_prev = g2
    out = _final(xs, peer_out, g2_prev, final_norm_g[None, :])
    return out.reshape(BATCH, SEQ, D_MODEL)
```

```python
import functools
import math

import jax
import jax.numpy as jnp
import numpy as np
from jax import lax
from jax.experimental import pallas as pl
from jax.experimental.pallas import tpu as pltpu

F32 = jnp.float32
BF16 = jnp.bfloat16

D_MODEL = 1024
BATCH = 8
SEQ = 4096
DEPTH = 4
GRID_W = 64
CTX_LEN = 256
N_MOD = 6
NORM_EPS = 1e-6
ROPE_BASE = 10000.0
MLA_HEADS = 8
MLA_HEAD_DIM = 64
MLA_ROPE_DIM = 32
MLA_Q_RANK = 384
MLA_KV_RANK = 256
DIFF_HEADS = 4
DIFF_HEAD_DIM = 64
PEER_HEADS = 8
PEER_TOPK = 16
N_KEYS = 128
N_EXPERTS = N_KEYS * N_KEYS

T_LAT = BATCH * SEQ
T_CTX = BATCH * CTX_LEN
TT = T_LAT + T_CTX
TM = 256
NB_LAT = T_LAT // TM
NB_ALL = TT // TM
BLK_PER_BATCH = SEQ // TM
MOD_ROWS = 16
PROJ_PAD = 2304
TQ = 256
TK = 512
PEER_TB = 128
LANES = 128
SUBLANES = 8
VMEM_LIMIT = 56 * 1024 * 1024


def _cparams(sem, vmem=VMEM_LIMIT):
    return pltpu.CompilerParams(dimension_semantics=sem, vmem_limit_bytes=vmem)


def _rms(x):
    return x * lax.rsqrt(jnp.mean(x * x, axis=-1, keepdims=True) + NORM_EPS)


def _bid(i):
    return jnp.where(i < NB_LAT, i // BLK_PER_BATCH, BATCH)


def _posblk(i):
    return jnp.where(i < NB_LAT, i % BLK_PER_BATCH, BLK_PER_BATCH)


def _mod_body(cc_ref, w_ref, b_ref, o_ref):
    cc = cc_ref[...]
    s = cc / (1.0 + jnp.exp(-cc))
    o_ref[0] = jnp.dot(s, w_ref[0], precision=lax.Precision.HIGHEST,
                       preferred_element_type=F32) + b_ref[0]


def _modulation(cc, w_ada, b_ada):
    nj = N_MOD
    return pl.pallas_call(
        _mod_body,
        grid=(DEPTH, nj),
        in_specs=[
            pl.BlockSpec((MOD_ROWS, D_MODEL), lambda l, j: (0, 0)),
            pl.BlockSpec((1, D_MODEL, D_MODEL), lambda l, j: (l, 0, j)),
            pl.BlockSpec((1, 1, D_MODEL), lambda l, j: (l, 0, j)),
        ],
        out_specs=pl.BlockSpec((1, MOD_ROWS, D_MODEL), lambda l, j: (l, 0, j)),
        out_shape=jax.ShapeDtypeStruct((DEPTH, MOD_ROWS, N_MOD * D_MODEL), F32),
        compiler_params=_cparams(("arbitrary", "arbitrary")),
        name="modulation",
    )(cc, w_ada, b_ada.reshape(DEPTH, 1, N_MOD * D_MODEL))


def _lam_body(d0_ref, d1_ref, d2_ref, d3_ref, li_ref, o_ref):
    a = jnp.sum(d0_ref[...] * d1_ref[...], axis=-1, keepdims=True)
    b = jnp.sum(d2_ref[...] * d3_ref[...], axis=-1, keepdims=True)
    o_ref[...] = jnp.exp(a) - jnp.exp(b) + li_ref[...]


def _diff_lambda(diff_lambda, lam_init):
    dl = diff_lambda.astype(F32)
    return pl.pallas_call(
        _lam_body,
        out_shape=jax.ShapeDtypeStruct((DEPTH, LANES), F32),
        name="diff_lambda",
    )(dl[:, 0], dl[:, 1], dl[:, 2], dl[:, 3], lam_init)


def _rope(xb, c, sp, sm, shift):
    return xb * c + pltpu.roll(xb, shift, 1) * sp + pltpu.roll(xb, LANES - shift, 1) * sm


def _qkv_body(has_peer, *refs):
    if has_peer:
        x_ref, peer_ref, g2_ref = refs[:3]
        refs = refs[3:]
    else:
        x_ref = refs[0]
        refs = refs[1:]
    (ng_ref, sc_ref, sh_ref, win_ref, gq_ref, gkv_ref, wq_ref, wk_ref, wv_ref, tab_ref) = refs[:10]
    outs = refs[10:]
    if has_peer:
        xo_ref, outs = outs[0], outs[1:]
    qm_ref, km_ref, vm_ref, qd_ref, kd_ref, vd_ref = outs

    x = x_ref[...]
    if has_peer:
        x = x + g2_ref[0] * peer_ref[...]
        xo_ref[...] = x
    h = _rms(x) * ng_ref[...] * (1.0 + sc_ref[0]) + sh_ref[0]
    proj = jnp.dot(h.astype(BF16), win_ref[...], preferred_element_type=F32)
    cq = _rms(proj[:, 0:384]) * gq_ref[...]
    ckv = _rms(proj[:, 384:640]) * gkv_ref[...]
    kr = proj[:, 640:768]
    q = jnp.dot(cq.astype(BF16), wq_ref[...], preferred_element_type=F32)
    ckv16 = ckv.astype(BF16)
    k = jnp.dot(ckv16, wk_ref[...], preferred_element_type=F32)
    v = jnp.dot(ckv16, wv_ref[...], preferred_element_type=F32)
    tab = tab_ref[...]
    cm, spm, smm = tab[:, 0:128], tab[:, 128:256], tab[:, 256:384]
    cd, spd, smd = tab[:, 384:512], tab[:, 512:640], tab[:, 640:768]
    kr_rot = _rope(kr, cm, spm, smm, MLA_ROPE_DIM // 2)
    scale_m = (MLA_HEAD_DIM + MLA_ROPE_DIM) ** -0.5
    for hh in range(MLA_HEADS):
        sl = slice(hh * LANES, (hh + 1) * LANES)
        qm_ref[:, sl] = (_rope(q[:, sl], cm, spm, smm, MLA_ROPE_DIM // 2) * scale_m).astype(BF16)
        km_ref[:, sl] = (k[:, sl] + kr_rot).astype(BF16)
    vm_ref[...] = v.T.astype(BF16)
    scale_d = DIFF_HEAD_DIM ** -0.5
    for hh in range(DIFF_HEADS):
        sl = slice(hh * LANES, (hh + 1) * LANES)
        qd = proj[:, 768 + hh * LANES:768 + (hh + 1) * LANES]
        kd = proj[:, 1280 + hh * LANES:1280 + (hh + 1) * LANES]
        qd_ref[:, sl] = (_rope(qd, cd, spd, smd, DIFF_HEAD_DIM // 2) * scale_d).astype(BF16)
        kd_ref[:, sl] = _rope(kd, cd, spd, smd, DIFF_HEAD_DIM // 2).astype(BF16)
    vd_ref[...] = proj[:, 1792:2304].T.astype(BF16)


def _qkv(x, peer, g2, ng, sc, sh, win, gq, gkv, wq, wk, wv, tab):
    has_peer = peer is not None
    row = lambda i: (i, 0)
    modrow = lambda i: (_bid(i), 0, 0)
    const = lambda i: (0, 0)
    in_specs = [pl.BlockSpec((TM, D_MODEL), row)]
    args = [x]
    if has_peer:
        in_specs += [pl.BlockSpec((TM, D_MODEL), row), pl.BlockSpec((1, 1, D_MODEL), modrow)]
        args += [peer, g2]
    in_specs += [
        pl.BlockSpec((1, D_MODEL), const),
        pl.BlockSpec((1, 1, D_MODEL), modrow),
        pl.BlockSpec((1, 1, D_MODEL), modrow),
        pl.BlockSpec((D_MODEL, PROJ_PAD), const),
        pl.BlockSpec((1, MLA_Q_RANK), const),
        pl.BlockSpec((1, MLA_KV_RANK), const),
        pl.BlockSpec((MLA_Q_RANK, MLA_HEADS * LANES), const),
        pl.BlockSpec((MLA_KV_RANK, MLA_HEADS * LANES), const),
        pl.BlockSpec((MLA_KV_RANK, 512), const),
        pl.BlockSpec((TM, 768), lambda i: (_posblk(i), 0)),
    ]
    args += [ng, sc, sh, win, gq, gkv, wq, wk, wv, tab]
    out_specs, out_shape = [], []
    if has_peer:
        out_specs.append(pl.BlockSpec((TM, D_MODEL), row))
        out_shape.append(jax.ShapeDtypeStruct((TT, D_MODEL), F32))
    for width, transposed in ((1024, False), (1024, False), (512, True),
                              (512, False), (512, False), (512, True)):
        if transposed:
            out_specs.append(pl.BlockSpec((width, TM), lambda i: (0, i)))
            out_shape.append(jax.ShapeDtypeStruct((width, TT), BF16))
        else:
            out_specs.append(pl.BlockSpec((TM, width), row))
            out_shape.append(jax.ShapeDtypeStruct((TT, width), BF16))
    res = pl.pallas_call(
        functools.partial(_qkv_body, has_peer),
        grid=(NB_ALL,),
        in_specs=in_specs,
        out_specs=out_specs,
        out_shape=out_shape,
        compiler_params=_cparams(("parallel",)),
        name="qkv",
    )(*args)
    if has_peer:
        return res[0], res[1:]
    return x, res


def _flash_pair(qs, ksls, kc_ref, kl_ref, vtc_ref, vtl_ref, n_pairs, sa_scr, sb_scr):
    n_lat = SEQ // TK

    def col_reduce(parts, op, final):
        while len(parts) > 1:
            parts = [op(parts[i], parts[i + 1]) for i in range(0, len(parts), 2)]
        return final(parts[0], axis=0, keepdims=True)

    def row_groups(x):
        return [x[i:i + SUBLANES] for i in range(0, x.shape[0], SUBLANES)]

    def scores(k_of):
        return [lax.dot_general(k_of(ksl), q, (((1,), (1,)), ((), ())), preferred_element_type=F32)
                for q, ksl in zip(qs, ksls)]

    def softmax_pv(read_s, vt, carries):
        mid = []
        for h, (m, l, acc) in enumerate(carries):
            m_new = jnp.maximum(m, col_reduce(row_groups(read_s(h)), jnp.maximum, jnp.max))
            alpha = jnp.exp(m - m_new)
            p = jnp.exp(read_s(h) - m_new)
            l = alpha * l + col_reduce(row_groups(p), jnp.add, jnp.sum)
            mid.append((m_new, l, alpha * acc, p.astype(BF16)))
        return tuple((m_new, l, acc + jnp.dot(vt, p, preferred_element_type=F32))
                     for (m_new, l, acc, p) in mid)

    def k_lat(c):
        r0 = pl.multiple_of(c * TK, TK)
        return lambda ksl: kl_ref[pl.ds(r0, TK), ksl]

    def vt_lat(c):
        return vtl_ref[:, pl.ds(pl.multiple_of(c * TK, TK), TK)]

    def stage(scr, sc):
        for h, x in enumerate(sc):
            scr[h] = x

    init = tuple((jnp.full((1, TQ), -jnp.inf, F32), jnp.zeros((1, TQ), F32),
                  jnp.zeros((LANES, TQ), F32)) for _ in qs)
    ctx_scores = scores(lambda ksl: kc_ref[:, ksl])
    carries = softmax_pv(lambda h: ctx_scores[h], vtc_ref[...], init)
    stage(sa_scr, scores(k_lat(0)))

    def pair(j, carries):
        c = 2 * j
        stage(sb_scr, scores(k_lat(c + 1)))
        carries = softmax_pv(lambda h: sa_scr[h], vt_lat(c), carries)
        stage(sa_scr, scores(k_lat(jnp.minimum(c + 2, n_lat - 1))))
        return softmax_pv(lambda h: sb_scr[h], vt_lat(c + 1), carries)

    carries = lax.fori_loop(0, n_pairs, pair, carries)
    return [acc / l for (m, l, acc) in carries]


def _n_pairs(nq):
    return jnp.where(pl.program_id(2) < nq, SEQ // (2 * TK), 0)


def _attn_mla_body(nq, q_ref, kc_ref, kl_ref, vtc_ref, vtl_ref, o_ref, sa_scr, sb_scr):
    qs = [q_ref[:, 0:LANES], q_ref[:, LANES:2 * LANES]]
    ksls = [slice(0, LANES), slice(LANES, 2 * LANES)]
    ot0, ot1 = _flash_pair(qs, ksls, kc_ref, kl_ref, vtc_ref, vtl_ref, _n_pairs(nq), sa_scr, sb_scr)
    row = lax.broadcasted_iota(jnp.int32, (LANES, TQ), 0)
    o_ref[...] = jnp.where(row < MLA_HEAD_DIM, ot0, ot1).T.astype(o_ref.dtype)


def _attn_diff_body(nq, lam_scale, q_ref, kc_ref, kl_ref, vtc_ref, vtl_ref, lam_ref, g_ref, o_ref,
                    sa_scr, sb_scr):
    lane = lax.broadcasted_iota(jnp.int32, (TQ, LANES), 1)
    q = q_ref[...]
    zero = jnp.zeros_like(q)
    qs = [jnp.where(lane < DIFF_HEAD_DIM, q, zero), jnp.where(lane < DIFF_HEAD_DIM, zero, q)]
    ksls = [slice(0, LANES), slice(0, LANES)]
    ot0, ot1 = _flash_pair(qs, ksls, kc_ref, kl_ref, vtc_ref, vtl_ref, _n_pairs(nq), sa_scr, sb_scr)
    o = ot0.T - lam_ref[...] * ot1.T
    o_ref[...] = (_rms(o) * g_ref[...] * lam_scale).astype(o_ref.dtype)


def _attention(kind, q, k, vt, with_ctx, lam=None, g=None, lam_scale=None):
    nq = SEQ // TQ
    steps = nq + 1 if with_ctx else nq
    qw = 2 * LANES if kind == "mla" else LANES
    ctx0 = T_LAT // CTX_LEN

    def qrow(b, p, i):
        return (jnp.where(i < nq, b * nq + i, (T_LAT // TQ) + b), p)

    in_specs = [
        pl.BlockSpec((TQ, qw), qrow),
        pl.BlockSpec((CTX_LEN, qw), lambda b, p, i: (ctx0 + b, p)),
        pl.BlockSpec((SEQ, qw), lambda b, p, i: (b, p)),
        pl.BlockSpec((LANES, CTX_LEN), lambda b, p, i: (p, ctx0 + b)),
        pl.BlockSpec((LANES, SEQ), lambda b, p, i: (p, b)),
    ]
    args = [q, k, k, vt, vt]
    if kind == "mla":
        body = functools.partial(_attn_mla_body, nq)
    else:
        body = functools.partial(_attn_diff_body, nq, lam_scale)
        in_specs += [pl.BlockSpec((1, LANES), lambda b, p, i: (0, 0)),
                     pl.BlockSpec((1, LANES), lambda b, p, i: (0, 0))]
        args += [lam, g]
    rows = TT if with_ctx else T_LAT
    return pl.pallas_call(
        body,
        grid=(BATCH, 4, steps),
        in_specs=in_specs,
        out_specs=pl.BlockSpec((TQ, LANES), qrow),
        out_shape=jax.ShapeDtypeStruct((rows, 512), BF16),
        scratch_shapes=[pltpu.VMEM((2, TK, TQ), F32), pltpu.VMEM((2, TK, TQ), F32)],
        compiler_params=_cparams(("parallel", "parallel", "arbitrary")),
        name="attn_" + kind,
    )(*args)


def _top16(s, payload=None):
    n_rows = s.shape[0]
    rowf = lax.broadcasted_iota(jnp.int32, s.shape, 0).astype(F32)
    slot = lax.broadcasted_iota(jnp.int32, (PEER_TOPK, s.shape[1]), 0)
    vals = jnp.zeros((PEER_TOPK, s.shape[1]), F32)
    picks = jnp.zeros((PEER_TOPK, s.shape[1]), F32)
    for r in range(PEER_TOPK):
        m = jnp.max(s, axis=0, keepdims=True)
        am = jnp.min(jnp.where(s == m, rowf, float(n_rows)), axis=0, keepdims=True)
        hit = rowf == am
        pick = am if payload is None else jnp.max(jnp.where(hit, payload, -1.0), axis=0, keepdims=True)
        vals = jnp.where(slot == r, m, vals)
        picks = jnp.where(slot == r, pick, picks)
        s = jnp.where(hit, -jnp.inf, s)
    return vals, picks


def _staircase(a16, b16, combine, pad):
    tm = a16.shape[1]
    sub = lax.broadcasted_iota(jnp.int32, (SUBLANES, tm), 0)
    a_lo, a_hi = a16[0:SUBLANES], a16[SUBLANES:]
    b_lo, b_hi = b16[0:SUBLANES], b16[SUBLANES:]
    row = lambda x, r: jnp.broadcast_to(x[r:r + 1], (SUBLANES, tm))
    take = lambda x, idx: jnp.take_along_axis(x, idx, axis=0)
    a3 = jnp.where(sub < 5, 2, 3)
    b3 = jnp.where(sub < 5, sub, sub - 5)
    a4 = jnp.where(sub < 1, 3, jnp.where(sub < 4, 4, jnp.where(sub < 6, 5, 6)))
    b4 = jnp.where(sub < 1, 3, jnp.where(sub < 4, sub - 1, jnp.where(sub < 6, sub - 4, sub - 6)))
    pieces = [
        combine(row(a_lo, 0), b_lo),
        combine(row(a_lo, 0), b_hi),
        combine(row(a_lo, 1), b_lo),
        combine(take(a_lo, a3), take(b_lo, b3)),
        combine(take(a_lo, a4), take(b_lo, b4)),
        jnp.where(sub < 2, combine(row(a_lo, 7), b_lo), pad),
        combine(a_hi, row(b_lo, 0)),
    ]
    return jnp.concatenate(pieces, axis=0)


def _mid_body(x_ref, om_ref, od_ref, wo_m_ref, wo_d_ref, g1_ref, ng_ref, sc_ref, sh_ref,
              wq_ref, sk_ref, xo_ref, h_ref, eidx_ref, gate_ref, q_scr):
    y = (jnp.dot(om_ref[...], wo_m_ref[...], preferred_element_type=F32)
         + jnp.dot(od_ref[...], wo_d_ref[...], preferred_element_type=F32))
    x = x_ref[...] + g1_ref[0] * y
    xo_ref[...] = x
    h = _rms(x) * ng_ref[...] * (1.0 + sc_ref[0]) + sh_ref[0]
    h_ref[...] = h
    q_scr[...] = jnp.dot(h.astype(BF16), wq_ref[...], preferred_element_type=F32).astype(BF16)

    def head(hh, _):
        sv, si = [], []
        for c in range(2):
            c0 = pl.multiple_of(hh * 2 * N_KEYS + c * N_KEYS, N_KEYS)
            qh = q_scr[:, pl.ds(c0, N_KEYS)]
            s = lax.dot_general(sk_ref[c], qh, (((1,), (1,)), ((), ())), preferred_element_type=F32)
            vals, keys = _top16(s)
            sv.append(vals)
            si.append(keys)
        cand_s = _staircase(sv[0], sv[1], lambda a, b: a + b, -jnp.inf)
        cand_e = _staircase(si[0], si[1], lambda a, b: a * float(N_KEYS) + b, -1.0)
        top_s, top_e = _top16(cand_s, cand_e)
        ex = jnp.exp(top_s - jnp.max(top_s, axis=0, keepdims=True))
        gate = ex / jnp.sum(ex, axis=0, keepdims=True)
        r0 = pl.multiple_of(hh * PEER_TOPK, PEER_TOPK)
        eidx_ref[pl.ds(r0, PEER_TOPK), :] = top_e.astype(jnp.int32) * ROWS_PER_EXPERT
        gate_ref[pl.ds(r0, PEER_TOPK), :] = gate
        return 0

    lax.fori_loop(0, PEER_HEADS, head, 0)


def _mid(n_blocks, x, om, od, wo_m, wo_d, g1, ng, sc, sh, wq, sk):
    n = n_blocks * TM
    row = lambda i: (i, 0)
    modrow = lambda i: (_bid(i), 0, 0)
    const = lambda i: (0, 0)
    return pl.pallas_call(
        _mid_body,
        grid=(n_blocks,),
        in_specs=[
            pl.BlockSpec((TM, D_MODEL), row),
            pl.BlockSpec((TM, 512), row),
            pl.BlockSpec((TM, 512), row),
            pl.BlockSpec((512, D_MODEL), const),
            pl.BlockSpec((512, D_MODEL), const),
            pl.BlockSpec((1, 1, D_MODEL), modrow),
            pl.BlockSpec((1, D_MODEL), const),
            pl.BlockSpec((1, 1, D_MODEL), modrow),
            pl.BlockSpec((1, 1, D_MODEL), modrow),
            pl.BlockSpec((D_MODEL, PEER_HEADS * 2 * N_KEYS), const),
            pl.BlockSpec((2, N_KEYS, N_KEYS), lambda i: (0, 0, 0)),
        ],
        out_specs=[
            pl.BlockSpec((TM, D_MODEL), row),
            pl.BlockSpec((TM, D_MODEL), row),
            pl.BlockSpec((PEER_HEADS * PEER_TOPK, TM), lambda i: (0, i)),
            pl.BlockSpec((PEER_HEADS * PEER_TOPK, TM), lambda i: (0, i)),
        ],
        out_shape=[
            jax.ShapeDtypeStruct((n, D_MODEL), F32),
            jax.ShapeDtypeStruct((n, D_MODEL), F32),
            jax.ShapeDtypeStruct((PEER_HEADS * PEER_TOPK, n), jnp.int32),
            jax.ShapeDtypeStruct((PEER_HEADS * PEER_TOPK, n), F32),
        ],
        scratch_shapes=[pltpu.VMEM((TM, PEER_HEADS * 2 * N_KEYS), BF16)],
        compiler_params=_cparams(("parallel",)),
        name="mid",
    )(x, om, od, wo_m, wo_d, g1, ng, sc, sh, wq, sk)


N_SEL = PEER_HEADS * PEER_TOPK
HALF = D_MODEL // 2
ROWS_PER_EXPERT = HALF // LANES
_BFLY_ORDER = (0, 4, 2, 6, 1, 5, 3, 7)


def _pack_table(tab):
    u = lax.bitcast_convert_type(tab.astype(BF16), jnp.uint16).astype(jnp.uint32)
    w = u[:, :HALF] | (u[:, HALF:] << 16)
    return w.reshape(tab.shape[0] * ROWS_PER_EXPERT, LANES)


def _load_pair(tab_ref, ra, rb):
    w = jnp.concatenate([tab_ref[pl.ds(pl.multiple_of(ra, ROWS_PER_EXPERT), ROWS_PER_EXPERT), :],
                         tab_ref[pl.ds(pl.multiple_of(rb, ROWS_PER_EXPERT), ROWS_PER_EXPERT), :]], axis=0)
    lo = pltpu.bitcast(w << 16, F32)
    hi = pltpu.bitcast(w & jnp.uint32(0xFFFF0000), F32)
    return lo, hi


def _peer_u_body(idx_ref, x_ref, gate_ref, tab_ref, act_ref, r_scr):
    sub = lax.broadcasted_iota(jnp.int32, (SUBLANES, LANES), 0)
    lane = lax.broadcasted_iota(jnp.int32, (SUBLANES, LANES), 1)
    top = sub < 4
    n_grp = N_SEL // SUBLANES

    def fold(p, q, width):
        tp = p + pltpu.roll(p, SUBLANES - width, 0)
        tq = q + pltpu.roll(q, SUBLANES - width, 0)
        return jnp.where((sub % (2 * width)) < width, tp, pltpu.roll(tq, width, 0))

    def tile(tb, _):
        def token(tt, _):
            t = tb * SUBLANES + tt
            xt = x_ref[t]
            xsw = pltpu.roll(xt, 4, 0)
            xlo = jnp.where(top, xt, xsw)
            xhi = jnp.where(top, xsw, xt)
            for g in range(n_grp):
                parts = []
                for pr in range(4):
                    ea = idx_ref[t, g * SUBLANES + _BFLY_ORDER[2 * pr]]
                    eb = idx_ref[t, g * SUBLANES + _BFLY_ORDER[2 * pr + 1]]
                    lo, hi = _load_pair(tab_ref, ea, eb)
                    parts.append(lo * xlo + hi * xhi)
                r_scr[tt * n_grp + g] = fold(fold(parts[0], parts[1], 2), fold(parts[2], parts[3], 2), 1)
            return 0

        lax.fori_loop(0, SUBLANES, token, 0)
        a = jnp.zeros((SUBLANES, LANES), F32)
        for i in range(SUBLANES * n_grp):
            a = jnp.where(lane == i, jnp.sum(r_scr[i], axis=1, keepdims=True), a)
        act_ref[tb] = 0.5 * a * (1.0 + lax.erf(a * (2.0 ** -0.5))) * gate_ref[tb]
        return 0

    lax.fori_loop(0, PEER_TB // SUBLANES, tile, 0)


def _peer_v_body(idx_ref, act_ref, tab_ref, o_ref):
    sub = lax.broadcasted_iota(jnp.int32, (SUBLANES, LANES), 0)
    lane = lax.broadcasted_iota(jnp.int32, (SUBLANES, LANES), 1)
    top = sub < 4
    n_grp = N_SEL // SUBLANES
    pair_rows = [jnp.where(top, 2 * p, 2 * p + 1) for p in range(4)]

    def tile(tb, _):
        a_tile = act_ref[tb]

        def token(tt, _):
            t = tb * SUBLANES + tt
            acc = [jnp.zeros((SUBLANES, LANES), F32) for _ in range(4)]
            for g in range(n_grp):
                col = jnp.sum(jnp.where(lane == tt * n_grp + g, a_tile, 0.0), axis=1, keepdims=True)
                actg = jnp.broadcast_to(col, (SUBLANES, LANES))
                for p in range(4):
                    ea = idx_ref[t, g * SUBLANES + 2 * p]
                    eb = idx_ref[t, g * SUBLANES + 2 * p + 1]
                    lo, hi = _load_pair(tab_ref, ea, eb)
                    av = jnp.take_along_axis(actg, pair_rows[p], axis=0)
                    k = 2 * (p % 2)
                    acc[k] = acc[k] + av * lo
                    acc[k + 1] = acc[k + 1] + av * hi
            lo = acc[0] + acc[2]
            hi = acc[1] + acc[3]
            lo = lo + pltpu.roll(lo, 4, 0)
            hi = hi + pltpu.roll(hi, 4, 0)
            o_ref[t] = jnp.where(top, lo, hi)
            return 0

        lax.fori_loop(0, SUBLANES, token, 0)
        return 0

    lax.fori_loop(0, PEER_TB // SUBLANES, tile, 0)


def _table_spec():
    return pl.BlockSpec((N_EXPERTS * ROWS_PER_EXPERT, LANES), lambda i: (0, 0),
                        pipeline_mode=pl.Buffered(1))


def _peer_u(n, idx, x3, gate3, tab):
    return pl.pallas_call(
        _peer_u_body,
        grid=(n // PEER_TB,),
        in_specs=[
            pl.BlockSpec((PEER_TB, N_SEL), lambda i: (i, 0), memory_space=pltpu.SMEM),
            pl.BlockSpec((PEER_TB, SUBLANES, LANES), lambda i: (i, 0, 0)),
            pl.BlockSpec((PEER_TB // SUBLANES, SUBLANES, LANES), lambda i: (i, 0, 0)),
            _table_spec(),
        ],
        out_specs=pl.BlockSpec((PEER_TB // SUBLANES, SUBLANES, LANES), lambda i: (i, 0, 0)),
        out_shape=jax.ShapeDtypeStruct((n // SUBLANES, SUBLANES, LANES), F32),
        scratch_shapes=[pltpu.VMEM((N_SEL, SUBLANES, LANES), F32)],
        compiler_params=_cparams(("arbitrary",)),
        name="peer_u",
    )(idx, x3, gate3, tab)


def _peer_v(n, idx, act3, tab):
    return pl.pallas_call(
        _peer_v_body,
        grid=(n // PEER_TB,),
        in_specs=[
            pl.BlockSpec((PEER_TB, N_SEL), lambda i: (i, 0), memory_space=pltpu.SMEM),
            pl.BlockSpec((PEER_TB // SUBLANES, SUBLANES, LANES), lambda i: (i, 0, 0)),
            _table_spec(),
        ],
        out_specs=pl.BlockSpec((PEER_TB, SUBLANES, LANES), lambda i: (i, 0, 0)),
        out_shape=jax.ShapeDtypeStruct((n, SUBLANES, LANES), F32),
        compiler_params=_cparams(("arbitrary",)),
        name="peer_v",
    )(idx, act3, tab)


def _final_body(x_ref, peer_ref, g2_ref, g_ref, o_ref):
    x = x_ref[...] + g2_ref[0] * peer_ref[...]
    o_ref[...] = _rms(x) * g_ref[...]


def _final(x, peer, g2, g):
    row = lambda i: (i, 0)
    return pl.pallas_call(
        _final_body,
        grid=(NB_LAT,),
        in_specs=[
            pl.BlockSpec((TM, D_MODEL), row),
            pl.BlockSpec((TM, D_MODEL), row),
            pl.BlockSpec((1, 1, D_MODEL), lambda i: (_bid(i), 0, 0)),
            pl.BlockSpec((1, D_MODEL), lambda i: (0, 0)),
        ],
        out_specs=pl.BlockSpec((TM, D_MODEL), row),
        out_shape=jax.ShapeDtypeStruct((T_LAT, D_MODEL), F32),
        compiler_params=_cparams(("parallel",)),
        name="final_norm",
    )(x, peer, g2, g)


def _deinterleave(width):
    return np.concatenate([np.arange(0, width, 2), np.arange(1, width, 2)])


def _rope_tables():
    pos = np.arange(SEQ)
    rowp = jnp.asarray(pos // GRID_W, F32)
    colp = jnp.asarray(pos % GRID_W, F32)

    def angles(dim):
        quarter = dim // 4
        inv = ROPE_BASE ** (-jnp.arange(quarter, dtype=F32) / quarter)
        return jnp.concatenate([rowp[:, None] * inv, colp[:, None] * inv], axis=-1)

    am = angles(MLA_ROPE_DIM)
    ad = angles(DIFF_HEAD_DIM)
    cm, sm_ = jnp.cos(am), jnp.sin(am)
    cd, sd = jnp.cos(ad), jnp.sin(ad)
    one = lambda w: jnp.ones((SEQ, w), F32)
    zero = lambda w: jnp.zeros((SEQ, w), F32)
    t_cm = jnp.concatenate([one(64), cm, cm, one(32)], axis=1)
    t_spm = jnp.concatenate([zero(80), sm_, zero(32)], axis=1)
    t_smm = jnp.concatenate([zero(64), -sm_, zero(48)], axis=1)
    t_cd = jnp.concatenate([cd, cd, cd, cd], axis=1)
    t_spd = jnp.concatenate([zero(32), sd, zero(32), sd], axis=1)
    t_smd = jnp.concatenate([-sd, zero(32), -sd, zero(32)], axis=1)
    lat = jnp.concatenate([t_cm, t_spm, t_smm, t_cd, t_spd, t_smd], axis=1)
    ident = jnp.concatenate([jnp.ones((CTX_LEN, 128), F32), jnp.zeros((CTX_LEN, 256), F32),
                             jnp.ones((CTX_LEN, 128), F32), jnp.zeros((CTX_LEN, 256), F32)], axis=1)
    return jnp.concatenate([lat, ident], axis=0)


def _prep_layer_weights(w_in, wq_up, wq_rope, wk_up, wv_up, w_out, peer_wq, peer_subkeys):
    p32 = _deinterleave(MLA_ROPE_DIM)
    p64 = _deinterleave(DIFF_HEAD_DIM)
    z = lambda w: jnp.zeros((D_MODEL, w), F32)
    kr = w_in[:, 640:672][:, p32]
    qd = w_in[:, 672:1184].reshape(D_MODEL, 8, 64)[:, :, p64].reshape(D_MODEL, 512)
    kd = w_in[:, 1184:1696].reshape(D_MODEL, 8, 64)[:, :, p64].reshape(D_MODEL, 512)
    win = jnp.concatenate([w_in[:, 0:640], z(64), kr, z(32), qd, kd, w_in[:, 1696:2208]], axis=1)
    qn = wq_up.reshape(MLA_Q_RANK, MLA_HEADS, MLA_HEAD_DIM)
    qr = wq_rope.reshape(MLA_Q_RANK, MLA_HEADS, MLA_ROPE_DIM)[:, :, p32]
    wq = jnp.concatenate([qn, qr, jnp.zeros((MLA_Q_RANK, MLA_HEADS, 32), F32)], axis=2)
    kn = wk_up.reshape(MLA_KV_RANK, MLA_HEADS, MLA_HEAD_DIM)
    wk = jnp.concatenate([kn, jnp.zeros((MLA_KV_RANK, MLA_HEADS, 64), F32)], axis=2)
    return dict(
        win=win.astype(BF16),
        wq=wq.reshape(MLA_Q_RANK, MLA_HEADS * LANES).astype(BF16),
        wk=wk.reshape(MLA_KV_RANK, MLA_HEADS * LANES).astype(BF16),
        wv=wv_up.astype(BF16),
        wo_m=w_out[:512].astype(BF16),
        wo_d=w_out[512:].astype(BF16),
        pwq=peer_wq.astype(BF16),
        sk=peer_subkeys.astype(BF16),
    )


def kernel(x, c, ctx, c_ctx, norm_attn_g, norm_ffn_g, w_ada, b_ada, w_in, mla_q_norm_g, mla_wq_up, mla_wq_rope, mla_kv_norm_g, mla_wk_up, mla_wv_up, diff_lambda, diff_subnorm_g, w_out, peer_wq, peer_subkeys, peer_u, peer_v, final_norm_g):
    xs = jnp.concatenate([x.reshape(T_LAT, D_MODEL), ctx.reshape(T_CTX, D_MODEL)], axis=0)
    cc = jnp.concatenate([c, c_ctx[None, :], jnp.zeros((MOD_ROWS - BATCH - 1, D_MODEL), F32)], axis=0)
    mod = _modulation(cc, w_ada, b_ada)
    lam_inits = [0.8 - 0.6 * math.exp(-0.3 * l) for l in range(DEPTH)]
    lam_all = _diff_lambda(diff_lambda, jnp.broadcast_to(jnp.asarray(lam_inits, F32)[:, None], (DEPTH, LANES)))
    tab = _rope_tables()

    peer_out = None
    g2_prev = None
    for l in range(DEPTH):
        last = l == DEPTH - 1
        w = _prep_layer_weights(w_in[l], mla_wq_up[l], mla_wq_rope[l], mla_wk_up[l], mla_wv_up[l],
                                w_out[l], peer_wq[l], peer_subkeys[l])
        m = mod[l].reshape(MOD_ROWS, N_MOD, 1, D_MODEL)
        sh1, sc1, g1, sh2, sc2, g2 = (m[:, j] for j in range(N_MOD))
        xs, (qm, km, vm, qd, kd, vd) = _qkv(
            xs, peer_out, g2_prev, norm_attn_g[l][None, :], sc1, sh1, w["win"],
            mla_q_norm_g[l][None, :], mla_kv_norm_g[l][None, :], w["wq"], w["wk"], w["wv"], tab)
        om = _attention("mla", qm, km, vm, not last)
        od = _attention("diff", qd, kd, vd, not last, lam=lam_all[l][None, :],
                        g=diff_subnorm_g[l][None, :], lam_scale=1.0 - lam_inits[l])
        n_blocks = NB_LAT if last else NB_ALL
        n = n_blocks * TM
        xs, h2, eidx_t, gate_t = _mid(n_blocks, xs, om, od, w["wo_m"], w["wo_d"], g1,
                                      norm_ffn_g[l][None, :], sc2, sh2, w["pwq"], w["sk"])
        idx = eidx_t.T
        gate3 = gate_t.reshape(16, 8, n // 8, 8).transpose(2, 1, 3, 0).reshape(n // 8, 8, LANES)
        act3 = _peer_u(n, idx, h2.reshape(n, SUBLANES, LANES), gate3, _pack_table(peer_u[l]))
        peer_out = _peer_v(n, idx, act3, _pack_table(peer_v[l])).reshape(n, D_MODEL)
        g2_prev = g2
    out = _final(xs, peer_out, g2_prev, final_norm_g[None, :])
    return out.reshape(BATCH, SEQ, D_MODEL)
```

```python
import functools
import math

import jax
import jax.numpy as jnp
import numpy as np
from jax import lax
from jax.experimental import pallas as pl
from jax.experimental.pallas import tpu as pltpu

F32 = jnp.float32
BF16 = jnp.bfloat16

D_MODEL = 1024
BATCH = 8
SEQ = 4096
DEPTH = 4
GRID_W = 64
CTX_LEN = 256
N_MOD = 6
NORM_EPS = 1e-6
ROPE_BASE = 10000.0
MLA_HEADS = 8
MLA_HEAD_DIM = 64
MLA_ROPE_DIM = 32
MLA_Q_RANK = 384
MLA_KV_RANK = 256
DIFF_HEADS = 4
DIFF_HEAD_DIM = 64
PEER_HEADS = 8
PEER_TOPK = 16
N_KEYS = 128
N_EXPERTS = N_KEYS * N_KEYS

T_LAT = BATCH * SEQ
T_CTX = BATCH * CTX_LEN
TT = T_LAT + T_CTX
TM = 256
NB_LAT = T_LAT // TM
NB_ALL = TT // TM
BLK_PER_BATCH = SEQ // TM
MOD_ROWS = 16
PROJ_PAD = 2304
TQ = 256
TK = 512
PEER_TB = 128
LANES = 128
SUBLANES = 8
VMEM_LIMIT = 56 * 1024 * 1024


def _cparams(sem, vmem=VMEM_LIMIT):
    return pltpu.CompilerParams(dimension_semantics=sem, vmem_limit_bytes=vmem)


def _rms(x):
    return x * lax.rsqrt(jnp.mean(x * x, axis=-1, keepdims=True) + NORM_EPS)


def _bid(i):
    return jnp.where(i < NB_LAT, i // BLK_PER_BATCH, BATCH)


def _posblk(i):
    return jnp.where(i < NB_LAT, i % BLK_PER_BATCH, BLK_PER_BATCH)


def _mod_body(cc_ref, w_ref, b_ref, o_ref):
    cc = cc_ref[...]
    s = cc / (1.0 + jnp.exp(-cc))
    o_ref[0] = jnp.dot(s, w_ref[0], precision=lax.Precision.HIGHEST,
                       preferred_element_type=F32) + b_ref[0]


def _modulation(cc, w_ada, b_ada):
    nj = N_MOD
    return pl.pallas_call(
        _mod_body,
        grid=(DEPTH, nj),
        in_specs=[
            pl.BlockSpec((MOD_ROWS, D_MODEL), lambda l, j: (0, 0)),
            pl.BlockSpec((1, D_MODEL, D_MODEL), lambda l, j: (l, 0, j)),
            pl.BlockSpec((1, 1, D_MODEL), lambda l, j: (l, 0, j)),
        ],
        out_specs=pl.BlockSpec((1, MOD_ROWS, D_MODEL), lambda l, j: (l, 0, j)),
        out_shape=jax.ShapeDtypeStruct((DEPTH, MOD_ROWS, N_MOD * D_MODEL), F32),
        compiler_params=_cparams(("arbitrary", "arbitrary")),
        name="modulation",
    )(cc, w_ada, b_ada.reshape(DEPTH, 1, N_MOD * D_MODEL))


def _lam_body(d0_ref, d1_ref, d2_ref, d3_ref, li_ref, o_ref):
    a = jnp.sum(d0_ref[...] * d1_ref[...], axis=-1, keepdims=True)
    b = jnp.sum(d2_ref[...] * d3_ref[...], axis=-1, keepdims=True)
    o_ref[...] = jnp.exp(a) - jnp.exp(b) + li_ref[...]


def _diff_lambda(diff_lambda, lam_init):
    dl = diff_lambda.astype(F32)
    return pl.pallas_call(
        _lam_body,
        out_shape=jax.ShapeDtypeStruct((DEPTH, LANES), F32),
        name="diff_lambda",
    )(dl[:, 0], dl[:, 1], dl[:, 2], dl[:, 3], lam_init)


def _rope(xb, c, sp, sm, shift):
    return xb * c + pltpu.roll(xb, shift, 1) * sp + pltpu.roll(xb, LANES - shift, 1) * sm


def _qkv_body(has_peer, *refs):
    if has_peer:
        x_ref, peer_ref, g2_ref = refs[:3]
        refs = refs[3:]
    else:
        x_ref = refs[0]
        refs = refs[1:]
    (ng_ref, sc_ref, sh_ref, win_ref, gq_ref, gkv_ref, wq_ref, wk_ref, wv_ref, tab_ref) = refs[:10]
    outs = refs[10:]
    if has_peer:
        xo_ref, outs = outs[0], outs[1:]
    qm_ref, km_ref, vm_ref, qd_ref, kd_ref, vd_ref = outs

    x = x_ref[...]
    if has_peer:
        x = x + g2_ref[0] * peer_ref[...]
        xo_ref[...] = x
    h = _rms(x) * ng_ref[...] * (1.0 + sc_ref[0]) + sh_ref[0]
    proj = jnp.dot(h.astype(BF16), win_ref[...], preferred_element_type=F32)
    cq = _rms(proj[:, 0:384]) * gq_ref[...]
    ckv = _rms(proj[:, 384:640]) * gkv_ref[...]
    kr = proj[:, 640:768]
    q = jnp.dot(cq.astype(BF16), wq_ref[...], preferred_element_type=F32)
    ckv16 = ckv.astype(BF16)
    k = jnp.dot(ckv16, wk_ref[...], preferred_element_type=F32)
    v = jnp.dot(ckv16, wv_ref[...], preferred_element_type=F32)
    tab = tab_ref[...]
    cm, spm, smm = tab[:, 0:128], tab[:, 128:256], tab[:, 256:384]
    cd, spd, smd = tab[:, 384:512], tab[:, 512:640], tab[:, 640:768]
    kr_rot = _rope(kr, cm, spm, smm, MLA_ROPE_DIM // 2)
    scale_m = (MLA_HEAD_DIM + MLA_ROPE_DIM) ** -0.5
    for hh in range(MLA_HEADS):
        sl = slice(hh * LANES, (hh + 1) * LANES)
        qm_ref[:, sl] = (_rope(q[:, sl], cm, spm, smm, MLA_ROPE_DIM // 2) * scale_m).astype(BF16)
        km_ref[:, sl] = (k[:, sl] + kr_rot).astype(BF16)
    vm_ref[...] = v.T.astype(BF16)
    scale_d = DIFF_HEAD_DIM ** -0.5
    for hh in range(DIFF_HEADS):
        sl = slice(hh * LANES, (hh + 1) * LANES)
        qd = proj[:, 768 + hh * LANES:768 + (hh + 1) * LANES]
        kd = proj[:, 1280 + hh * LANES:1280 + (hh + 1) * LANES]
        qd_ref[:, sl] = (_rope(qd, cd, spd, smd, DIFF_HEAD_DIM // 2) * scale_d).astype(BF16)
        kd_ref[:, sl] = _rope(kd, cd, spd, smd, DIFF_HEAD_DIM // 2).astype(BF16)
    vd_ref[...] = proj[:, 1792:2304].T.astype(BF16)


def _qkv(x, peer, g2, ng, sc, sh, win, gq, gkv, wq, wk, wv, tab):
    has_peer = peer is not None
    row = lambda i: (i, 0)
    modrow = lambda i: (_bid(i), 0, 0)
    const = lambda i: (0, 0)
    in_specs = [pl.BlockSpec((TM, D_MODEL), row)]
    args = [x]
    if has_peer:
        in_specs += [pl.BlockSpec((TM, D_MODEL), row), pl.BlockSpec((1, 1, D_MODEL), modrow)]
        args += [peer, g2]
    in_specs += [
        pl.BlockSpec((1, D_MODEL), const),
        pl.BlockSpec((1, 1, D_MODEL), modrow),
        pl.BlockSpec((1, 1, D_MODEL), modrow),
        pl.BlockSpec((D_MODEL, PROJ_PAD), const),
        pl.BlockSpec((1, MLA_Q_RANK), const),
        pl.BlockSpec((1, MLA_KV_RANK), const),
        pl.BlockSpec((MLA_Q_RANK, MLA_HEADS * LANES), const),
        pl.BlockSpec((MLA_KV_RANK, MLA_HEADS * LANES), const),
        pl.BlockSpec((MLA_KV_RANK, 512), const),
        pl.BlockSpec((TM, 768), lambda i: (_posblk(i), 0)),
    ]
    args += [ng, sc, sh, win, gq, gkv, wq, wk, wv, tab]
    out_specs, out_shape = [], []
    if has_peer:
        out_specs.append(pl.BlockSpec((TM, D_MODEL), row))
        out_shape.append(jax.ShapeDtypeStruct((TT, D_MODEL), F32))
    for width, transposed in ((1024, False), (1024, False), (512, True),
                              (512, False), (512, False), (512, True)):
        if transposed:
            out_specs.append(pl.BlockSpec((width, TM), lambda i: (0, i)))
            out_shape.append(jax.ShapeDtypeStruct((width, TT), BF16))
        else:
            out_specs.append(pl.BlockSpec((TM, width), row))
            out_shape.append(jax.ShapeDtypeStruct((TT, width), BF16))
    res = pl.pallas_call(
        functools.partial(_qkv_body, has_peer),
        grid=(NB_ALL,),
        in_specs=in_specs,
        out_specs=out_specs,
        out_shape=out_shape,
        compiler_params=_cparams(("parallel",)),
        name="qkv",
    )(*args)
    if has_peer:
        return res[0], res[1:]
    return x, res


def _flash_pair(qs, ksls, kc_ref, kl_ref, vtc_ref, vtl_ref, n_pairs, sa_scr, sb_scr):
    n_lat = SEQ // TK

    def col_reduce(parts, op, final):
        while len(parts) > 1:
            parts = [op(parts[i], parts[i + 1]) for i in range(0, len(parts), 2)]
        return final(parts[0], axis=0, keepdims=True)

    def row_groups(x):
        return [x[i:i + SUBLANES] for i in range(0, x.shape[0], SUBLANES)]

    def scores(k_of):
        return [lax.dot_general(k_of(ksl), q, (((1,), (1,)), ((), ())), preferred_element_type=F32)
                for q, ksl in zip(qs, ksls)]

    def softmax_pv(read_s, vt, carries):
        mid = []
        for h, (m, l, acc) in enumerate(carries):
            m_new = jnp.maximum(m, col_reduce(row_groups(read_s(h)), jnp.maximum, jnp.max))
            alpha = jnp.exp(m - m_new)
            p = jnp.exp(read_s(h) - m_new)
            l = alpha * l + col_reduce(row_groups(p), jnp.add, jnp.sum)
            mid.append((m_new, l, alpha * acc, p.astype(BF16)))
        return tuple((m_new, l, acc + jnp.dot(vt, p, preferred_element_type=F32))
                     for (m_new, l, acc, p) in mid)

    def k_lat(c):
        r0 = pl.multiple_of(c * TK, TK)
        return lambda ksl: kl_ref[pl.ds(r0, TK), ksl]

    def vt_lat(c):
        return vtl_ref[:, pl.ds(pl.multiple_of(c * TK, TK), TK)]

    def stage(scr, sc):
        for h, x in enumerate(sc):
            scr[h] = x

    init = tuple((jnp.full((1, TQ), -jnp.inf, F32), jnp.zeros((1, TQ), F32),
                  jnp.zeros((LANES, TQ), F32)) for _ in qs)
    ctx_scores = scores(lambda ksl: kc_ref[:, ksl])
    carries = softmax_pv(lambda h: ctx_scores[h], vtc_ref[...], init)
    stage(sa_scr, scores(k_lat(0)))

    def pair(j, carries):
        c = 2 * j
        stage(sb_scr, scores(k_lat(c + 1)))
        carries = softmax_pv(lambda h: sa_scr[h], vt_lat(c), carries)
        stage(sa_scr, scores(k_lat(jnp.minimum(c + 2, n_lat - 1))))
        return softmax_pv(lambda h: sb_scr[h], vt_lat(c + 1), carries)

    carries = lax.fori_loop(0, n_pairs, pair, carries)
    return [acc / l for (m, l, acc) in carries]


def _n_pairs(nq):
    return jnp.where(pl.program_id(2) < nq, SEQ // (2 * TK), 0)


def _attn_mla_body(nq, q_ref, kc_ref, kl_ref, vtc_ref, vtl_ref, o_ref, sa_scr, sb_scr):
    qs = [q_ref[:, 0:LANES], q_ref[:, LANES:2 * LANES]]
    ksls = [slice(0, LANES), slice(LANES, 2 * LANES)]
    ot0, ot1 = _flash_pair(qs, ksls, kc_ref, kl_ref, vtc_ref, vtl_ref, _n_pairs(nq), sa_scr, sb_scr)
    row = lax.broadcasted_iota(jnp.int32, (LANES, TQ), 0)
    o_ref[...] = jnp.where(row < MLA_HEAD_DIM, ot0, ot1).T.astype(o_ref.dtype)


def _attn_diff_body(nq, lam_scale, q_ref, kc_ref, kl_ref, vtc_ref, vtl_ref, lam_ref, g_ref, o_ref,
                    sa_scr, sb_scr):
    lane = lax.broadcasted_iota(jnp.int32, (TQ, LANES), 1)
    q = q_ref[...]
    zero = jnp.zeros_like(q)
    qs = [jnp.where(lane < DIFF_HEAD_DIM, q, zero), jnp.where(lane < DIFF_HEAD_DIM, zero, q)]
    ksls = [slice(0, LANES), slice(0, LANES)]
    ot0, ot1 = _flash_pair(qs, ksls, kc_ref, kl_ref, vtc_ref, vtl_ref, _n_pairs(nq), sa_scr, sb_scr)
    o = ot0.T - lam_ref[...] * ot1.T
    o_ref[...] = (_rms(o) * g_ref[...] * lam_scale).astype(o_ref.dtype)


def _attention(kind, q, k, vt, with_ctx, lam=None, g=None, lam_scale=None):
    nq = SEQ // TQ
    steps = nq + 1 if with_ctx else nq
    qw = 2 * LANES if kind == "mla" else LANES
    ctx0 = T_LAT // CTX_LEN

    def qrow(b, p, i):
        return (jnp.where(i < nq, b * nq + i, (T_LAT // TQ) + b), p)

    in_specs = [
        pl.BlockSpec((TQ, qw), qrow),
        pl.BlockSpec((CTX_LEN, qw), lambda b, p, i: (ctx0 + b, p)),
        pl.BlockSpec((SEQ, qw), lambda b, p, i: (b, p)),
        pl.BlockSpec((LANES, CTX_LEN), lambda b, p, i: (p, ctx0 + b)),
        pl.BlockSpec((LANES, SEQ), lambda b, p, i: (p, b)),
    ]
    args = [q, k, k, vt, vt]
    if kind == "mla":
        body = functools.partial(_attn_mla_body, nq)
    else:
        body = functools.partial(_attn_diff_body, nq, lam_scale)
        in_specs += [pl.BlockSpec((1, LANES), lambda b, p, i: (0, 0)),
                     pl.BlockSpec((1, LANES), lambda b, p, i: (0, 0))]
        args += [lam, g]
    rows = TT if with_ctx else T_LAT
    return pl.pallas_call(
        body,
        grid=(BATCH, 4, steps),
        in_specs=in_specs,
        out_specs=pl.BlockSpec((TQ, LANES), qrow),
        out_shape=jax.ShapeDtypeStruct((rows, 512), BF16),
        scratch_shapes=[pltpu.VMEM((2, TK, TQ), F32), pltpu.VMEM((2, TK, TQ), F32)],
        compiler_params=_cparams(("parallel", "parallel", "arbitrary")),
        name="attn_" + kind,
    )(*args)


def _top16(s, payload=None):
    n_rows = s.shape[0]
    rowf = lax.broadcasted_iota(jnp.int32, s.shape, 0).astype(F32)
    slot = lax.broadcasted_iota(jnp.int32, (PEER_TOPK, s.shape[1]), 0)
    vals = jnp.zeros((PEER_TOPK, s.shape[1]), F32)
    picks = jnp.zeros((PEER_TOPK, s.shape[1]), F32)
    for r in range(PEER_TOPK):
        m = jnp.max(s, axis=0, keepdims=True)
        am = jnp.min(jnp.where(s == m, rowf, float(n_rows)), axis=0, keepdims=True)
        hit = rowf == am
        pick = am if payload is None else jnp.max(jnp.where(hit, payload, -1.0), axis=0, keepdims=True)
        vals = jnp.where(slot == r, m, vals)
        picks = jnp.where(slot == r, pick, picks)
        s = jnp.where(hit, -jnp.inf, s)
    return vals, picks


def _staircase(a16, b16, combine, pad):
    tm = a16.shape[1]
    sub = lax.broadcasted_iota(jnp.int32, (SUBLANES, tm), 0)
    a_lo, a_hi = a16[0:SUBLANES], a16[SUBLANES:]
    b_lo, b_hi = b16[0:SUBLANES], b16[SUBLANES:]
    row = lambda x, r: jnp.broadcast_to(x[r:r + 1], (SUBLANES, tm))
    take = lambda x, idx: jnp.take_along_axis(x, idx, axis=0)
    a3 = jnp.where(sub < 5, 2, 3)
    b3 = jnp.where(sub < 5, sub, sub - 5)
    a4 = jnp.where(sub < 1, 3, jnp.where(sub < 4, 4, jnp.where(sub < 6, 5, 6)))
    b4 = jnp.where(sub < 1, 3, jnp.where(sub < 4, sub - 1, jnp.where(sub < 6, sub - 4, sub - 6)))
    pieces = [
        combine(row(a_lo, 0), b_lo),
        combine(row(a_lo, 0), b_hi),
        combine(row(a_lo, 1), b_lo),
        combine(take(a_lo, a3), take(b_lo, b3)),
        combine(take(a_lo, a4), take(b_lo, b4)),
        jnp.where(sub < 2, combine(row(a_lo, 7), b_lo), pad),
        combine(a_hi, row(b_lo, 0)),
    ]
    return jnp.concatenate(pieces, axis=0)


def _mid_body(x_ref, om_ref, od_ref, wo_m_ref, wo_d_ref, g1_ref, ng_ref, sc_ref, sh_ref,
              wq_ref, sk_ref, xo_ref, h_ref, eidx_ref, gate_ref, q_scr):
    y = (jnp.dot(om_ref[...], wo_m_ref[...], preferred_element_type=F32)
         + jnp.dot(od_ref[...], wo_d_ref[...], preferred_element_type=F32))
    x = x_ref[...] + g1_ref[0] * y
    xo_ref[...] = x
    h = _rms(x) * ng_ref[...] * (1.0 + sc_ref[0]) + sh_ref[0]
    h_ref[...] = h
    q_scr[...] = jnp.dot(h.astype(BF16), wq_ref[...], preferred_element_type=F32).astype(BF16)

    def head(hh, _):
        sv, si = [], []
        for c in range(2):
            c0 = pl.multiple_of(hh * 2 * N_KEYS + c * N_KEYS, N_KEYS)
            qh = q_scr[:, pl.ds(c0, N_KEYS)]
            s = lax.dot_general(sk_ref[c], qh, (((1,), (1,)), ((), ())), preferred_element_type=F32)
            vals, keys = _top16(s)
            sv.append(vals)
            si.append(keys)
        cand_s = _staircase(sv[0], sv[1], lambda a, b: a + b, -jnp.inf)
        cand_e = _staircase(si[0], si[1], lambda a, b: a * float(N_KEYS) + b, -1.0)
        top_s, top_e = _top16(cand_s, cand_e)
        ex = jnp.exp(top_s - jnp.max(top_s, axis=0, keepdims=True))
        gate = ex / jnp.sum(ex, axis=0, keepdims=True)
        r0 = pl.multiple_of(hh * PEER_TOPK, PEER_TOPK)
        eidx_ref[pl.ds(r0, PEER_TOPK), :] = top_e.astype(jnp.int32) * ROWS_PER_EXPERT
        gate_ref[pl.ds(r0, PEER_TOPK), :] = gate
        return 0

    lax.fori_loop(0, PEER_HEADS, head, 0)


def _mid(n_blocks, x, om, od, wo_m, wo_d, g1, ng, sc, sh, wq, sk):
    n = n_blocks * TM
    row = lambda i: (i, 0)
    modrow = lambda i: (_bid(i), 0, 0)
    const = lambda i: (0, 0)
    return pl.pallas_call(
        _mid_body,
        grid=(n_blocks,),
        in_specs=[
            pl.BlockSpec((TM, D_MODEL), row),
            pl.BlockSpec((TM, 512), row),
            pl.BlockSpec((TM, 512), row),
            pl.BlockSpec((512, D_MODEL), const),
            pl.BlockSpec((512, D_MODEL), const),
            pl.BlockSpec((1, 1, D_MODEL), modrow),
            pl.BlockSpec((1, D_MODEL), const),
            pl.BlockSpec((1, 1, D_MODEL), modrow),
            pl.BlockSpec((1, 1, D_MODEL), modrow),
            pl.BlockSpec((D_MODEL, PEER_HEADS * 2 * N_KEYS), const),
            pl.BlockSpec((2, N_KEYS, N_KEYS), lambda i: (0, 0, 0)),
        ],
        out_specs=[
            pl.BlockSpec((TM, D_MODEL), row),
            pl.BlockSpec((TM, D_MODEL), row),
            pl.BlockSpec((PEER_HEADS * PEER_TOPK, TM), lambda i: (0, i)),
            pl.BlockSpec((PEER_HEADS * PEER_TOPK, TM), lambda i: (0, i)),
        ],
        out_shape=[
            jax.ShapeDtypeStruct((n, D_MODEL), F32),
            jax.ShapeDtypeStruct((n, D_MODEL), F32),
            jax.ShapeDtypeStruct((PEER_HEADS * PEER_TOPK, n), jnp.int32),
            jax.ShapeDtypeStruct((PEER_HEADS * PEER_TOPK, n), F32),
        ],
        scratch_shapes=[pltpu.VMEM((TM, PEER_HEADS * 2 * N_KEYS), BF16)],
        compiler_params=_cparams(("parallel",)),
        name="mid",
    )(x, om, od, wo_m, wo_d, g1, ng, sc, sh, wq, sk)


N_SEL = PEER_HEADS * PEER_TOPK
HALF = D_MODEL // 2
ROWS_PER_EXPERT = HALF // LANES
_BFLY_ORDER = (0, 4, 2, 6, 1, 5, 3, 7)


def _pack_table(tab):
    u = lax.bitcast_convert_type(tab.astype(BF16), jnp.uint16).astype(jnp.uint32)
    w = u[:, :HALF] | (u[:, HALF:] << 16)
    return w.reshape(tab.shape[0] * ROWS_PER_EXPERT, LANES)


def _load_pair(tab_ref, ra, rb):
    w = jnp.concatenate([tab_ref[pl.ds(pl.multiple_of(ra, ROWS_PER_EXPERT), ROWS_PER_EXPERT), :],
                         tab_ref[pl.ds(pl.multiple_of(rb, ROWS_PER_EXPERT), ROWS_PER_EXPERT), :]], axis=0)
    lo = pltpu.bitcast(w << 16, F32)
    hi = pltpu.bitcast(w & jnp.uint32(0xFFFF0000), F32)
    return lo, hi


def _peer_u_body(idx_ref, x_ref, gate_ref, tab_ref, act_ref, ra_scr, rb_scr):
    sub = lax.broadcasted_iota(jnp.int32, (SUBLANES, LANES), 0)
    lane = lax.broadcasted_iota(jnp.int32, (SUBLANES, LANES), 1)
    top = sub < 4
    n_grp = N_SEL // SUBLANES
    n_tiles = PEER_TB // SUBLANES

    def fold(p, q, width):
        tp = p + pltpu.roll(p, SUBLANES - width, 0)
        tq = q + pltpu.roll(q, SUBLANES - width, 0)
        return jnp.where((sub % (2 * width)) < width, tp, pltpu.roll(tq, width, 0))

    def gather_tile(tb, cur_scr, prev_scr):
        def token(tt, a):
            t = tb * SUBLANES + tt
            for i in range(n_grp):
                j = tt * n_grp + i
                a = jnp.where(lane == j, jnp.sum(prev_scr[j], axis=1, keepdims=True), a)
            xt = x_ref[t]
            xsw = pltpu.roll(xt, 4, 0)
            xlo = jnp.where(top, xt, xsw)
            xhi = jnp.where(top, xsw, xt)
            for g in range(n_grp):
                parts = []
                for pr in range(4):
                    ea = idx_ref[t, g * SUBLANES + _BFLY_ORDER[2 * pr]]
                    eb = idx_ref[t, g * SUBLANES + _BFLY_ORDER[2 * pr + 1]]
                    lo, hi = _load_pair(tab_ref, ea, eb)
                    parts.append(lo * xlo + hi * xhi)
                cur_scr[tt * n_grp + g] = fold(fold(parts[0], parts[1], 2), fold(parts[2], parts[3], 2), 1)
            return a

        return lax.fori_loop(0, SUBLANES, token, jnp.zeros((SUBLANES, LANES), F32))

    def emit(tb, scores):
        act_ref[tb] = 0.5 * scores * (1.0 + lax.erf(scores * (2.0 ** -0.5))) * gate_ref[tb]

    rb_scr[...] = jnp.zeros(rb_scr.shape, F32)

    def tile_pair(jp, _):
        prev_scores = gather_tile(2 * jp, ra_scr, rb_scr)

        @pl.when(jp > 0)
        def _():
            emit(2 * jp - 1, prev_scores)

        emit(2 * jp, gather_tile(2 * jp + 1, rb_scr, ra_scr))
        return 0

    lax.fori_loop(0, n_tiles // 2, tile_pair, 0)
    last = jnp.zeros((SUBLANES, LANES), F32)
    for j in range(SUBLANES * n_grp):
        last = jnp.where(lane == j, jnp.sum(rb_scr[j], axis=1, keepdims=True), last)
    emit(n_tiles - 1, last)


def _peer_v_body(idx_ref, act_ref, tab_ref, o_ref):
    sub = lax.broadcasted_iota(jnp.int32, (SUBLANES, LANES), 0)
    lane = lax.broadcasted_iota(jnp.int32, (SUBLANES, LANES), 1)
    top = sub < 4
    n_grp = N_SEL // SUBLANES
    pair_rows = [jnp.where(top, 2 * p, 2 * p + 1) for p in range(4)]

    def tile(tb, _):
        a_tile = act_ref[tb]

        for tt in range(SUBLANES):
            t = tb * SUBLANES + tt
            acc = [jnp.zeros((SUBLANES, LANES), F32) for _ in range(4)]
            for g in range(n_grp):
                col = jnp.sum(jnp.where(lane == tt * n_grp + g, a_tile, 0.0), axis=1, keepdims=True)
                actg = jnp.broadcast_to(col, (SUBLANES, LANES))
                for p in range(4):
                    ea = idx_ref[t, g * SUBLANES + 2 * p]
                    eb = idx_ref[t, g * SUBLANES + 2 * p + 1]
                    lo, hi = _load_pair(tab_ref, ea, eb)
                    av = jnp.take_along_axis(actg, pair_rows[p], axis=0)
                    k = 2 * (p % 2)
                    acc[k] = acc[k] + av * lo
                    acc[k + 1] = acc[k + 1] + av * hi
            lo = acc[0] + acc[2]
            hi = acc[1] + acc[3]
            lo = lo + pltpu.roll(lo, 4, 0)
            hi = hi + pltpu.roll(hi, 4, 0)
            o_ref[t] = jnp.where(top, lo, hi)
        return 0

    lax.fori_loop(0, PEER_TB // SUBLANES, tile, 0)


def _table_spec():
    return pl.BlockSpec((N_EXPERTS * ROWS_PER_EXPERT, LANES), lambda i: (0, 0),
                        pipeline_mode=pl.Buffered(1))


def _peer_u(n, idx, x3, gate3, tab):
    return pl.pallas_call(
        _peer_u_body,
        grid=(n // PEER_TB,),
        in_specs=[
            pl.BlockSpec((PEER_TB, N_SEL), lambda i: (i, 0), memory_space=pltpu.SMEM),
            pl.BlockSpec((PEER_TB, SUBLANES, LANES), lambda i: (i, 0, 0)),
            pl.BlockSpec((PEER_TB // SUBLANES, SUBLANES, LANES), lambda i: (i, 0, 0)),
            _table_spec(),
        ],
        out_specs=pl.BlockSpec((PEER_TB // SUBLANES, SUBLANES, LANES), lambda i: (i, 0, 0)),
        out_shape=jax.ShapeDtypeStruct((n // SUBLANES, SUBLANES, LANES), F32),
        scratch_shapes=[pltpu.VMEM((N_SEL, SUBLANES, LANES), F32),
                        pltpu.VMEM((N_SEL, SUBLANES, LANES), F32)],
        compiler_params=_cparams(("arbitrary",)),
        name="peer_u",
    )(idx, x3, gate3, tab)


def _peer_v(n, idx, act3, tab):
    return pl.pallas_call(
        _peer_v_body,
        grid=(n // PEER_TB,),
        in_specs=[
            pl.BlockSpec((PEER_TB, N_SEL), lambda i: (i, 0), memory_space=pltpu.SMEM),
            pl.BlockSpec((PEER_TB // SUBLANES, SUBLANES, LANES), lambda i: (i, 0, 0)),
            _table_spec(),
        ],
        out_specs=pl.BlockSpec((PEER_TB, SUBLANES, LANES), lambda i: (i, 0, 0)),
        out_shape=jax.ShapeDtypeStruct((n, SUBLANES, LANES), F32),
        compiler_params=_cparams(("arbitrary",)),
        name="peer_v",
    )(idx, act3, tab)


def _final_body(x_ref, peer_ref, g2_ref, g_ref, o_ref):
    x = x_ref[...] + g2_ref[0] * peer_ref[...]
    o_ref[...] = _rms(x) * g_ref[...]


def _final(x, peer, g2, g):
    row = lambda i: (i, 0)
    return pl.pallas_call(
        _final_body,
        grid=(NB_LAT,),
        in_specs=[
            pl.BlockSpec((TM, D_MODEL), row),
            pl.BlockSpec((TM, D_MODEL), row),
            pl.BlockSpec((1, 1, D_MODEL), lambda i: (_bid(i), 0, 0)),
            pl.BlockSpec((1, D_MODEL), lambda i: (0, 0)),
        ],
        out_specs=pl.BlockSpec((TM, D_MODEL), row),
        out_shape=jax.ShapeDtypeStruct((T_LAT, D_MODEL), F32),
        compiler_params=_cparams(("parallel",)),
        name="final_norm",
    )(x, peer, g2, g)


def _deinterleave(width):
    return np.concatenate([np.arange(0, width, 2), np.arange(1, width, 2)])


def _rope_tables():
    pos = np.arange(SEQ)
    rowp = jnp.asarray(pos // GRID_W, F32)
    colp = jnp.asarray(pos % GRID_W, F32)

    def angles(dim):
        quarter = dim // 4
        inv = ROPE_BASE ** (-jnp.arange(quarter, dtype=F32) / quarter)
        return jnp.concatenate([rowp[:, None] * inv, colp[:, None] * inv], axis=-1)

    am = angles(MLA_ROPE_DIM)
    ad = angles(DIFF_HEAD_DIM)
    cm, sm_ = jnp.cos(am), jnp.sin(am)
    cd, sd = jnp.cos(ad), jnp.sin(ad)
    one = lambda w: jnp.ones((SEQ, w), F32)
    zero = lambda w: jnp.zeros((SEQ, w), F32)
    t_cm = jnp.concatenate([one(64), cm, cm, one(32)], axis=1)
    t_spm = jnp.concatenate([zero(80), sm_, zero(32)], axis=1)
    t_smm = jnp.concatenate([zero(64), -sm_, zero(48)], axis=1)
    t_cd = jnp.concatenate([cd, cd, cd, cd], axis=1)
    t_spd = jnp.concatenate([zero(32), sd, zero(32), sd], axis=1)
    t_smd = jnp.concatenate([-sd, zero(32), -sd, zero(32)], axis=1)
    lat = jnp.concatenate([t_cm, t_spm, t_smm, t_cd, t_spd, t_smd], axis=1)
    ident = jnp.concatenate([jnp.ones((CTX_LEN, 128), F32), jnp.zeros((CTX_LEN, 256), F32),
                             jnp.ones((CTX_LEN, 128), F32), jnp.zeros((CTX_LEN, 256), F32)], axis=1)
    return jnp.concatenate([lat, ident], axis=0)


def _prep_layer_weights(w_in, wq_up, wq_rope, wk_up, wv_up, w_out, peer_wq, peer_subkeys):
    p32 = _deinterleave(MLA_ROPE_DIM)
    p64 = _deinterleave(DIFF_HEAD_DIM)
    z = lambda w: jnp.zeros((D_MODEL, w), F32)
    kr = w_in[:, 640:672][:, p32]
    qd = w_in[:, 672:1184].reshape(D_MODEL, 8, 64)[:, :, p64].reshape(D_MODEL, 512)
    kd = w_in[:, 1184:1696].reshape(D_MODEL, 8, 64)[:, :, p64].reshape(D_MODEL, 512)
    win = jnp.concatenate([w_in[:, 0:640], z(64), kr, z(32), qd, kd, w_in[:, 1696:2208]], axis=1)
    qn = wq_up.reshape(MLA_Q_RANK, MLA_HEADS, MLA_HEAD_DIM)
    qr = wq_rope.reshape(MLA_Q_RANK, MLA_HEADS, MLA_ROPE_DIM)[:, :, p32]
    wq = jnp.concatenate([qn, qr, jnp.zeros((MLA_Q_RANK, MLA_HEADS, 32), F32)], axis=2)
    kn = wk_up.reshape(MLA_KV_RANK, MLA_HEADS, MLA_HEAD_DIM)
    wk = jnp.concatenate([kn, jnp.zeros((MLA_KV_RANK, MLA_HEADS, 64), F32)], axis=2)
    return dict(
        win=win.astype(BF16),
        wq=wq.reshape(MLA_Q_RANK, MLA_HEADS * LANES).astype(BF16),
        wk=wk.reshape(MLA_KV_RANK, MLA_HEADS * LANES).astype(BF16),
        wv=wv_up.astype(BF16),
        wo_m=w_out[:512].astype(BF16),
        wo_d=w_out[512:].astype(BF16),
        pwq=peer_wq.astype(BF16),
        sk=peer_subkeys.astype(BF16),
    )


def kernel(x, c, ctx, c_ctx, norm_attn_g, norm_ffn_g, w_ada, b_ada, w_in, mla_q_norm_g, mla_wq_up, mla_wq_rope, mla_kv_norm_g, mla_wk_up, mla_wv_up, diff_lambda, diff_subnorm_g, w_out, peer_wq, peer_subkeys, peer_u, peer_v, final_norm_g):
    xs = jnp.concatenate([x.reshape(T_LAT, D_MODEL), ctx.reshape(T_CTX, D_MODEL)], axis=0)
    cc = jnp.concatenate([c, c_ctx[None, :], jnp.zeros((MOD_ROWS - BATCH - 1, D_MODEL), F32)], axis=0)
    mod = _modulation(cc, w_ada, b_ada)
    lam_inits = [0.8 - 0.6 * math.exp(-0.3 * l) for l in range(DEPTH)]
    lam_all = _diff_lambda(diff_lambda, jnp.broadcast_to(jnp.asarray(lam_inits, F32)[:, None], (DEPTH, LANES)))
    tab = _rope_tables()

    peer_out = None
    g2_prev = None
    for l in range(DEPTH):
        last = l == DEPTH - 1
        w = _prep_layer_weights(w_in[l], mla_wq_up[l], mla_wq_rope[l], mla_wk_up[l], mla_wv_up[l],
                                w_out[l], peer_wq[l], peer_subkeys[l])
        m = mod[l].reshape(MOD_ROWS, N_MOD, 1, D_MODEL)
        sh1, sc1, g1, sh2, sc2, g2 = (m[:, j] for j in range(N_MOD))
        xs, (qm, km, vm, qd, kd, vd) = _qkv(
            xs, peer_out, g2_prev, norm_attn_g[l][None, :], sc1, sh1, w["win"],
            mla_q_norm_g[l][None, :], mla_kv_norm_g[l][None, :], w["wq"], w["wk"], w["wv"], tab)
        om = _attention("mla", qm, km, vm, not last)
        od = _attention("diff", qd, kd, vd, not last, lam=lam_all[l][None, :],
                        g=diff_subnorm_g[l][None, :], lam_scale=1.0 - lam_inits[l])
        n_blocks = NB_LAT if last else NB_ALL
        n = n_blocks * TM
        xs, h2, eidx_t, gate_t = _mid(n_blocks, xs, om, od, w["wo_m"], w["wo_d"], g1,
                                      norm_ffn_g[l][None, :], sc2, sh2, w["pwq"], w["sk"])
        idx = eidx_t.T
        gate3 = gate_t.reshape(16, 8, n // 8, 8).transpose(2, 1, 3, 0).reshape(n // 8, 8, LANES)
        act3 = _peer_u(n, idx, h2.reshape(n, SUBLANES, LANES), gate3, _pack_table(peer_u[l]))
        peer_out = _peer_v(n, idx, act3, _pack_table(peer_v[l])).reshape(n, D_MODEL)
        g2_prev = g2
    out = _final(xs, peer_out, g2_prev, final_norm_g[None, :])
    return out.reshape(BATCH, SEQ, D_MODEL)
```

```python
import functools
import math

import jax
import jax.numpy as jnp
import numpy as np
from jax import lax
from jax.experimental import pallas as pl
from jax.experimental.pallas import tpu as pltpu

F32 = jnp.float32
BF16 = jnp.bfloat16

D_MODEL = 1024
BATCH = 8
SEQ = 4096
DEPTH = 4
GRID_W = 64
CTX_LEN = 256
N_MOD = 6
NORM_EPS = 1e-6
ROPE_BASE = 10000.0
MLA_HEADS = 8
MLA_HEAD_DIM = 64
MLA_ROPE_DIM = 32
MLA_Q_RANK = 384
MLA_KV_RANK = 256
DIFF_HEADS = 4
DIFF_HEAD_DIM = 64
PEER_HEADS = 8
PEER_TOPK = 16
N_KEYS = 128
N_EXPERTS = N_KEYS * N_KEYS

T_LAT = BATCH * SEQ
T_CTX = BATCH * CTX_LEN
TT = T_LAT + T_CTX
TM = 256
NB_LAT = T_LAT // TM
NB_ALL = TT // TM
BLK_PER_BATCH = SEQ // TM
MOD_ROWS = 16
PROJ_PAD = 2304
TQ = 256
TK = 512
PEER_TB = 128
LANES = 128
SUBLANES = 8
VMEM_LIMIT = 56 * 1024 * 1024
LOG2_E = math.log2(math.e)


def _cparams(sem, vmem=VMEM_LIMIT):
    return pltpu.CompilerParams(dimension_semantics=sem, vmem_limit_bytes=vmem)


def _rms(x):
    return x * lax.rsqrt(jnp.mean(x * x, axis=-1, keepdims=True) + NORM_EPS)


def _bid(i):
    return jnp.where(i < NB_LAT, i // BLK_PER_BATCH, BATCH)


def _posblk(i):
    return jnp.where(i < NB_LAT, i % BLK_PER_BATCH, BLK_PER_BATCH)


def _mod_body(cc_ref, w_ref, b_ref, o_ref):
    cc = cc_ref[...]
    s = cc / (1.0 + jnp.exp(-cc))
    o_ref[0] = jnp.dot(s, w_ref[0], precision=lax.Precision.HIGHEST,
                       preferred_element_type=F32) + b_ref[0]


def _modulation(cc, w_ada, b_ada):
    nj = N_MOD
    return pl.pallas_call(
        _mod_body,
        grid=(DEPTH, nj),
        in_specs=[
            pl.BlockSpec((MOD_ROWS, D_MODEL), lambda l, j: (0, 0)),
            pl.BlockSpec((1, D_MODEL, D_MODEL), lambda l, j: (l, 0, j)),
            pl.BlockSpec((1, 1, D_MODEL), lambda l, j: (l, 0, j)),
        ],
        out_specs=pl.BlockSpec((1, MOD_ROWS, D_MODEL), lambda l, j: (l, 0, j)),
        out_shape=jax.ShapeDtypeStruct((DEPTH, MOD_ROWS, N_MOD * D_MODEL), F32),
        compiler_params=_cparams(("arbitrary", "arbitrary")),
        name="modulation",
    )(cc, w_ada, b_ada.reshape(DEPTH, 1, N_MOD * D_MODEL))


def _lam_body(d0_ref, d1_ref, d2_ref, d3_ref, li_ref, o_ref):
    a = jnp.sum(d0_ref[...] * d1_ref[...], axis=-1, keepdims=True)
    b = jnp.sum(d2_ref[...] * d3_ref[...], axis=-1, keepdims=True)
    o_ref[...] = jnp.exp(a) - jnp.exp(b) + li_ref[...]


def _diff_lambda(diff_lambda, lam_init):
    dl = diff_lambda.astype(F32)
    return pl.pallas_call(
        _lam_body,
        out_shape=jax.ShapeDtypeStruct((DEPTH, LANES), F32),
        name="diff_lambda",
    )(dl[:, 0], dl[:, 1], dl[:, 2], dl[:, 3], lam_init)


def _rope(xb, c, sp, sm, shift):
    return xb * c + pltpu.roll(xb, shift, 1) * sp + pltpu.roll(xb, LANES - shift, 1) * sm


def _qkv_body(has_peer, *refs):
    if has_peer:
        x_ref, peer_ref, g2_ref = refs[:3]
        refs = refs[3:]
    else:
        x_ref = refs[0]
        refs = refs[1:]
    (ng_ref, sc_ref, sh_ref, win_ref, gq_ref, gkv_ref, wq_ref, wk_ref, wv_ref, tab_ref) = refs[:10]
    outs = refs[10:]
    if has_peer:
        xo_ref, outs = outs[0], outs[1:]
    qm_ref, km_ref, vm_ref, qd_ref, kd_ref, vd_ref = outs

    x = x_ref[...]
    if has_peer:
        x = x + g2_ref[0] * peer_ref[...]
        xo_ref[...] = x
    h = _rms(x) * ng_ref[...] * (1.0 + sc_ref[0]) + sh_ref[0]
    proj = jnp.dot(h.astype(BF16), win_ref[...], preferred_element_type=F32)
    cq = _rms(proj[:, 0:384]) * gq_ref[...]
    ckv = _rms(proj[:, 384:640]) * gkv_ref[...]
    kr = proj[:, 640:768]
    q = jnp.dot(cq.astype(BF16), wq_ref[...], preferred_element_type=F32)
    ckv16 = ckv.astype(BF16)
    k = jnp.dot(ckv16, wk_ref[...], preferred_element_type=F32)
    v = jnp.dot(ckv16, wv_ref[...], preferred_element_type=F32)
    tab = tab_ref[...]
    cm, spm, smm = tab[:, 0:128], tab[:, 128:256], tab[:, 256:384]
    cd, spd, smd = tab[:, 384:512], tab[:, 512:640], tab[:, 640:768]
    kr_rot = _rope(kr, cm, spm, smm, MLA_ROPE_DIM // 2)
    scale_m = LOG2_E * (MLA_HEAD_DIM + MLA_ROPE_DIM) ** -0.5
    for hh in range(MLA_HEADS):
        sl = slice(hh * LANES, (hh + 1) * LANES)
        qm_ref[:, sl] = (_rope(q[:, sl], cm, spm, smm, MLA_ROPE_DIM // 2) * scale_m).astype(BF16)
        km_ref[:, sl] = (k[:, sl] + kr_rot).astype(BF16)
    vm_ref[...] = v.T.astype(BF16)
    scale_d = LOG2_E * DIFF_HEAD_DIM ** -0.5
    for hh in range(DIFF_HEADS):
        sl = slice(hh * LANES, (hh + 1) * LANES)
        qd = proj[:, 768 + hh * LANES:768 + (hh + 1) * LANES]
        kd = proj[:, 1280 + hh * LANES:1280 + (hh + 1) * LANES]
        qd_ref[:, sl] = (_rope(qd, cd, spd, smd, DIFF_HEAD_DIM // 2) * scale_d).astype(BF16)
        kd_ref[:, sl] = _rope(kd, cd, spd, smd, DIFF_HEAD_DIM // 2).astype(BF16)
    vd_ref[...] = proj[:, 1792:2304].T.astype(BF16)


def _qkv(x, peer, g2, ng, sc, sh, win, gq, gkv, wq, wk, wv, tab):
    has_peer = peer is not None
    row = lambda i: (i, 0)
    modrow = lambda i: (_bid(i), 0, 0)
    const = lambda i: (0, 0)
    in_specs = [pl.BlockSpec((TM, D_MODEL), row)]
    args = [x]
    if has_peer:
        in_specs += [pl.BlockSpec((TM, D_MODEL), row), pl.BlockSpec((1, 1, D_MODEL), modrow)]
        args += [peer, g2]
    in_specs += [
        pl.BlockSpec((1, D_MODEL), const),
        pl.BlockSpec((1, 1, D_MODEL), modrow),
        pl.BlockSpec((1, 1, D_MODEL), modrow),
        pl.BlockSpec((D_MODEL, PROJ_PAD), const),
        pl.BlockSpec((1, MLA_Q_RANK), const),
        pl.BlockSpec((1, MLA_KV_RANK), const),
        pl.BlockSpec((MLA_Q_RANK, MLA_HEADS * LANES), const),
        pl.BlockSpec((MLA_KV_RANK, MLA_HEADS * LANES), const),
        pl.BlockSpec((MLA_KV_RANK, 512), const),
        pl.BlockSpec((TM, 768), lambda i: (_posblk(i), 0)),
    ]
    args += [ng, sc, sh, win, gq, gkv, wq, wk, wv, tab]
    out_specs, out_shape = [], []
    if has_peer:
        out_specs.append(pl.BlockSpec((TM, D_MODEL), row))
        out_shape.append(jax.ShapeDtypeStruct((TT, D_MODEL), F32))
    for width, transposed in ((1024, False), (1024, False), (512, True),
                              (512, False), (512, False), (512, True)):
        if transposed:
            out_specs.append(pl.BlockSpec((width, TM), lambda i: (0, i)))
            out_shape.append(jax.ShapeDtypeStruct((width, TT), BF16))
        else:
            out_specs.append(pl.BlockSpec((TM, width), row))
            out_shape.append(jax.ShapeDtypeStruct((TT, width), BF16))
    res = pl.pallas_call(
        functools.partial(_qkv_body, has_peer),
        grid=(NB_ALL,),
        in_specs=in_specs,
        out_specs=out_specs,
        out_shape=out_shape,
        compiler_params=_cparams(("parallel",)),
        name="qkv",
    )(*args)
    if has_peer:
        return res[0], res[1:]
    return x, res


def _flash_pair(tq, qs, ksls, vsls, kc_ref, kl_ref, vtc_ref, vtl_ref, n_pairs, sa_scr, sb_scr):
    n_lat = SEQ // TK

    def col_reduce(parts, op, final):
        while len(parts) > 1:
            parts = [op(parts[i], parts[i + 1]) for i in range(0, len(parts), 2)]
        return final(parts[0], axis=0, keepdims=True)

    def row_groups(x):
        return [x[i:i + SUBLANES] for i in range(0, x.shape[0], SUBLANES)]

    def scores(k_of):
        return [lax.dot_general(k_of(ksl), q, (((1,), (1,)), ((), ())), preferred_element_type=F32)
                for q, ksl in zip(qs, ksls)]

    def softmax_pv(read_s, vt_of, carries):
        mid = []
        for h, (m, l, acc) in enumerate(carries):
            m_new = jnp.maximum(m, col_reduce(row_groups(read_s(h)), jnp.maximum, jnp.max))
            alpha = jnp.exp2(m - m_new)
            p = jnp.exp2(read_s(h) - m_new)
            l = alpha * l + col_reduce(row_groups(p), jnp.add, jnp.sum)
            mid.append((m_new, l, alpha * acc, p.astype(BF16)))
        return tuple((m_new, l, acc + jnp.dot(vt_of(vsl), p, preferred_element_type=F32))
                     for (m_new, l, acc, p), vsl in zip(mid, vsls))

    def k_lat(c):
        r0 = pl.multiple_of(c * TK, TK)
        return lambda ksl: kl_ref[pl.ds(r0, TK), ksl]

    def vt_lat(c):
        r0 = pl.multiple_of(c * TK, TK)
        return lambda vsl: vtl_ref[vsl, pl.ds(r0, TK)]

    def stage(scr, sc):
        for h, x in enumerate(sc):
            scr[h] = x

    init = tuple((jnp.full((1, tq), -jnp.inf, F32), jnp.zeros((1, tq), F32),
                  jnp.zeros((vsl.stop - vsl.start, tq), F32)) for vsl in vsls)
    ctx_scores = scores(lambda ksl: kc_ref[:, ksl])
    if n_pairs:
        stage(sa_scr, scores(k_lat(0)))
    carries = softmax_pv(lambda h: ctx_scores[h], lambda vsl: vtc_ref[vsl, :], init)

    def pair(j, carries):
        c = 2 * j
        stage(sb_scr, scores(k_lat(c + 1)))
        carries = softmax_pv(lambda h: sa_scr[h], vt_lat(c), carries)
        stage(sa_scr, scores(k_lat(jnp.minimum(c + 2, n_lat - 1))))
        return softmax_pv(lambda h: sb_scr[h], vt_lat(c + 1), carries)

    if n_pairs:
        carries = lax.fori_loop(0, n_pairs, pair, carries)
    return [acc / l for (m, l, acc) in carries]


def _attn_mla_body(tq, n_pairs, *refs):
    if n_pairs:
        q_ref, kc_ref, kl_ref, vtc_ref, vtl_ref, o_ref, sa_scr, sb_scr = refs
    else:
        q_ref, kc_ref, vtc_ref, o_ref = refs
        kl_ref = vtl_ref = sa_scr = sb_scr = None
    qs = [q_ref[:, 0:LANES], q_ref[:, LANES:2 * LANES]]
    ksls = [slice(0, LANES), slice(LANES, 2 * LANES)]
    vsls = [slice(0, MLA_HEAD_DIM), slice(MLA_HEAD_DIM, 2 * MLA_HEAD_DIM)]
    ot0, ot1 = _flash_pair(tq, qs, ksls, vsls, kc_ref, kl_ref, vtc_ref, vtl_ref, n_pairs, sa_scr, sb_scr)
    o_ref[...] = jnp.concatenate([ot0, ot1], axis=0).T.astype(o_ref.dtype)


def _attn_diff_body(tq, n_pairs, lam_scale, *refs):
    if n_pairs:
        q_ref, kc_ref, kl_ref, vtc_ref, vtl_ref, lam_ref, g_ref, o_ref, sa_scr, sb_scr = refs
    else:
        q_ref, kc_ref, vtc_ref, lam_ref, g_ref, o_ref = refs
        kl_ref = vtl_ref = sa_scr = sb_scr = None
    lane = lax.broadcasted_iota(jnp.int32, (tq, LANES), 1)
    q = q_ref[...]
    zero = jnp.zeros_like(q)
    qs = [jnp.where(lane < DIFF_HEAD_DIM, q, zero), jnp.where(lane < DIFF_HEAD_DIM, zero, q)]
    ksls = [slice(0, LANES), slice(0, LANES)]
    vsls = [slice(0, LANES), slice(0, LANES)]
    ot0, ot1 = _flash_pair(tq, qs, ksls, vsls, kc_ref, kl_ref, vtc_ref, vtl_ref, n_pairs, sa_scr, sb_scr)
    o = ot0.T - lam_ref[...] * ot1.T
    o_ref[...] = (_rms(o) * g_ref[...] * lam_scale).astype(o_ref.dtype)


def _attention(kind, q, k, vt, ctx_queries, lam=None, g=None, lam_scale=None):
    qw = 2 * LANES if kind == "mla" else LANES
    ctx0 = T_LAT // CTX_LEN
    if ctx_queries:
        tq, n_pairs, steps, rows = CTX_LEN, 0, 1, T_CTX
        q_map = lambda b, p, i: (ctx0 + b, p)
        o_map = lambda b, p, i: (b, p)
    else:
        tq, n_pairs, steps, rows = TQ, SEQ // (2 * TK), SEQ // TQ, T_LAT
        q_map = lambda b, p, i: (b * steps + i, p)
        o_map = q_map
    kc_spec = pl.BlockSpec((CTX_LEN, qw), lambda b, p, i: (ctx0 + b, p))
    vtc_spec = pl.BlockSpec((LANES, CTX_LEN), lambda b, p, i: (p, ctx0 + b))
    in_specs = [pl.BlockSpec((tq, qw), q_map), kc_spec]
    args = [q, k]
    if n_pairs:
        in_specs += [pl.BlockSpec((SEQ, qw), lambda b, p, i: (b, p)), vtc_spec,
                     pl.BlockSpec((LANES, SEQ), lambda b, p, i: (p, b))]
        args += [k, vt, vt]
    else:
        in_specs += [vtc_spec]
        args += [vt]
    if kind == "mla":
        body = functools.partial(_attn_mla_body, tq, n_pairs)
    else:
        body = functools.partial(_attn_diff_body, tq, n_pairs, lam_scale)
        in_specs += [pl.BlockSpec((1, LANES), lambda b, p, i: (0, 0)),
                     pl.BlockSpec((1, LANES), lambda b, p, i: (0, 0))]
        args += [lam, g]
    scratch = [pltpu.VMEM((2, TK, tq), F32), pltpu.VMEM((2, TK, tq), F32)] if n_pairs else []
    return pl.pallas_call(
        body,
        grid=(BATCH, 4, steps),
        in_specs=in_specs,
        out_specs=pl.BlockSpec((tq, LANES), o_map),
        out_shape=jax.ShapeDtypeStruct((rows, 512), BF16),
        scratch_shapes=scratch,
        compiler_params=_cparams(("parallel", "parallel", "arbitrary")),
        name="attn_" + kind + ("_ctx" if ctx_queries else ""),
    )(*args)


def _top16(s, payload=None):
    n_rows = s.shape[0]
    rowf = lax.broadcasted_iota(jnp.int32, s.shape, 0).astype(F32)
    slot = lax.broadcasted_iota(jnp.int32, (PEER_TOPK, s.shape[1]), 0)
    vals = jnp.zeros((PEER_TOPK, s.shape[1]), F32)
    picks = jnp.zeros((PEER_TOPK, s.shape[1]), F32)
    for r in range(PEER_TOPK):
        m = jnp.max(s, axis=0, keepdims=True)
        am = jnp.min(jnp.where(s == m, rowf, float(n_rows)), axis=0, keepdims=True)
        hit = rowf == am
        pick = am if payload is None else jnp.max(jnp.where(hit, payload, -1.0), axis=0, keepdims=True)
        vals = jnp.where(slot == r, m, vals)
        picks = jnp.where(slot == r, pick, picks)
        s = jnp.where(hit, -jnp.inf, s)
    return vals, picks


def _staircase(a16, b16, combine, pad):
    tm = a16.shape[1]
    sub = lax.broadcasted_iota(jnp.int32, (SUBLANES, tm), 0)
    a_lo, a_hi = a16[0:SUBLANES], a16[SUBLANES:]
    b_lo, b_hi = b16[0:SUBLANES], b16[SUBLANES:]
    row = lambda x, r: jnp.broadcast_to(x[r:r + 1], (SUBLANES, tm))
    take = lambda x, idx: jnp.take_along_axis(x, idx, axis=0)
    a3 = jnp.where(sub < 5, 2, 3)
    b3 = jnp.where(sub < 5, sub, sub - 5)
    a4 = jnp.where(sub < 1, 3, jnp.where(sub < 4, 4, jnp.where(sub < 6, 5, 6)))
    b4 = jnp.where(sub < 1, 3, jnp.where(sub < 4, sub - 1, jnp.where(sub < 6, sub - 4, sub - 6)))
    pieces = [
        combine(row(a_lo, 0), b_lo),
        combine(row(a_lo, 0), b_hi),
        combine(row(a_lo, 1), b_lo),
        combine(take(a_lo, a3), take(b_lo, b3)),
        combine(take(a_lo, a4), take(b_lo, b4)),
        jnp.where(sub < 2, combine(row(a_lo, 7), b_lo), pad),
        combine(a_hi, row(b_lo, 0)),
    ]
    return jnp.concatenate(pieces, axis=0)


def _mid_body(x_ref, om_ref, od_ref, wo_m_ref, wo_d_ref, g1_ref, ng_ref, sc_ref, sh_ref,
              wq_ref, sk_ref, xo_ref, h_ref, eidx_ref, gate_ref, q_scr):
    y = (jnp.dot(om_ref[...], wo_m_ref[...], preferred_element_type=F32)
         + jnp.dot(od_ref[...], wo_d_ref[...], preferred_element_type=F32))
    x = x_ref[...] + g1_ref[0] * y
    xo_ref[...] = x
    h = _rms(x) * ng_ref[...] * (1.0 + sc_ref[0]) + sh_ref[0]
    h_ref[...] = h
    q_scr[...] = jnp.dot(h.astype(BF16), wq_ref[...], preferred_element_type=F32).astype(BF16)

    def head(hh, _):
        sv, si = [], []
        for c in range(2):
            c0 = pl.multiple_of(hh * 2 * N_KEYS + c * N_KEYS, N_KEYS)
            qh = q_scr[:, pl.ds(c0, N_KEYS)]
            s = lax.dot_general(sk_ref[c], qh, (((1,), (1,)), ((), ())), preferred_element_type=F32)
            vals, keys = _top16(s)
            sv.append(vals)
            si.append(keys)
        cand_s = _staircase(sv[0], sv[1], lambda a, b: a + b, -jnp.inf)
        cand_e = _staircase(si[0], si[1], lambda a, b: a * float(N_KEYS) + b, -1.0)
        top_s, top_e = _top16(cand_s, cand_e)
        ex = jnp.exp(top_s - jnp.max(top_s, axis=0, keepdims=True))
        gate = ex / jnp.sum(ex, axis=0, keepdims=True)
        r0 = pl.multiple_of(hh * PEER_TOPK, PEER_TOPK)
        eidx_ref[pl.ds(r0, PEER_TOPK), :] = top_e.astype(jnp.int32) * ROWS_PER_EXPERT
        gate_ref[pl.ds(r0, PEER_TOPK), :] = gate
        return 0

    lax.fori_loop(0, PEER_HEADS, head, 0)


def _mid(n_blocks, x, om, od, wo_m, wo_d, g1, ng, sc, sh, wq, sk):
    n = n_blocks * TM
    row = lambda i: (i, 0)
    modrow = lambda i: (_bid(i), 0, 0)
    const = lambda i: (0, 0)
    return pl.pallas_call(
        _mid_body,
        grid=(n_blocks,),
        in_specs=[
            pl.BlockSpec((TM, D_MODEL), row),
            pl.BlockSpec((TM, 512), row),
            pl.BlockSpec((TM, 512), row),
            pl.BlockSpec((512, D_MODEL), const),
            pl.BlockSpec((512, D_MODEL), const),
            pl.BlockSpec((1, 1, D_MODEL), modrow),
            pl.BlockSpec((1, D_MODEL), const),
            pl.BlockSpec((1, 1, D_MODEL), modrow),
            pl.BlockSpec((1, 1, D_MODEL), modrow),
            pl.BlockSpec((D_MODEL, PEER_HEADS * 2 * N_KEYS), const),
            pl.BlockSpec((2, N_KEYS, N_KEYS), lambda i: (0, 0, 0)),
        ],
        out_specs=[
            pl.BlockSpec((TM, D_MODEL), row),
            pl.BlockSpec((TM, D_MODEL), row),
            pl.BlockSpec((PEER_HEADS * PEER_TOPK, TM), lambda i: (0, i)),
            pl.BlockSpec((PEER_HEADS * PEER_TOPK, TM), lambda i: (0, i)),
        ],
        out_shape=[
            jax.ShapeDtypeStruct((n, D_MODEL), F32),
            jax.ShapeDtypeStruct((n, D_MODEL), F32),
            jax.ShapeDtypeStruct((PEER_HEADS * PEER_TOPK, n), jnp.int32),
            jax.ShapeDtypeStruct((PEER_HEADS * PEER_TOPK, n), F32),
        ],
        scratch_shapes=[pltpu.VMEM((TM, PEER_HEADS * 2 * N_KEYS), BF16)],
        compiler_params=_cparams(("parallel",)),
        name="mid",
    )(x, om, od, wo_m, wo_d, g1, ng, sc, sh, wq, sk)


N_SEL = PEER_HEADS * PEER_TOPK
HALF = D_MODEL // 2
ROWS_PER_EXPERT = HALF // LANES
_BFLY_ORDER = (0, 4, 2, 6, 1, 5, 3, 7)


def _pack_table(tab):
    u = lax.bitcast_convert_type(tab, jnp.uint32)
    r = u + jnp.uint32(0x7FFF) + ((u >> 16) & jnp.uint32(1))
    w = (r[:, :HALF] >> 16) | (r[:, HALF:] & jnp.uint32(0xFFFF0000))
    return w.reshape(tab.shape[0] * ROWS_PER_EXPERT, LANES)


def _load_pair(tab_ref, ra, rb):
    w = jnp.concatenate([tab_ref[pl.ds(pl.multiple_of(ra, ROWS_PER_EXPERT), ROWS_PER_EXPERT), :],
                         tab_ref[pl.ds(pl.multiple_of(rb, ROWS_PER_EXPERT), ROWS_PER_EXPERT), :]], axis=0)
    lo = pltpu.bitcast(w << 16, F32)
    hi = pltpu.bitcast(w & jnp.uint32(0xFFFF0000), F32)
    return lo, hi


def _peer_u_body(idx_ref, x_ref, gate_ref, tab_ref, act_ref, ra_scr, rb_scr):
    sub = lax.broadcasted_iota(jnp.int32, (SUBLANES, LANES), 0)
    lane = lax.broadcasted_iota(jnp.int32, (SUBLANES, LANES), 1)
    top = sub < 4
    n_grp = N_SEL // SUBLANES
    n_tiles = PEER_TB // SUBLANES

    def fold(p, q, width):
        tp = p + pltpu.roll(p, SUBLANES - width, 0)
        tq = q + pltpu.roll(q, SUBLANES - width, 0)
        return jnp.where((sub % (2 * width)) < width, tp, pltpu.roll(tq, width, 0))

    def gather_tile(tb, cur_scr, prev_scr):
        def token(tt, a):
            t = tb * SUBLANES + tt
            for i in range(n_grp):
                j = tt * n_grp + i
                a = jnp.where(lane == j, jnp.sum(prev_scr[j], axis=1, keepdims=True), a)
            xt = x_ref[t]
            xsw = pltpu.roll(xt, 4, 0)
            xlo = jnp.where(top, xt, xsw)
            xhi = jnp.where(top, xsw, xt)
            for g in range(n_grp):
                parts = []
                for pr in range(4):
                    ea = idx_ref[t, g * SUBLANES + _BFLY_ORDER[2 * pr]]
                    eb = idx_ref[t, g * SUBLANES + _BFLY_ORDER[2 * pr + 1]]
                    lo, hi = _load_pair(tab_ref, ea, eb)
                    parts.append(lo * xlo + hi * xhi)
                cur_scr[tt * n_grp + g] = fold(fold(parts[0], parts[1], 2), fold(parts[2], parts[3], 2), 1)
            return a

        return lax.fori_loop(0, SUBLANES, token, jnp.zeros((SUBLANES, LANES), F32))

    def emit(tb, scores):
        act_ref[tb] = 0.5 * scores * (1.0 + lax.erf(scores * (2.0 ** -0.5))) * gate_ref[tb]

    rb_scr[...] = jnp.zeros(rb_scr.shape, F32)

    def tile_pair(jp, _):
        prev_scores = gather_tile(2 * jp, ra_scr, rb_scr)

        @pl.when(jp > 0)
        def _():
            emit(2 * jp - 1, prev_scores)

        emit(2 * jp, gather_tile(2 * jp + 1, rb_scr, ra_scr))
        return 0

    lax.fori_loop(0, n_tiles // 2, tile_pair, 0)
    last = jnp.zeros((SUBLANES, LANES), F32)
    for j in range(SUBLANES * n_grp):
        last = jnp.where(lane == j, jnp.sum(rb_scr[j], axis=1, keepdims=True), last)
    emit(n_tiles - 1, last)


def _peer_v_body(idx_ref, act_ref, tab_ref, o_ref):
    sub = lax.broadcasted_iota(jnp.int32, (SUBLANES, LANES), 0)
    lane = lax.broadcasted_iota(jnp.int32, (SUBLANES, LANES), 1)
    top = sub < 4
    n_grp = N_SEL // SUBLANES
    pair_rows = [jnp.where(top, 2 * p, 2 * p + 1) for p in range(4)]

    def tile(tb, _):
        a_tile = act_ref[tb]

        for tt in range(SUBLANES):
            t = tb * SUBLANES + tt
            acc = [jnp.zeros((SUBLANES, LANES), F32) for _ in range(4)]
            for g in range(n_grp):
                col = jnp.sum(jnp.where(lane == tt * n_grp + g, a_tile, 0.0), axis=1, keepdims=True)
                actg = jnp.broadcast_to(col, (SUBLANES, LANES))
                for p in range(4):
                    ea = idx_ref[t, g * SUBLANES + 2 * p]
                    eb = idx_ref[t, g * SUBLANES + 2 * p + 1]
                    lo, hi = _load_pair(tab_ref, ea, eb)
                    av = jnp.take_along_axis(actg, pair_rows[p], axis=0)
                    k = 2 * (p % 2)
                    acc[k] = acc[k] + av * lo
                    acc[k + 1] = acc[k + 1] + av * hi
            lo = acc[0] + acc[2]
            hi = acc[1] + acc[3]
            lo = lo + pltpu.roll(lo, 4, 0)
            hi = hi + pltpu.roll(hi, 4, 0)
            o_ref[t] = jnp.where(top, lo, hi)
        return 0

    lax.fori_loop(0, PEER_TB // SUBLANES, tile, 0)


def _table_spec():
    return pl.BlockSpec((N_EXPERTS * ROWS_PER_EXPERT, LANES), lambda i: (0, 0),
                        pipeline_mode=pl.Buffered(1))


def _peer_u(n, idx, x3, gate3, tab):
    return pl.pallas_call(
        _peer_u_body,
        grid=(n // PEER_TB,),
        in_specs=[
            pl.BlockSpec((PEER_TB, N_SEL), lambda i: (i, 0), memory_space=pltpu.SMEM),
            pl.BlockSpec((PEER_TB, SUBLANES, LANES), lambda i: (i, 0, 0)),
            pl.BlockSpec((PEER_TB // SUBLANES, SUBLANES, LANES), lambda i: (i, 0, 0)),
            _table_spec(),
        ],
        out_specs=pl.BlockSpec((PEER_TB // SUBLANES, SUBLANES, LANES), lambda i: (i, 0, 0)),
        out_shape=jax.ShapeDtypeStruct((n // SUBLANES, SUBLANES, LANES), F32),
        scratch_shapes=[pltpu.VMEM((N_SEL, SUBLANES, LANES), F32),
                        pltpu.VMEM((N_SEL, SUBLANES, LANES), F32)],
        compiler_params=_cparams(("arbitrary",)),
        name="peer_u",
    )(idx, x3, gate3, tab)


def _peer_v(n, idx, act3, tab):
    return pl.pallas_call(
        _peer_v_body,
        grid=(n // PEER_TB,),
        in_specs=[
            pl.BlockSpec((PEER_TB, N_SEL), lambda i: (i, 0), memory_space=pltpu.SMEM),
            pl.BlockSpec((PEER_TB // SUBLANES, SUBLANES, LANES), lambda i: (i, 0, 0)),
            _table_spec(),
        ],
        out_specs=pl.BlockSpec((PEER_TB, SUBLANES, LANES), lambda i: (i, 0, 0)),
        out_shape=jax.ShapeDtypeStruct((n, SUBLANES, LANES), F32),
        compiler_params=_cparams(("arbitrary",)),
        name="peer_v",
    )(idx, act3, tab)


def _final_body(x_ref, peer_ref, g2_ref, g_ref, o_ref):
    x = x_ref[...] + g2_ref[0] * peer_ref[...]
    o_ref[...] = _rms(x) * g_ref[...]


def _final(x, peer, g2, g):
    row = lambda i: (i, 0)
    return pl.pallas_call(
        _final_body,
        grid=(NB_LAT,),
        in_specs=[
            pl.BlockSpec((TM, D_MODEL), row),
            pl.BlockSpec((TM, D_MODEL), row),
            pl.BlockSpec((1, 1, D_MODEL), lambda i: (_bid(i), 0, 0)),
            pl.BlockSpec((1, D_MODEL), lambda i: (0, 0)),
        ],
        out_specs=pl.BlockSpec((TM, D_MODEL), row),
        out_shape=jax.ShapeDtypeStruct((T_LAT, D_MODEL), F32),
        compiler_params=_cparams(("parallel",)),
        name="final_norm",
    )(x, peer, g2, g)


def _deinterleave(width):
    return np.concatenate([np.arange(0, width, 2), np.arange(1, width, 2)])


def _rope_tables():
    pos = np.arange(SEQ)
    rowp = jnp.asarray(pos // GRID_W, F32)
    colp = jnp.asarray(pos % GRID_W, F32)

    def angles(dim):
        quarter = dim // 4
        inv = ROPE_BASE ** (-jnp.arange(quarter, dtype=F32) / quarter)
        return jnp.concatenate([rowp[:, None] * inv, colp[:, None] * inv], axis=-1)

    am = angles(MLA_ROPE_DIM)
    ad = angles(DIFF_HEAD_DIM)
    cm, sm_ = jnp.cos(am), jnp.sin(am)
    cd, sd = jnp.cos(ad), jnp.sin(ad)
    one = lambda w: jnp.ones((SEQ, w), F32)
    zero = lambda w: jnp.zeros((SEQ, w), F32)
    t_cm = jnp.concatenate([one(64), cm, cm, one(32)], axis=1)
    t_spm = jnp.concatenate([zero(80), sm_, zero(32)], axis=1)
    t_smm = jnp.concatenate([zero(64), -sm_, zero(48)], axis=1)
    t_cd = jnp.concatenate([cd, cd, cd, cd], axis=1)
    t_spd = jnp.concatenate([zero(32), sd, zero(32), sd], axis=1)
    t_smd = jnp.concatenate([-sd, zero(32), -sd, zero(32)], axis=1)
    lat = jnp.concatenate([t_cm, t_spm, t_smm, t_cd, t_spd, t_smd], axis=1)
    ident = jnp.concatenate([jnp.ones((CTX_LEN, 128), F32), jnp.zeros((CTX_LEN, 256), F32),
                             jnp.ones((CTX_LEN, 128), F32), jnp.zeros((CTX_LEN, 256), F32)], axis=1)
    return jnp.concatenate([lat, ident], axis=0)


def _prep_layer_weights(w_in, wq_up, wq_rope, wk_up, wv_up, w_out, peer_wq, peer_subkeys):
    p32 = _deinterleave(MLA_ROPE_DIM)
    p64 = _deinterleave(DIFF_HEAD_DIM)
    z = lambda w: jnp.zeros((D_MODEL, w), F32)
    kr = w_in[:, 640:672][:, p32]
    qd = w_in[:, 672:1184].reshape(D_MODEL, 8, 64)[:, :, p64].reshape(D_MODEL, 512)
    kd = w_in[:, 1184:1696].reshape(D_MODEL, 8, 64)[:, :, p64].reshape(D_MODEL, 512)
    win = jnp.concatenate([w_in[:, 0:640], z(64), kr, z(32), qd, kd, w_in[:, 1696:2208]], axis=1)
    qn = wq_up.reshape(MLA_Q_RANK, MLA_HEADS, MLA_HEAD_DIM)
    qr = wq_rope.reshape(MLA_Q_RANK, MLA_HEADS, MLA_ROPE_DIM)[:, :, p32]
    wq = jnp.concatenate([qn, qr, jnp.zeros((MLA_Q_RANK, MLA_HEADS, 32), F32)], axis=2)
    kn = wk_up.reshape(MLA_KV_RANK, MLA_HEADS, MLA_HEAD_DIM)
    wk = jnp.concatenate([kn, jnp.zeros((MLA_KV_RANK, MLA_HEADS, 64), F32)], axis=2)
    return dict(
        win=win.astype(BF16),
        wq=wq.reshape(MLA_Q_RANK, MLA_HEADS * LANES).astype(BF16),
        wk=wk.reshape(MLA_KV_RANK, MLA_HEADS * LANES).astype(BF16),
        wv=wv_up.astype(BF16),
        wo_m=w_out[:512].astype(BF16),
        wo_d=w_out[512:].astype(BF16),
        pwq=peer_wq.astype(BF16),
        sk=peer_subkeys.astype(BF16),
    )


def kernel(x, c, ctx, c_ctx, norm_attn_g, norm_ffn_g, w_ada, b_ada, w_in, mla_q_norm_g, mla_wq_up, mla_wq_rope, mla_kv_norm_g, mla_wk_up, mla_wv_up, diff_lambda, diff_subnorm_g, w_out, peer_wq, peer_subkeys, peer_u, peer_v, final_norm_g):
    xs = jnp.concatenate([x.reshape(T_LAT, D_MODEL), ctx.reshape(T_CTX, D_MODEL)], axis=0)
    cc = jnp.concatenate([c, c_ctx[None, :], jnp.zeros((MOD_ROWS - BATCH - 1, D_MODEL), F32)], axis=0)
    mod = _modulation(cc, w_ada, b_ada)
    lam_inits = [0.8 - 0.6 * math.exp(-0.3 * l) for l in range(DEPTH)]
    lam_all = _diff_lambda(diff_lambda, jnp.broadcast_to(jnp.asarray(lam_inits, F32)[:, None], (DEPTH, LANES)))
    tab = _rope_tables()

    peer_out = None
    g2_prev = None
    for l in range(DEPTH):
        last = l == DEPTH - 1
        w = _prep_layer_weights(w_in[l], mla_wq_up[l], mla_wq_rope[l], mla_wk_up[l], mla_wv_up[l],
                                w_out[l], peer_wq[l], peer_subkeys[l])
        m = mod[l].reshape(MOD_ROWS, N_MOD, 1, D_MODEL)
        sh1, sc1, g1, sh2, sc2, g2 = (m[:, j] for j in range(N_MOD))
        xs, (qm, km, vm, qd, kd, vd) = _qkv(
            xs, peer_out, g2_prev, norm_attn_g[l][None, :], sc1, sh1, w["win"],
            mla_q_norm_g[l][None, :], mla_kv_norm_g[l][None, :], w["wq"], w["wk"], w["wv"], tab)
        diff_args = dict(lam=lam_all[l][None, :], g=diff_subnorm_g[l][None, :],
                         lam_scale=1.0 - lam_inits[l])
        om = _attention("mla", qm, km, vm, False)
        od = _attention("diff", qd, kd, vd, False, **diff_args)
        if not last:
            om = jnp.concatenate([om, _attention("mla", qm, km, vm, True)], axis=0)
            od = jnp.concatenate([od, _attention("diff", qd, kd, vd, True, **diff_args)], axis=0)
        n_blocks = NB_LAT if last else NB_ALL
        n = n_blocks * TM
        xs, h2, eidx_t, gate_t = _mid(n_blocks, xs, om, od, w["wo_m"], w["wo_d"], g1,
                                      norm_ffn_g[l][None, :], sc2, sh2, w["pwq"], w["sk"])
        idx = eidx_t.T
        gate3 = gate_t.reshape(16, 8, n // 8, 8).transpose(2, 1, 3, 0).reshape(n // 8, 8, LANES)
        act3 = _peer_u(n, idx, h2.reshape(n, SUBLANES, LANES), gate3, _pack_table(peer_u[l]))
        peer_out = _peer_v(n, idx, act3, _pack_table(peer_v[l])).reshape(n, D_MODEL)
        g2_prev = g2
    out = _final(xs, peer_out, g2_prev, final_norm_g[None, :])
    return out.reshape(BATCH, SEQ, D_MODEL)
```

```python
import functools
import math

import jax
import jax.numpy as jnp
import numpy as np
from jax import lax
from jax.experimental import pallas as pl
from jax.experimental.pallas import tpu as pltpu

F32 = jnp.float32
BF16 = jnp.bfloat16

D_MODEL = 1024
BATCH = 8
SEQ = 4096
DEPTH = 4
GRID_W = 64
CTX_LEN = 256
N_MOD = 6
NORM_EPS = 1e-6
ROPE_BASE = 10000.0
MLA_HEADS = 8
MLA_HEAD_DIM = 64
MLA_ROPE_DIM = 32
MLA_Q_RANK = 384
MLA_KV_RANK = 256
DIFF_HEADS = 4
DIFF_HEAD_DIM = 64
PEER_HEADS = 8
PEER_TOPK = 16
N_KEYS = 128
N_EXPERTS = N_KEYS * N_KEYS

T_LAT = BATCH * SEQ
T_CTX = BATCH * CTX_LEN
TT = T_LAT + T_CTX
TM = 256
NB_LAT = T_LAT // TM
NB_ALL = TT // TM
BLK_PER_BATCH = SEQ // TM
MOD_ROWS = 16
PROJ_PAD = 2304
TQ = 256
TK = 512
PEER_SB = 16
LANES = 128
SUBLANES = 8
VMEM_LIMIT = 56 * 1024 * 1024
LOG2_E = math.log2(math.e)


def _cparams(sem, vmem=VMEM_LIMIT):
    return pltpu.CompilerParams(dimension_semantics=sem, vmem_limit_bytes=vmem)


def _rms(x):
    return x * lax.rsqrt(jnp.mean(x * x, axis=-1, keepdims=True) + NORM_EPS)


def _bid(i):
    return jnp.where(i < NB_LAT, i // BLK_PER_BATCH, BATCH)


def _posblk(i):
    return jnp.where(i < NB_LAT, i % BLK_PER_BATCH, BLK_PER_BATCH)


def _mod_body(cc_ref, w_ref, b_ref, o_ref):
    cc = cc_ref[...]
    s = cc / (1.0 + jnp.exp(-cc))
    o_ref[0] = jnp.dot(s, w_ref[0], precision=lax.Precision.HIGHEST,
                       preferred_element_type=F32) + b_ref[0]


def _modulation(cc, w_ada, b_ada):
    nj = N_MOD
    return pl.pallas_call(
        _mod_body,
        grid=(DEPTH, nj),
        in_specs=[
            pl.BlockSpec((MOD_ROWS, D_MODEL), lambda l, j: (0, 0)),
            pl.BlockSpec((1, D_MODEL, D_MODEL), lambda l, j: (l, 0, j)),
            pl.BlockSpec((1, 1, D_MODEL), lambda l, j: (l, 0, j)),
        ],
        out_specs=pl.BlockSpec((1, MOD_ROWS, D_MODEL), lambda l, j: (l, 0, j)),
        out_shape=jax.ShapeDtypeStruct((DEPTH, MOD_ROWS, N_MOD * D_MODEL), F32),
        compiler_params=_cparams(("arbitrary", "arbitrary")),
        name="modulation",
    )(cc, w_ada, b_ada.reshape(DEPTH, 1, N_MOD * D_MODEL))


def _lam_body(d0_ref, d1_ref, d2_ref, d3_ref, li_ref, o_ref):
    a = jnp.sum(d0_ref[...] * d1_ref[...], axis=-1, keepdims=True)
    b = jnp.sum(d2_ref[...] * d3_ref[...], axis=-1, keepdims=True)
    o_ref[...] = jnp.exp(a) - jnp.exp(b) + li_ref[...]


def _diff_lambda(diff_lambda, lam_init):
    dl = diff_lambda.astype(F32)
    return pl.pallas_call(
        _lam_body,
        out_shape=jax.ShapeDtypeStruct((DEPTH, LANES), F32),
        name="diff_lambda",
    )(dl[:, 0], dl[:, 1], dl[:, 2], dl[:, 3], lam_init)


def _rope(xb, c, sp, sm, shift):
    return xb * c + pltpu.roll(xb, shift, 1) * sp + pltpu.roll(xb, LANES - shift, 1) * sm


def _qkv_body(has_peer, *refs):
    if has_peer:
        x_ref, peer_ref, g2_ref = refs[:3]
        refs = refs[3:]
    else:
        x_ref = refs[0]
        refs = refs[1:]
    (ng_ref, sc_ref, sh_ref, win_ref, gq_ref, gkv_ref, wq_ref, wk_ref, wv_ref, tab_ref) = refs[:10]
    outs = refs[10:]
    if has_peer:
        xo_ref, outs = outs[0], outs[1:]
    qm_ref, km_ref, vm_ref, qd_ref, kd_ref, vd_ref = outs

    x = x_ref[...]
    if has_peer:
        x = x + g2_ref[0] * peer_ref[...]
        xo_ref[...] = x
    h = _rms(x) * ng_ref[...] * (1.0 + sc_ref[0]) + sh_ref[0]
    proj = jnp.dot(h.astype(BF16), win_ref[...], preferred_element_type=F32)
    cq = _rms(proj[:, 0:384]) * gq_ref[...]
    ckv = _rms(proj[:, 384:640]) * gkv_ref[...]
    kr = proj[:, 640:768]
    q = jnp.dot(cq.astype(BF16), wq_ref[...], preferred_element_type=F32)
    ckv16 = ckv.astype(BF16)
    k = jnp.dot(ckv16, wk_ref[...], preferred_element_type=F32)
    v = jnp.dot(ckv16, wv_ref[...], preferred_element_type=F32)
    tab = tab_ref[...]
    cm, spm, smm = tab[:, 0:128], tab[:, 128:256], tab[:, 256:384]
    cd, spd, smd = tab[:, 384:512], tab[:, 512:640], tab[:, 640:768]
    kr_rot = _rope(kr, cm, spm, smm, MLA_ROPE_DIM // 2)
    scale_m = LOG2_E * (MLA_HEAD_DIM + MLA_ROPE_DIM) ** -0.5
    for hh in range(MLA_HEADS):
        sl = slice(hh * LANES, (hh + 1) * LANES)
        qm_ref[:, sl] = (_rope(q[:, sl], cm, spm, smm, MLA_ROPE_DIM // 2) * scale_m).astype(BF16)
        km_ref[:, sl] = (k[:, sl] + kr_rot).astype(BF16)
    vm_ref[...] = v.T.astype(BF16)
    scale_d = LOG2_E * DIFF_HEAD_DIM ** -0.5
    for hh in range(DIFF_HEADS):
        sl = slice(hh * LANES, (hh + 1) * LANES)
        qd = proj[:, 768 + hh * LANES:768 + (hh + 1) * LANES]
        kd = proj[:, 1280 + hh * LANES:1280 + (hh + 1) * LANES]
        qd_ref[:, sl] = (_rope(qd, cd, spd, smd, DIFF_HEAD_DIM // 2) * scale_d).astype(BF16)
        kd_ref[:, sl] = _rope(kd, cd, spd, smd, DIFF_HEAD_DIM // 2).astype(BF16)
    vd_ref[...] = proj[:, 1792:2304].T.astype(BF16)


def _qkv(x, peer, g2, ng, sc, sh, win, gq, gkv, wq, wk, wv, tab):
    has_peer = peer is not None
    row = lambda i: (i, 0)
    modrow = lambda i: (_bid(i), 0, 0)
    const = lambda i: (0, 0)
    in_specs = [pl.BlockSpec((TM, D_MODEL), row)]
    args = [x]
    if has_peer:
        in_specs += [pl.BlockSpec((TM, D_MODEL), row), pl.BlockSpec((1, 1, D_MODEL), modrow)]
        args += [peer, g2]
    in_specs += [
        pl.BlockSpec((1, D_MODEL), const),
        pl.BlockSpec((1, 1, D_MODEL), modrow),
        pl.BlockSpec((1, 1, D_MODEL), modrow),
        pl.BlockSpec((D_MODEL, PROJ_PAD), const),
        pl.BlockSpec((1, MLA_Q_RANK), const),
        pl.BlockSpec((1, MLA_KV_RANK), const),
        pl.BlockSpec((MLA_Q_RANK, MLA_HEADS * LANES), const),
        pl.BlockSpec((MLA_KV_RANK, MLA_HEADS * LANES), const),
        pl.BlockSpec((MLA_KV_RANK, 512), const),
        pl.BlockSpec((TM, 768), lambda i: (_posblk(i), 0)),
    ]
    args += [ng, sc, sh, win, gq, gkv, wq, wk, wv, tab]
    out_specs, out_shape = [], []
    if has_peer:
        out_specs.append(pl.BlockSpec((TM, D_MODEL), row))
        out_shape.append(jax.ShapeDtypeStruct((TT, D_MODEL), F32))
    for width, transposed in ((1024, False), (1024, False), (512, True),
                              (512, False), (512, False), (512, True)):
        if transposed:
            out_specs.append(pl.BlockSpec((width, TM), lambda i: (0, i)))
            out_shape.append(jax.ShapeDtypeStruct((width, TT), BF16))
        else:
            out_specs.append(pl.BlockSpec((TM, width), row))
            out_shape.append(jax.ShapeDtypeStruct((TT, width), BF16))
    res = pl.pallas_call(
        functools.partial(_qkv_body, has_peer),
        grid=(NB_ALL,),
        in_specs=in_specs,
        out_specs=out_specs,
        out_shape=out_shape,
        compiler_params=_cparams(("parallel",)),
        name="qkv",
    )(*args)
    if has_peer:
        return res[0], res[1:]
    return x, res


def _flash_pair(tq, qs, ksls, vsls, kc_ref, kl_ref, vtc_ref, vtl_ref, n_pairs, sa_scr, sb_scr):
    n_lat = SEQ // TK

    def col_reduce(parts, op, final):
        while len(parts) > 1:
            parts = [op(parts[i], parts[i + 1]) for i in range(0, len(parts), 2)]
        return final(parts[0], axis=0, keepdims=True)

    def row_groups(x):
        return [x[i:i + SUBLANES] for i in range(0, x.shape[0], SUBLANES)]

    def scores(k_of):
        return [lax.dot_general(k_of(ksl), q, (((1,), (1,)), ((), ())), preferred_element_type=F32)
                for q, ksl in zip(qs, ksls)]

    def softmax_pv(read_s, vt_of, carries):
        mid = []
        for h, (m, l, acc) in enumerate(carries):
            m_new = jnp.maximum(m, col_reduce(row_groups(read_s(h)), jnp.maximum, jnp.max))
            alpha = jnp.exp2(m - m_new)
            p = jnp.exp2(read_s(h) - m_new)
            l = alpha * l + col_reduce(row_groups(p), jnp.add, jnp.sum)
            mid.append((m_new, l, alpha * acc, p.astype(BF16)))
        return tuple((m_new, l, acc + jnp.dot(vt_of(vsl), p, preferred_element_type=F32))
                     for (m_new, l, acc, p), vsl in zip(mid, vsls))

    def k_lat(c):
        r0 = pl.multiple_of(c * TK, TK)
        return lambda ksl: kl_ref[pl.ds(r0, TK), ksl]

    def vt_lat(c):
        r0 = pl.multiple_of(c * TK, TK)
        return lambda vsl: vtl_ref[vsl, pl.ds(r0, TK)]

    def stage(scr, sc):
        for h, x in enumerate(sc):
            scr[h] = x

    init = tuple((jnp.full((1, tq), -jnp.inf, F32), jnp.zeros((1, tq), F32),
                  jnp.zeros((vsl.stop - vsl.start, tq), F32)) for vsl in vsls)
    ctx_scores = scores(lambda ksl: kc_ref[:, ksl])
    if n_pairs:
        stage(sa_scr, scores(k_lat(0)))
    carries = softmax_pv(lambda h: ctx_scores[h], lambda vsl: vtc_ref[vsl, :], init)

    def pair(j, carries):
        c = 2 * j
        stage(sb_scr, scores(k_lat(c + 1)))
        carries = softmax_pv(lambda h: sa_scr[h], vt_lat(c), carries)
        stage(sa_scr, scores(k_lat(jnp.minimum(c + 2, n_lat - 1))))
        return softmax_pv(lambda h: sb_scr[h], vt_lat(c + 1), carries)

    if n_pairs:
        carries = lax.fori_loop(0, n_pairs, pair, carries)
    return [acc / l for (m, l, acc) in carries]


def _attn_mla_body(tq, n_pairs, *refs):
    if n_pairs:
        q_ref, kc_ref, kl_ref, vtc_ref, vtl_ref, o_ref, sa_scr, sb_scr = refs
    else:
        q_ref, kc_ref, vtc_ref, o_ref = refs
        kl_ref = vtl_ref = sa_scr = sb_scr = None
    qs = [q_ref[:, 0:LANES], q_ref[:, LANES:2 * LANES]]
    ksls = [slice(0, LANES), slice(LANES, 2 * LANES)]
    vsls = [slice(0, MLA_HEAD_DIM), slice(MLA_HEAD_DIM, 2 * MLA_HEAD_DIM)]
    ot0, ot1 = _flash_pair(tq, qs, ksls, vsls, kc_ref, kl_ref, vtc_ref, vtl_ref, n_pairs, sa_scr, sb_scr)
    o_ref[...] = jnp.concatenate([ot0, ot1], axis=0).T.astype(o_ref.dtype)


def _attn_diff_body(tq, n_pairs, lam_scale, *refs):
    if n_pairs:
        q_ref, kc_ref, kl_ref, vtc_ref, vtl_ref, lam_ref, g_ref, o_ref, sa_scr, sb_scr = refs
    else:
        q_ref, kc_ref, vtc_ref, lam_ref, g_ref, o_ref = refs
        kl_ref = vtl_ref = sa_scr = sb_scr = None
    lane = lax.broadcasted_iota(jnp.int32, (tq, LANES), 1)
    q = q_ref[...]
    zero = jnp.zeros_like(q)
    qs = [jnp.where(lane < DIFF_HEAD_DIM, q, zero), jnp.where(lane < DIFF_HEAD_DIM, zero, q)]
    ksls = [slice(0, LANES), slice(0, LANES)]
    vsls = [slice(0, LANES), slice(0, LANES)]
    ot0, ot1 = _flash_pair(tq, qs, ksls, vsls, kc_ref, kl_ref, vtc_ref, vtl_ref, n_pairs, sa_scr, sb_scr)
    o = ot0.T - lam_ref[...] * ot1.T
    o_ref[...] = (_rms(o) * g_ref[...] * lam_scale).astype(o_ref.dtype)


def _attention(kind, q, k, vt, ctx_queries, lam=None, g=None, lam_scale=None):
    qw = 2 * LANES if kind == "mla" else LANES
    ctx0 = T_LAT // CTX_LEN
    if ctx_queries:
        tq, n_pairs, steps, rows = CTX_LEN, 0, 1, T_CTX
        q_map = lambda b, p, i: (ctx0 + b, p)
        o_map = lambda b, p, i: (b, p)
    else:
        tq, n_pairs, steps, rows = TQ, SEQ // (2 * TK), SEQ // TQ, T_LAT
        q_map = lambda b, p, i: (b * steps + i, p)
        o_map = q_map
    kc_spec = pl.BlockSpec((CTX_LEN, qw), lambda b, p, i: (ctx0 + b, p))
    vtc_spec = pl.BlockSpec((LANES, CTX_LEN), lambda b, p, i: (p, ctx0 + b))
    in_specs = [pl.BlockSpec((tq, qw), q_map), kc_spec]
    args = [q, k]
    if n_pairs:
        in_specs += [pl.BlockSpec((SEQ, qw), lambda b, p, i: (b, p)), vtc_spec,
                     pl.BlockSpec((LANES, SEQ), lambda b, p, i: (p, b))]
        args += [k, vt, vt]
    else:
        in_specs += [vtc_spec]
        args += [vt]
    if kind == "mla":
        body = functools.partial(_attn_mla_body, tq, n_pairs)
    else:
        body = functools.partial(_attn_diff_body, tq, n_pairs, lam_scale)
        in_specs += [pl.BlockSpec((1, LANES), lambda b, p, i: (0, 0)),
                     pl.BlockSpec((1, LANES), lambda b, p, i: (0, 0))]
        args += [lam, g]
    scratch = [pltpu.VMEM((2, TK, tq), F32), pltpu.VMEM((2, TK, tq), F32)] if n_pairs else []
    return pl.pallas_call(
        body,
        grid=(BATCH, 4, steps),
        in_specs=in_specs,
        out_specs=pl.BlockSpec((tq, LANES), o_map),
        out_shape=jax.ShapeDtypeStruct((rows, 512), BF16),
        scratch_shapes=scratch,
        compiler_params=_cparams(("parallel", "parallel", "arbitrary")),
        name="attn_" + kind + ("_ctx" if ctx_queries else ""),
    )(*args)


def _top16(s, payload=None):
    n_rows = s.shape[0]
    rowf = lax.broadcasted_iota(jnp.int32, s.shape, 0).astype(F32)
    slot = lax.broadcasted_iota(jnp.int32, (PEER_TOPK, s.shape[1]), 0)
    vals = jnp.zeros((PEER_TOPK, s.shape[1]), F32)
    picks = jnp.zeros((PEER_TOPK, s.shape[1]), F32)
    for r in range(PEER_TOPK):
        m = jnp.max(s, axis=0, keepdims=True)
        am = jnp.min(jnp.where(s == m, rowf, float(n_rows)), axis=0, keepdims=True)
        hit = rowf == am
        pick = am if payload is None else jnp.max(jnp.where(hit, payload, -1.0), axis=0, keepdims=True)
        vals = jnp.where(slot == r, m, vals)
        picks = jnp.where(slot == r, pick, picks)
        s = jnp.where(hit, -jnp.inf, s)
    return vals, picks


def _staircase(a16, b16, combine, pad):
    tm = a16.shape[1]
    sub = lax.broadcasted_iota(jnp.int32, (SUBLANES, tm), 0)
    a_lo, a_hi = a16[0:SUBLANES], a16[SUBLANES:]
    b_lo, b_hi = b16[0:SUBLANES], b16[SUBLANES:]
    row = lambda x, r: jnp.broadcast_to(x[r:r + 1], (SUBLANES, tm))
    take = lambda x, idx: jnp.take_along_axis(x, idx, axis=0)
    a3 = jnp.where(sub < 5, 2, 3)
    b3 = jnp.where(sub < 5, sub, sub - 5)
    a4 = jnp.where(sub < 1, 3, jnp.where(sub < 4, 4, jnp.where(sub < 6, 5, 6)))
    b4 = jnp.where(sub < 1, 3, jnp.where(sub < 4, sub - 1, jnp.where(sub < 6, sub - 4, sub - 6)))
    pieces = [
        combine(row(a_lo, 0), b_lo),
        combine(row(a_lo, 0), b_hi),
        combine(row(a_lo, 1), b_lo),
        combine(take(a_lo, a3), take(b_lo, b3)),
        combine(take(a_lo, a4), take(b_lo, b4)),
        jnp.where(sub < 2, combine(row(a_lo, 7), b_lo), pad),
        combine(a_hi, row(b_lo, 0)),
    ]
    return jnp.concatenate(pieces, axis=0)


def _mid_body(x_ref, om_ref, od_ref, wo_m_ref, wo_d_ref, g1_ref, ng_ref, sc_ref, sh_ref,
              wq_ref, sk_ref, xo_ref, h_ref, eidx_ref, gate_ref, q_scr):
    y = (jnp.dot(om_ref[...], wo_m_ref[...], preferred_element_type=F32)
         + jnp.dot(od_ref[...], wo_d_ref[...], preferred_element_type=F32))
    x = x_ref[...] + g1_ref[0] * y
    xo_ref[...] = x
    h = _rms(x) * ng_ref[...] * (1.0 + sc_ref[0]) + sh_ref[0]
    h_ref[...] = h
    q_scr[...] = jnp.dot(h.astype(BF16), wq_ref[...], preferred_element_type=F32).astype(BF16)

    def head(hh, _):
        sv, si = [], []
        for c in range(2):
            c0 = pl.multiple_of(hh * 2 * N_KEYS + c * N_KEYS, N_KEYS)
            qh = q_scr[:, pl.ds(c0, N_KEYS)]
            s = lax.dot_general(sk_ref[c], qh, (((1,), (1,)), ((), ())), preferred_element_type=F32)
            vals, keys = _top16(s)
            sv.append(vals)
            si.append(keys)
        cand_s = _staircase(sv[0], sv[1], lambda a, b: a + b, -jnp.inf)
        cand_e = _staircase(si[0], si[1], lambda a, b: a * float(N_KEYS) + b, -1.0)
        top_s, top_e = _top16(cand_s, cand_e)
        ex = jnp.exp(top_s - jnp.max(top_s, axis=0, keepdims=True))
        gate = ex / jnp.sum(ex, axis=0, keepdims=True)
        r0 = pl.multiple_of(hh * PEER_TOPK, PEER_TOPK)
        eidx_ref[pl.ds(r0, PEER_TOPK), :] = top_e.astype(jnp.int32) * ROWS_PER_EXPERT
        gate_ref[pl.ds(r0, PEER_TOPK), :] = gate
        return 0

    lax.fori_loop(0, PEER_HEADS, head, 0)


def _mid(n_blocks, x, om, od, wo_m, wo_d, g1, ng, sc, sh, wq, sk):
    n = n_blocks * TM
    row = lambda i: (i, 0)
    modrow = lambda i: (_bid(i), 0, 0)
    const = lambda i: (0, 0)
    return pl.pallas_call(
        _mid_body,
        grid=(n_blocks,),
        in_specs=[
            pl.BlockSpec((TM, D_MODEL), row),
            pl.BlockSpec((TM, 512), row),
            pl.BlockSpec((TM, 512), row),
            pl.BlockSpec((512, D_MODEL), const),
            pl.BlockSpec((512, D_MODEL), const),
            pl.BlockSpec((1, 1, D_MODEL), modrow),
            pl.BlockSpec((1, D_MODEL), const),
            pl.BlockSpec((1, 1, D_MODEL), modrow),
            pl.BlockSpec((1, 1, D_MODEL), modrow),
            pl.BlockSpec((D_MODEL, PEER_HEADS * 2 * N_KEYS), const),
            pl.BlockSpec((2, N_KEYS, N_KEYS), lambda i: (0, 0, 0)),
        ],
        out_specs=[
            pl.BlockSpec((TM, D_MODEL), row),
            pl.BlockSpec((TM, D_MODEL), row),
            pl.BlockSpec((PEER_HEADS * PEER_TOPK, TM), lambda i: (0, i)),
            pl.BlockSpec((PEER_HEADS * PEER_TOPK, TM), lambda i: (0, i)),
        ],
        out_shape=[
            jax.ShapeDtypeStruct((n, D_MODEL), F32),
            jax.ShapeDtypeStruct((n, D_MODEL), F32),
            jax.ShapeDtypeStruct((PEER_HEADS * PEER_TOPK, n), jnp.int32),
            jax.ShapeDtypeStruct((PEER_HEADS * PEER_TOPK, n), F32),
        ],
        scratch_shapes=[pltpu.VMEM((TM, PEER_HEADS * 2 * N_KEYS), BF16)],
        compiler_params=_cparams(("parallel",)),
        name="mid",
    )(x, om, od, wo_m, wo_d, g1, ng, sc, sh, wq, sk)


N_SEL = PEER_HEADS * PEER_TOPK
HALF = D_MODEL // 2
ROWS_PER_EXPERT = HALF // LANES
_BFLY_ORDER = (0, 4, 2, 6, 1, 5, 3, 7)


PACK_BLOCK = 512


def _pack_body(t_ref, o_ref):
    u = pltpu.bitcast(t_ref[...], jnp.uint32)
    r = u + jnp.uint32(0x7FFF) + ((u >> 16) & jnp.uint32(1))
    w = (r[:, :HALF] >> 16) | (r[:, HALF:] & jnp.uint32(0xFFFF0000))
    for s in range(ROWS_PER_EXPERT):
        o_ref[pl.ds(s, PACK_BLOCK, stride=ROWS_PER_EXPERT), :] = w[:, s * LANES:(s + 1) * LANES]


def _pack_table(tab):
    n = tab.shape[0]
    return pl.pallas_call(
        _pack_body,
        grid=(n // PACK_BLOCK,),
        in_specs=[pl.BlockSpec((PACK_BLOCK, D_MODEL), lambda i: (i, 0))],
        out_specs=pl.BlockSpec((PACK_BLOCK * ROWS_PER_EXPERT, LANES), lambda i: (i, 0)),
        out_shape=jax.ShapeDtypeStruct((n * ROWS_PER_EXPERT, LANES), jnp.uint32),
        compiler_params=_cparams(("parallel",)),
        name="pack_table",
    )(tab)


def _load_pair(tab_ref, ra, rb):
    w = jnp.concatenate([tab_ref[pl.ds(pl.multiple_of(ra, ROWS_PER_EXPERT), ROWS_PER_EXPERT), :],
                         tab_ref[pl.ds(pl.multiple_of(rb, ROWS_PER_EXPERT), ROWS_PER_EXPERT), :]], axis=0)
    lo = pltpu.bitcast(w << 16, F32)
    hi = pltpu.bitcast(w & jnp.uint32(0xFFFF0000), F32)
    return lo, hi


def _with_index_buffers(idx_hbm, idx_a, idx_b, sem, first, second):
    step = pl.program_id(0)
    words = PEER_SB * N_SEL

    def idx_copy(block, dst, k):
        return pltpu.make_async_copy(idx_hbm.at[pl.ds(block * words, words)], dst, sem.at[k])

    @pl.when(step == 0)
    def _():
        idx_copy(0, idx_a, 0).start()

    idx_copy(2 * step + 1, idx_b, 1).start()
    idx_copy(2 * step, idx_a, 0).wait()
    first(idx_a)

    @pl.when(step + 1 < pl.num_programs(0))
    def _():
        idx_copy(2 * step + 2, idx_a, 0).start()

    idx_copy(2 * step + 1, idx_b, 1).wait()
    second(idx_b)


def _peer_u_body(idx_hbm, x_ref, gate_ref, tab_ref, act_ref, idx_a, idx_b, r_scr, sem):
    sub = lax.broadcasted_iota(jnp.int32, (SUBLANES, LANES), 0)
    lane = lax.broadcasted_iota(jnp.int32, (SUBLANES, LANES), 1)
    top = sub < 4
    n_grp = N_SEL // SUBLANES
    tiles_per_buf = PEER_SB // SUBLANES

    def fold(p, q, width):
        tp = p + pltpu.roll(p, SUBLANES - width, 0)
        tq = q + pltpu.roll(q, SUBLANES - width, 0)
        return jnp.where((sub % (2 * width)) < width, tp, pltpu.roll(tq, width, 0))

    def gather_tile(idx_s, tile, tile_in_buf):
        for tt in range(SUBLANES):
            tl = tile_in_buf * SUBLANES + tt
            xt = x_ref[tile * SUBLANES + tt]
            xsw = pltpu.roll(xt, 4, 0)
            xlo = jnp.where(top, xt, xsw)
            xhi = jnp.where(top, xsw, xt)
            for g in range(n_grp):
                parts = []
                for pr in range(4):
                    ra = idx_s[tl * N_SEL + g * SUBLANES + _BFLY_ORDER[2 * pr]]
                    rb = idx_s[tl * N_SEL + g * SUBLANES + _BFLY_ORDER[2 * pr + 1]]
                    lo, hi = _load_pair(tab_ref, ra, rb)
                    parts.append(lo * xlo + hi * xhi)
                r_scr[tile * N_SEL + tt * n_grp + g] = fold(
                    fold(parts[0], parts[1], 2), fold(parts[2], parts[3], 2), 1)

    def reduce_tile(tile):
        a = jnp.zeros((SUBLANES, LANES), F32)
        for j in range(N_SEL):
            a = jnp.where(lane == j, jnp.sum(r_scr[tile * N_SEL + j], axis=1, keepdims=True), a)
        act_ref[tile] = 0.5 * a * (1.0 + lax.erf(a * (2.0 ** -0.5))) * gate_ref[tile]

    def first(idx_s):
        for k in range(tiles_per_buf):
            if k:
                reduce_tile(k - 1)
            gather_tile(idx_s, k, k)

    def second(idx_s):
        for k in range(tiles_per_buf):
            reduce_tile(tiles_per_buf + k - 1)
            gather_tile(idx_s, tiles_per_buf + k, k)
        reduce_tile(2 * tiles_per_buf - 1)

    _with_index_buffers(idx_hbm, idx_a, idx_b, sem, first, second)


def _peer_v_body(idx_hbm, act_ref, tab_ref, o_ref, idx_a, idx_b, sem):
    sub = lax.broadcasted_iota(jnp.int32, (SUBLANES, LANES), 0)
    lane = lax.broadcasted_iota(jnp.int32, (SUBLANES, LANES), 1)
    top = sub < 4
    n_grp = N_SEL // SUBLANES
    pair_rows = [jnp.where(top, 2 * p, 2 * p + 1) for p in range(4)]

    def process(idx_s, half):
        for tile in range(PEER_SB // SUBLANES):
            a_tile = act_ref[half * (PEER_SB // SUBLANES) + tile]
            for tt in range(SUBLANES):
                tl = tile * SUBLANES + tt
                acc = [jnp.zeros((SUBLANES, LANES), F32) for _ in range(4)]
                for g in range(n_grp):
                    col = jnp.sum(jnp.where(lane == tt * n_grp + g, a_tile, 0.0), axis=1, keepdims=True)
                    actg = jnp.broadcast_to(col, (SUBLANES, LANES))
                    for p in range(4):
                        ra = idx_s[tl * N_SEL + g * SUBLANES + 2 * p]
                        rb = idx_s[tl * N_SEL + g * SUBLANES + 2 * p + 1]
                        lo, hi = _load_pair(tab_ref, ra, rb)
                        av = jnp.take_along_axis(actg, pair_rows[p], axis=0)
                        k = 2 * (p % 2)
                        acc[k] = acc[k] + av * lo
                        acc[k + 1] = acc[k + 1] + av * hi
                lo = acc[0] + acc[2]
                hi = acc[1] + acc[3]
                lo = lo + pltpu.roll(lo, 4, 0)
                hi = hi + pltpu.roll(hi, 4, 0)
                o_ref[half * PEER_SB + tl] = jnp.where(top, lo, hi)

    _with_index_buffers(idx_hbm, idx_a, idx_b, sem,
                        lambda idx_s: process(idx_s, 0), lambda idx_s: process(idx_s, 1))


def _table_spec():
    return pl.BlockSpec((N_EXPERTS * ROWS_PER_EXPERT, LANES), lambda i: (0, 0),
                        pipeline_mode=pl.Buffered(1))


def _peer_idx_scratch():
    return [pltpu.SMEM((PEER_SB * N_SEL,), jnp.int32), pltpu.SMEM((PEER_SB * N_SEL,), jnp.int32)]


def _peer_u(n, idx, x3, gate3, tab):
    tb = 2 * PEER_SB
    return pl.pallas_call(
        _peer_u_body,
        grid=(n // tb,),
        in_specs=[
            pl.BlockSpec(memory_space=pl.ANY),
            pl.BlockSpec((tb, SUBLANES, LANES), lambda i: (i, 0, 0)),
            pl.BlockSpec((tb // SUBLANES, SUBLANES, LANES), lambda i: (i, 0, 0)),
            _table_spec(),
        ],
        out_specs=pl.BlockSpec((tb // SUBLANES, SUBLANES, LANES), lambda i: (i, 0, 0)),
        out_shape=jax.ShapeDtypeStruct((n // SUBLANES, SUBLANES, LANES), F32),
        scratch_shapes=_peer_idx_scratch() + [
            pltpu.VMEM((tb // SUBLANES * N_SEL, SUBLANES, LANES), F32),
            pltpu.SemaphoreType.DMA((2,))],
        compiler_params=_cparams(("arbitrary",)),
        name="peer_u",
    )(idx.reshape(-1), x3, gate3, tab)


def _peer_v(n, idx, act3, tab):
    tb = 2 * PEER_SB
    return pl.pallas_call(
        _peer_v_body,
        grid=(n // tb,),
        in_specs=[
            pl.BlockSpec(memory_space=pl.ANY),
            pl.BlockSpec((tb // SUBLANES, SUBLANES, LANES), lambda i: (i, 0, 0)),
            _table_spec(),
        ],
        out_specs=pl.BlockSpec((tb, SUBLANES, LANES), lambda i: (i, 0, 0)),
        out_shape=jax.ShapeDtypeStruct((n, SUBLANES, LANES), F32),
        scratch_shapes=_peer_idx_scratch() + [pltpu.SemaphoreType.DMA((2,))],
        compiler_params=_cparams(("arbitrary",)),
        name="peer_v",
    )(idx.reshape(-1), act3, tab)


def _final_body(x_ref, peer_ref, g2_ref, g_ref, o_ref):
    x = x_ref[...] + g2_ref[0] * peer_ref[...]
    o_ref[...] = _rms(x) * g_ref[...]


def _final(x, peer, g2, g):
    row = lambda i: (i, 0)
    return pl.pallas_call(
        _final_body,
        grid=(NB_LAT,),
        in_specs=[
            pl.BlockSpec((TM, D_MODEL), row),
            pl.BlockSpec((TM, D_MODEL), row),
            pl.BlockSpec((1, 1, D_MODEL), lambda i: (_bid(i), 0, 0)),
            pl.BlockSpec((1, D_MODEL), lambda i: (0, 0)),
        ],
        out_specs=pl.BlockSpec((TM, D_MODEL), row),
        out_shape=jax.ShapeDtypeStruct((T_LAT, D_MODEL), F32),
        compiler_params=_cparams(("parallel",)),
        name="final_norm",
    )(x, peer, g2, g)


def _deinterleave(width):
    return np.concatenate([np.arange(0, width, 2), np.arange(1, width, 2)])


def _rope_tables():
    pos = np.arange(SEQ)
    rowp = jnp.asarray(pos // GRID_W, F32)
    colp = jnp.asarray(pos % GRID_W, F32)

    def angles(dim):
        quarter = dim // 4
        inv = ROPE_BASE ** (-jnp.arange(quarter, dtype=F32) / quarter)
        return jnp.concatenate([rowp[:, None] * inv, colp[:, None] * inv], axis=-1)

    am = angles(MLA_ROPE_DIM)
    ad = angles(DIFF_HEAD_DIM)
    cm, sm_ = jnp.cos(am), jnp.sin(am)
    cd, sd = jnp.cos(ad), jnp.sin(ad)
    one = lambda w: jnp.ones((SEQ, w), F32)
    zero = lambda w: jnp.zeros((SEQ, w), F32)
    t_cm = jnp.concatenate([one(64), cm, cm, one(32)], axis=1)
    t_spm = jnp.concatenate([zero(80), sm_, zero(32)], axis=1)
    t_smm = jnp.concatenate([zero(64), -sm_, zero(48)], axis=1)
    t_cd = jnp.concatenate([cd, cd, cd, cd], axis=1)
    t_spd = jnp.concatenate([zero(32), sd, zero(32), sd], axis=1)
    t_smd = jnp.concatenate([-sd, zero(32), -sd, zero(32)], axis=1)
    lat = jnp.concatenate([t_cm, t_spm, t_smm, t_cd, t_spd, t_smd], axis=1)
    ident = jnp.concatenate([jnp.ones((CTX_LEN, 128), F32), jnp.zeros((CTX_LEN, 256), F32),
                             jnp.ones((CTX_LEN, 128), F32), jnp.zeros((CTX_LEN, 256), F32)], axis=1)
    return jnp.concatenate([lat, ident], axis=0)


def _prep_layer_weights(w_in, wq_up, wq_rope, wk_up, wv_up, w_out, peer_wq, peer_subkeys):
    p32 = _deinterleave(MLA_ROPE_DIM)
    p64 = _deinterleave(DIFF_HEAD_DIM)
    z = lambda w: jnp.zeros((D_MODEL, w), F32)
    kr = w_in[:, 640:672][:, p32]
    qd = w_in[:, 672:1184].reshape(D_MODEL, 8, 64)[:, :, p64].reshape(D_MODEL, 512)
    kd = w_in[:, 1184:1696].reshape(D_MODEL, 8, 64)[:, :, p64].reshape(D_MODEL, 512)
    win = jnp.concatenate([w_in[:, 0:640], z(64), kr, z(32), qd, kd, w_in[:, 1696:2208]], axis=1)
    qn = wq_up.reshape(MLA_Q_RANK, MLA_HEADS, MLA_HEAD_DIM)
    qr = wq_rope.reshape(MLA_Q_RANK, MLA_HEADS, MLA_ROPE_DIM)[:, :, p32]
    wq = jnp.concatenate([qn, qr, jnp.zeros((MLA_Q_RANK, MLA_HEADS, 32), F32)], axis=2)
    kn = wk_up.reshape(MLA_KV_RANK, MLA_HEADS, MLA_HEAD_DIM)
    wk = jnp.concatenate([kn, jnp.zeros((MLA_KV_RANK, MLA_HEADS, 64), F32)], axis=2)
    return dict(
        win=win.astype(BF16),
        wq=wq.reshape(MLA_Q_RANK, MLA_HEADS * LANES).astype(BF16),
        wk=wk.reshape(MLA_KV_RANK, MLA_HEADS * LANES).astype(BF16),
        wv=wv_up.astype(BF16),
        wo_m=w_out[:512].astype(BF16),
        wo_d=w_out[512:].astype(BF16),
        pwq=peer_wq.astype(BF16),
        sk=peer_subkeys.astype(BF16),
    )


def kernel(x, c, ctx, c_ctx, norm_attn_g, norm_ffn_g, w_ada, b_ada, w_in, mla_q_norm_g, mla_wq_up, mla_wq_rope, mla_kv_norm_g, mla_wk_up, mla_wv_up, diff_lambda, diff_subnorm_g, w_out, peer_wq, peer_subkeys, peer_u, peer_v, final_norm_g):
    xs = jnp.concatenate([x.reshape(T_LAT, D_MODEL), ctx.reshape(T_CTX, D_MODEL)], axis=0)
    cc = jnp.concatenate([c, c_ctx[None, :], jnp.zeros((MOD_ROWS - BATCH - 1, D_MODEL), F32)], axis=0)
    mod = _modulation(cc, w_ada, b_ada)
    lam_inits = [0.8 - 0.6 * math.exp(-0.3 * l) for l in range(DEPTH)]
    lam_all = _diff_lambda(diff_lambda, jnp.broadcast_to(jnp.asarray(lam_inits, F32)[:, None], (DEPTH, LANES)))
    tab = _rope_tables()

    peer_out = None
    g2_prev = None
    for l in range(DEPTH):
        last = l == DEPTH - 1
        w = _prep_layer_weights(w_in[l], mla_wq_up[l], mla_wq_rope[l], mla_wk_up[l], mla_wv_up[l],
                                w_out[l], peer_wq[l], peer_subkeys[l])
        m = mod[l].reshape(MOD_ROWS, N_MOD, 1, D_MODEL)
        sh1, sc1, g1, sh2, sc2, g2 = (m[:, j] for j in range(N_MOD))
        xs, (qm, km, vm, qd, kd, vd) = _qkv(
            xs, peer_out, g2_prev, norm_attn_g[l][None, :], sc1, sh1, w["win"],
            mla_q_norm_g[l][None, :], mla_kv_norm_g[l][None, :], w["wq"], w["wk"], w["wv"], tab)
        diff_args = dict(lam=lam_all[l][None, :], g=diff_subnorm_g[l][None, :],
                         lam_scale=1.0 - lam_inits[l])
        om = _attention("mla", qm, km, vm, False)
        od = _attention("diff", qd, kd, vd, False, **diff_args)
        if not last:
            om = jnp.concatenate([om, _attention("mla", qm, km, vm, True)], axis=0)
            od = jnp.concatenate([od, _attention("diff", qd, kd, vd, True, **diff_args)], axis=0)
        n_blocks = NB_LAT if last else NB_ALL
        n = n_blocks * TM
        xs, h2, eidx_t, gate_t = _mid(n_blocks, xs, om, od, w["wo_m"], w["wo_d"], g1,
                                      norm_ffn_g[l][None, :], sc2, sh2, w["pwq"], w["sk"])
        idx = eidx_t.T
        gate3 = gate_t.reshape(16, 8, n // 8, 8).transpose(2, 1, 3, 0).reshape(n // 8, 8, LANES)
        act3 = _peer_u(n, idx, h2.reshape(n, SUBLANES, LANES), gate3, _pack_table(peer_u[l]))
        peer_out = _peer_v(n, idx, act3, _pack_table(peer_v[l])).reshape(n, D_MODEL)
        g2_prev = g2
    out = _final(xs, peer_out, g2_prev, final_norm_g[None, :])
    return out.reshape(BATCH, SEQ, D_MODEL)
```

```python
import functools
import math

import jax
import jax.numpy as jnp
import numpy as np
from jax import lax
from jax.experimental import pallas as pl
from jax.experimental.pallas import tpu as pltpu

F32 = jnp.float32
BF16 = jnp.bfloat16

D_MODEL = 1024
BATCH = 8
SEQ = 4096
DEPTH = 4
GRID_W = 64
CTX_LEN = 256
N_MOD = 6
NORM_EPS = 1e-6
ROPE_BASE = 10000.0
MLA_HEADS = 8
MLA_HEAD_DIM = 64
MLA_ROPE_DIM = 32
MLA_Q_RANK = 384
MLA_KV_RANK = 256
DIFF_HEADS = 4
DIFF_HEAD_DIM = 64
PEER_HEADS = 8
PEER_TOPK = 16
N_KEYS = 128
N_EXPERTS = N_KEYS * N_KEYS

T_LAT = BATCH * SEQ
T_CTX = BATCH * CTX_LEN
TT = T_LAT + T_CTX
TM = 256
NB_LAT = T_LAT // TM
NB_ALL = TT // TM
BLK_PER_BATCH = SEQ // TM
MOD_ROWS = 16
PROJ_PAD = 2304
TQ = 256
TK = 512
PEER_SB = 16
LANES = 128
SUBLANES = 8
VMEM_LIMIT = 56 * 1024 * 1024
LOG2_E = math.log2(math.e)


def _cparams(sem, vmem=VMEM_LIMIT):
    return pltpu.CompilerParams(dimension_semantics=sem, vmem_limit_bytes=vmem)


def _rms(x):
    return x * lax.rsqrt(jnp.mean(x * x, axis=-1, keepdims=True) + NORM_EPS)


def _tile_rows_load(ref):
    return jnp.concatenate([ref[pl.ds(s, TM, stride=SUBLANES), :] for s in range(SUBLANES)], axis=1)


def _tile_rows_store(ref, x):
    for s in range(SUBLANES):
        ref[pl.ds(s, TM, stride=SUBLANES), :] = x[:, s * LANES:(s + 1) * LANES]


def _bid(i):
    return jnp.where(i < NB_LAT, i // BLK_PER_BATCH, BATCH)


def _posblk(i):
    return jnp.where(i < NB_LAT, i % BLK_PER_BATCH, BLK_PER_BATCH)


def _mod_body(cc_ref, w_ref, b_ref, o_ref):
    cc = cc_ref[...]
    s = cc / (1.0 + jnp.exp(-cc))
    o_ref[0] = jnp.dot(s, w_ref[0], precision=lax.Precision.HIGHEST,
                       preferred_element_type=F32) + b_ref[0]


def _modulation(cc, w_ada, b_ada):
    nj = N_MOD
    return pl.pallas_call(
        _mod_body,
        grid=(DEPTH, nj),
        in_specs=[
            pl.BlockSpec((MOD_ROWS, D_MODEL), lambda l, j: (0, 0)),
            pl.BlockSpec((1, D_MODEL, D_MODEL), lambda l, j: (l, 0, j)),
            pl.BlockSpec((1, 1, D_MODEL), lambda l, j: (l, 0, j)),
        ],
        out_specs=pl.BlockSpec((1, MOD_ROWS, D_MODEL), lambda l, j: (l, 0, j)),
        out_shape=jax.ShapeDtypeStruct((DEPTH, MOD_ROWS, N_MOD * D_MODEL), F32),
        compiler_params=_cparams(("arbitrary", "arbitrary")),
        name="modulation",
    )(cc, w_ada, b_ada.reshape(DEPTH, 1, N_MOD * D_MODEL))


def _lam_body(d0_ref, d1_ref, d2_ref, d3_ref, li_ref, o_ref):
    a = jnp.sum(d0_ref[...] * d1_ref[...], axis=-1, keepdims=True)
    b = jnp.sum(d2_ref[...] * d3_ref[...], axis=-1, keepdims=True)
    o_ref[...] = jnp.exp(a) - jnp.exp(b) + li_ref[...]


def _diff_lambda(diff_lambda, lam_init):
    dl = diff_lambda.astype(F32)
    return pl.pallas_call(
        _lam_body,
        out_shape=jax.ShapeDtypeStruct((DEPTH, LANES), F32),
        name="diff_lambda",
    )(dl[:, 0], dl[:, 1], dl[:, 2], dl[:, 3], lam_init)


def _rope(xb, c, sp, sm, shift):
    return xb * c + pltpu.roll(xb, shift, 1) * sp + pltpu.roll(xb, LANES - shift, 1) * sm


def _qkv_body(has_peer, *refs):
    if has_peer:
        x_ref, peer_ref, g2_ref = refs[:3]
        refs = refs[3:]
    else:
        x_ref = refs[0]
        refs = refs[1:]
    (ng_ref, sc_ref, sh_ref, win_ref, gq_ref, gkv_ref, wq_ref, wk_ref, wv_ref, tab_ref) = refs[:10]
    outs = refs[10:]
    if has_peer:
        xo_ref, outs = outs[0], outs[1:]
    qm_ref, km_ref, vm_ref, qd_ref, kd_ref, vd_ref = outs

    x = x_ref[...]
    if has_peer:
        x = x + g2_ref[0] * _tile_rows_load(peer_ref)
        xo_ref[...] = x
    h = _rms(x) * ng_ref[...] * (1.0 + sc_ref[0]) + sh_ref[0]
    proj = jnp.dot(h.astype(BF16), win_ref[...], preferred_element_type=F32)
    cq = _rms(proj[:, 0:384]) * gq_ref[...]
    ckv = _rms(proj[:, 384:640]) * gkv_ref[...]
    kr = proj[:, 640:768]
    q = jnp.dot(cq.astype(BF16), wq_ref[...], preferred_element_type=F32)
    ckv16 = ckv.astype(BF16)
    k = jnp.dot(ckv16, wk_ref[...], preferred_element_type=F32)
    v = jnp.dot(ckv16, wv_ref[...], preferred_element_type=F32)
    tab = tab_ref[...]
    cm, spm, smm = tab[:, 0:128], tab[:, 128:256], tab[:, 256:384]
    cd, spd, smd = tab[:, 384:512], tab[:, 512:640], tab[:, 640:768]
    kr_rot = _rope(kr, cm, spm, smm, MLA_ROPE_DIM // 2)
    scale_m = LOG2_E * (MLA_HEAD_DIM + MLA_ROPE_DIM) ** -0.5
    for hh in range(MLA_HEADS):
        sl = slice(hh * LANES, (hh + 1) * LANES)
        qm_ref[:, sl] = (_rope(q[:, sl], cm, spm, smm, MLA_ROPE_DIM // 2) * scale_m).astype(BF16)
        km_ref[:, sl] = (k[:, sl] + kr_rot).astype(BF16)
    vm_ref[...] = v.T.astype(BF16)
    scale_d = LOG2_E * DIFF_HEAD_DIM ** -0.5
    for hh in range(DIFF_HEADS):
        sl = slice(hh * LANES, (hh + 1) * LANES)
        qd = proj[:, 768 + hh * LANES:768 + (hh + 1) * LANES]
        kd = proj[:, 1280 + hh * LANES:1280 + (hh + 1) * LANES]
        qd_ref[:, sl] = (_rope(qd, cd, spd, smd, DIFF_HEAD_DIM // 2) * scale_d).astype(BF16)
        kd_ref[:, sl] = _rope(kd, cd, spd, smd, DIFF_HEAD_DIM // 2).astype(BF16)
    vd_ref[...] = proj[:, 1792:2304].T.astype(BF16)


def _qkv(x, peer, g2, ng, sc, sh, win, gq, gkv, wq, wk, wv, tab):
    has_peer = peer is not None
    row = lambda i: (i, 0)
    modrow = lambda i: (_bid(i), 0, 0)
    const = lambda i: (0, 0)
    in_specs = [pl.BlockSpec((TM, D_MODEL), row)]
    args = [x]
    if has_peer:
        in_specs += [pl.BlockSpec((TM * SUBLANES, LANES), row), pl.BlockSpec((1, 1, D_MODEL), modrow)]
        args += [peer, g2]
    in_specs += [
        pl.BlockSpec((1, D_MODEL), const),
        pl.BlockSpec((1, 1, D_MODEL), modrow),
        pl.BlockSpec((1, 1, D_MODEL), modrow),
        pl.BlockSpec((D_MODEL, PROJ_PAD), const),
        pl.BlockSpec((1, MLA_Q_RANK), const),
        pl.BlockSpec((1, MLA_KV_RANK), const),
        pl.BlockSpec((MLA_Q_RANK, MLA_HEADS * LANES), const),
        pl.BlockSpec((MLA_KV_RANK, MLA_HEADS * LANES), const),
        pl.BlockSpec((MLA_KV_RANK, 512), const),
        pl.BlockSpec((TM, 768), lambda i: (_posblk(i), 0)),
    ]
    args += [ng, sc, sh, win, gq, gkv, wq, wk, wv, tab]
    out_specs, out_shape = [], []
    if has_peer:
        out_specs.append(pl.BlockSpec((TM, D_MODEL), row))
        out_shape.append(jax.ShapeDtypeStruct((TT, D_MODEL), F32))
    for width, transposed in ((1024, False), (1024, False), (512, True),
                              (512, False), (512, False), (512, True)):
        if transposed:
            out_specs.append(pl.BlockSpec((width, TM), lambda i: (0, i)))
            out_shape.append(jax.ShapeDtypeStruct((width, TT), BF16))
        else:
            out_specs.append(pl.BlockSpec((TM, width), row))
            out_shape.append(jax.ShapeDtypeStruct((TT, width), BF16))
    res = pl.pallas_call(
        functools.partial(_qkv_body, has_peer),
        grid=(NB_ALL,),
        in_specs=in_specs,
        out_specs=out_specs,
        out_shape=out_shape,
        compiler_params=_cparams(("parallel",)),
        name="qkv",
    )(*args)
    if has_peer:
        return res[0], res[1:]
    return x, res


def _flash_pair(tq, qs, ksls, vsls, kc_ref, kl_ref, vtc_ref, vtl_ref, n_pairs, sa_scr, sb_scr):
    n_lat = SEQ // TK

    def col_reduce(parts, op, final):
        while len(parts) > 1:
            parts = [op(parts[i], parts[i + 1]) for i in range(0, len(parts), 2)]
        return final(parts[0], axis=0, keepdims=True)

    def row_groups(x):
        return [x[i:i + SUBLANES] for i in range(0, x.shape[0], SUBLANES)]

    def scores(k_of):
        return [lax.dot_general(k_of(ksl), q, (((1,), (1,)), ((), ())), preferred_element_type=F32)
                for q, ksl in zip(qs, ksls)]

    def softmax_pv(read_s, vt_of, carries):
        mid = []
        for h, (m, l, acc) in enumerate(carries):
            m_new = jnp.maximum(m, col_reduce(row_groups(read_s(h)), jnp.maximum, jnp.max))
            alpha = jnp.exp2(m - m_new)
            p = jnp.exp2(read_s(h) - m_new)
            l = alpha * l + col_reduce(row_groups(p), jnp.add, jnp.sum)
            mid.append((m_new, l, alpha * acc, p.astype(BF16)))
        return tuple((m_new, l, acc + jnp.dot(vt_of(vsl), p, preferred_element_type=F32))
                     for (m_new, l, acc, p), vsl in zip(mid, vsls))

    def k_lat(c):
        r0 = pl.multiple_of(c * TK, TK)
        return lambda ksl: kl_ref[pl.ds(r0, TK), ksl]

    def vt_lat(c):
        r0 = pl.multiple_of(c * TK, TK)
        return lambda vsl: vtl_ref[vsl, pl.ds(r0, TK)]

    def stage(scr, sc):
        for h, x in enumerate(sc):
            scr[h] = x

    init = tuple((jnp.full((1, tq), -jnp.inf, F32), jnp.zeros((1, tq), F32),
                  jnp.zeros((vsl.stop - vsl.start, tq), F32)) for vsl in vsls)
    ctx_scores = scores(lambda ksl: kc_ref[:, ksl])
    if n_pairs:
        stage(sa_scr, scores(k_lat(0)))
    carries = softmax_pv(lambda h: ctx_scores[h], lambda vsl: vtc_ref[vsl, :], init)

    def pair(j, carries):
        c = 2 * j
        stage(sb_scr, scores(k_lat(c + 1)))
        carries = softmax_pv(lambda h: sa_scr[h], vt_lat(c), carries)
        stage(sa_scr, scores(k_lat(jnp.minimum(c + 2, n_lat - 1))))
        return softmax_pv(lambda h: sb_scr[h], vt_lat(c + 1), carries)

    if n_pairs:
        carries = lax.fori_loop(0, n_pairs, pair, carries)
    return [acc / l for (m, l, acc) in carries]


def _attn_mla_body(tq, n_pairs, *refs):
    if n_pairs:
        q_ref, kc_ref, kl_ref, vtc_ref, vtl_ref, o_ref, sa_scr, sb_scr = refs
    else:
        q_ref, kc_ref, vtc_ref, o_ref = refs
        kl_ref = vtl_ref = sa_scr = sb_scr = None
    qs = [q_ref[:, 0:LANES], q_ref[:, LANES:2 * LANES]]
    ksls = [slice(0, LANES), slice(LANES, 2 * LANES)]
    vsls = [slice(0, MLA_HEAD_DIM), slice(MLA_HEAD_DIM, 2 * MLA_HEAD_DIM)]
    ot0, ot1 = _flash_pair(tq, qs, ksls, vsls, kc_ref, kl_ref, vtc_ref, vtl_ref, n_pairs, sa_scr, sb_scr)
    o_ref[...] = jnp.concatenate([ot0, ot1], axis=0).T.astype(o_ref.dtype)


def _attn_diff_body(tq, n_pairs, lam_scale, *refs):
    if n_pairs:
        q_ref, kc_ref, kl_ref, vtc_ref, vtl_ref, lam_ref, g_ref, o_ref, sa_scr, sb_scr = refs
    else:
        q_ref, kc_ref, vtc_ref, lam_ref, g_ref, o_ref = refs
        kl_ref = vtl_ref = sa_scr = sb_scr = None
    lane = lax.broadcasted_iota(jnp.int32, (tq, LANES), 1)
    q = q_ref[...]
    zero = jnp.zeros_like(q)
    qs = [jnp.where(lane < DIFF_HEAD_DIM, q, zero), jnp.where(lane < DIFF_HEAD_DIM, zero, q)]
    ksls = [slice(0, LANES), slice(0, LANES)]
    vsls = [slice(0, LANES), slice(0, LANES)]
    ot0, ot1 = _flash_pair(tq, qs, ksls, vsls, kc_ref, kl_ref, vtc_ref, vtl_ref, n_pairs, sa_scr, sb_scr)
    o = ot0.T - lam_ref[...] * ot1.T
    o_ref[...] = (_rms(o) * g_ref[...] * lam_scale).astype(o_ref.dtype)


def _attention(kind, q, k, vt, ctx_queries, lam=None, g=None, lam_scale=None):
    qw = 2 * LANES if kind == "mla" else LANES
    ctx0 = T_LAT // CTX_LEN
    if ctx_queries:
        tq, n_pairs, steps, rows = CTX_LEN, 0, 1, T_CTX
        q_map = lambda b, p, i: (ctx0 + b, p)
        o_map = lambda b, p, i: (b, p)
    else:
        tq, n_pairs, steps, rows = TQ, SEQ // (2 * TK), SEQ // TQ, T_LAT
        q_map = lambda b, p, i: (b * steps + i, p)
        o_map = q_map
    kc_spec = pl.BlockSpec((CTX_LEN, qw), lambda b, p, i: (ctx0 + b, p))
    vtc_spec = pl.BlockSpec((LANES, CTX_LEN), lambda b, p, i: (p, ctx0 + b))
    in_specs = [pl.BlockSpec((tq, qw), q_map), kc_spec]
    args = [q, k]
    if n_pairs:
        in_specs += [pl.BlockSpec((SEQ, qw), lambda b, p, i: (b, p)), vtc_spec,
                     pl.BlockSpec((LANES, SEQ), lambda b, p, i: (p, b))]
        args += [k, vt, vt]
    else:
        in_specs += [vtc_spec]
        args += [vt]
    if kind == "mla":
        body = functools.partial(_attn_mla_body, tq, n_pairs)
    else:
        body = functools.partial(_attn_diff_body, tq, n_pairs, lam_scale)
        in_specs += [pl.BlockSpec((1, LANES), lambda b, p, i: (0, 0)),
                     pl.BlockSpec((1, LANES), lambda b, p, i: (0, 0))]
        args += [lam, g]
    scratch = [pltpu.VMEM((2, TK, tq), F32), pltpu.VMEM((2, TK, tq), F32)] if n_pairs else []
    return pl.pallas_call(
        body,
        grid=(BATCH, 4, steps),
        in_specs=in_specs,
        out_specs=pl.BlockSpec((tq, LANES), o_map),
        out_shape=jax.ShapeDtypeStruct((rows, 512), BF16),
        scratch_shapes=scratch,
        compiler_params=_cparams(("parallel", "parallel", "arbitrary")),
        name="attn_" + kind + ("_ctx" if ctx_queries else ""),
    )(*args)


def _top16(s, payload=None):
    n_rows = s.shape[0]
    rowf = lax.broadcasted_iota(jnp.int32, s.shape, 0).astype(F32)
    slot = lax.broadcasted_iota(jnp.int32, (PEER_TOPK, s.shape[1]), 0)
    vals = jnp.zeros((PEER_TOPK, s.shape[1]), F32)
    picks = jnp.zeros((PEER_TOPK, s.shape[1]), F32)
    for r in range(PEER_TOPK):
        m = jnp.max(s, axis=0, keepdims=True)
        am = jnp.min(jnp.where(s == m, rowf, float(n_rows)), axis=0, keepdims=True)
        hit = rowf == am
        pick = am if payload is None else jnp.max(jnp.where(hit, payload, -1.0), axis=0, keepdims=True)
        vals = jnp.where(slot == r, m, vals)
        picks = jnp.where(slot == r, pick, picks)
        s = jnp.where(hit, -jnp.inf, s)
    return vals, picks


def _staircase(a16, b16, combine, pad):
    tm = a16.shape[1]
    sub = lax.broadcasted_iota(jnp.int32, (SUBLANES, tm), 0)
    a_lo, a_hi = a16[0:SUBLANES], a16[SUBLANES:]
    b_lo, b_hi = b16[0:SUBLANES], b16[SUBLANES:]
    row = lambda x, r: jnp.broadcast_to(x[r:r + 1], (SUBLANES, tm))
    take = lambda x, idx: jnp.take_along_axis(x, idx, axis=0)
    a3 = jnp.where(sub < 5, 2, 3)
    b3 = jnp.where(sub < 5, sub, sub - 5)
    a4 = jnp.where(sub < 1, 3, jnp.where(sub < 4, 4, jnp.where(sub < 6, 5, 6)))
    b4 = jnp.where(sub < 1, 3, jnp.where(sub < 4, sub - 1, jnp.where(sub < 6, sub - 4, sub - 6)))
    pieces = [
        combine(row(a_lo, 0), b_lo),
        combine(row(a_lo, 0), b_hi),
        combine(row(a_lo, 1), b_lo),
        combine(take(a_lo, a3), take(b_lo, b3)),
        combine(take(a_lo, a4), take(b_lo, b4)),
        jnp.where(sub < 2, combine(row(a_lo, 7), b_lo), pad),
        combine(a_hi, row(b_lo, 0)),
    ]
    return jnp.concatenate(pieces, axis=0)


def _mid_body(x_ref, om_ref, od_ref, wo_m_ref, wo_d_ref, g1_ref, ng_ref, sc_ref, sh_ref,
              wq_ref, sk_ref, xo_ref, h_ref, eidx_ref, gate_ref, q_scr):
    y = (jnp.dot(om_ref[...], wo_m_ref[...], preferred_element_type=F32)
         + jnp.dot(od_ref[...], wo_d_ref[...], preferred_element_type=F32))
    x = x_ref[...] + g1_ref[0] * y
    xo_ref[...] = x
    h = _rms(x) * ng_ref[...] * (1.0 + sc_ref[0]) + sh_ref[0]
    _tile_rows_store(h_ref, h)
    q_scr[...] = jnp.dot(h.astype(BF16), wq_ref[...], preferred_element_type=F32).astype(BF16)

    def head(hh, _):
        sv, si = [], []
        for c in range(2):
            c0 = pl.multiple_of(hh * 2 * N_KEYS + c * N_KEYS, N_KEYS)
            qh = q_scr[:, pl.ds(c0, N_KEYS)]
            s = lax.dot_general(sk_ref[c], qh, (((1,), (1,)), ((), ())), preferred_element_type=F32)
            vals, keys = _top16(s)
            sv.append(vals)
            si.append(keys)
        cand_s = _staircase(sv[0], sv[1], lambda a, b: a + b, -jnp.inf)
        cand_e = _staircase(si[0], si[1], lambda a, b: a * float(N_KEYS) + b, -1.0)
        top_s, top_e = _top16(cand_s, cand_e)
        ex = jnp.exp(top_s - jnp.max(top_s, axis=0, keepdims=True))
        gate = ex / jnp.sum(ex, axis=0, keepdims=True)
        r0 = pl.multiple_of(hh * PEER_TOPK, PEER_TOPK)
        eidx_ref[pl.ds(r0, PEER_TOPK), :] = top_e.astype(jnp.int32) * ROWS_PER_EXPERT
        gate_ref[pl.ds(r0, PEER_TOPK), :] = gate
        return 0

    lax.fori_loop(0, PEER_HEADS, head, 0)


def _mid(n_blocks, x, om, od, wo_m, wo_d, g1, ng, sc, sh, wq, sk):
    n = n_blocks * TM
    row = lambda i: (i, 0)
    modrow = lambda i: (_bid(i), 0, 0)
    const = lambda i: (0, 0)
    return pl.pallas_call(
        _mid_body,
        grid=(n_blocks,),
        in_specs=[
            pl.BlockSpec((TM, D_MODEL), row),
            pl.BlockSpec((TM, 512), row),
            pl.BlockSpec((TM, 512), row),
            pl.BlockSpec((512, D_MODEL), const),
            pl.BlockSpec((512, D_MODEL), const),
            pl.BlockSpec((1, 1, D_MODEL), modrow),
            pl.BlockSpec((1, D_MODEL), const),
            pl.BlockSpec((1, 1, D_MODEL), modrow),
            pl.BlockSpec((1, 1, D_MODEL), modrow),
            pl.BlockSpec((D_MODEL, PEER_HEADS * 2 * N_KEYS), const),
            pl.BlockSpec((2, N_KEYS, N_KEYS), lambda i: (0, 0, 0)),
        ],
        out_specs=[
            pl.BlockSpec((TM, D_MODEL), row),
            pl.BlockSpec((TM * SUBLANES, LANES), row),
            pl.BlockSpec((PEER_HEADS * PEER_TOPK, TM), lambda i: (0, i)),
            pl.BlockSpec((PEER_HEADS * PEER_TOPK, TM), lambda i: (0, i)),
        ],
        out_shape=[
            jax.ShapeDtypeStruct((n, D_MODEL), F32),
            jax.ShapeDtypeStruct((n * SUBLANES, LANES), F32),
            jax.ShapeDtypeStruct((PEER_HEADS * PEER_TOPK, n), jnp.int32),
            jax.ShapeDtypeStruct((PEER_HEADS * PEER_TOPK, n), F32),
        ],
        scratch_shapes=[pltpu.VMEM((TM, PEER_HEADS * 2 * N_KEYS), BF16)],
        compiler_params=_cparams(("parallel",)),
        name="mid",
    )(x, om, od, wo_m, wo_d, g1, ng, sc, sh, wq, sk)


N_SEL = PEER_HEADS * PEER_TOPK
HALF = D_MODEL // 2
ROWS_PER_EXPERT = HALF // LANES
_BFLY_ORDER = (0, 4, 2, 6, 1, 5, 3, 7)


PACK_BLOCK = 512


def _pack_body(t_ref, o_ref):
    u = pltpu.bitcast(t_ref[0], jnp.uint32)
    r = u + jnp.uint32(0x7FFF) + ((u >> 16) & jnp.uint32(1))
    w = (r[:, :HALF] >> 16) | (r[:, HALF:] & jnp.uint32(0xFFFF0000))
    for s in range(ROWS_PER_EXPERT):
        o_ref[pl.ds(s, PACK_BLOCK, stride=ROWS_PER_EXPERT), :] = w[:, s * LANES:(s + 1) * LANES]


def _pack_table(tabs, layer):
    n = tabs.shape[1]
    return pl.pallas_call(
        _pack_body,
        grid=(n // PACK_BLOCK,),
        in_specs=[pl.BlockSpec((1, PACK_BLOCK, D_MODEL), lambda i: (layer, i, 0))],
        out_specs=pl.BlockSpec((PACK_BLOCK * ROWS_PER_EXPERT, LANES), lambda i: (i, 0)),
        out_shape=jax.ShapeDtypeStruct((n * ROWS_PER_EXPERT, LANES), jnp.uint32),
        compiler_params=_cparams(("parallel",)),
        name="pack_table",
    )(tabs)


def _load_pair(tab_ref, ra, rb):
    w = jnp.concatenate([tab_ref[pl.ds(pl.multiple_of(ra, ROWS_PER_EXPERT), ROWS_PER_EXPERT), :],
                         tab_ref[pl.ds(pl.multiple_of(rb, ROWS_PER_EXPERT), ROWS_PER_EXPERT), :]], axis=0)
    lo = pltpu.bitcast(w << 16, F32)
    hi = pltpu.bitcast(w & jnp.uint32(0xFFFF0000), F32)
    return lo, hi


def _with_index_buffers(idx_hbm, idx_a, idx_b, sem, first, second):
    step = pl.program_id(0)
    words = PEER_SB * N_SEL

    def idx_copy(block, dst, k):
        return pltpu.make_async_copy(idx_hbm.at[pl.ds(block * words, words)], dst, sem.at[k])

    @pl.when(step == 0)
    def _():
        idx_copy(0, idx_a, 0).start()

    idx_copy(2 * step + 1, idx_b, 1).start()
    idx_copy(2 * step, idx_a, 0).wait()
    first(idx_a)

    @pl.when(step + 1 < pl.num_programs(0))
    def _():
        idx_copy(2 * step + 2, idx_a, 0).start()

    idx_copy(2 * step + 1, idx_b, 1).wait()
    second(idx_b)


def _peer_u_body(idx_hbm, x_ref, gate_ref, tab_ref, act_ref, idx_a, idx_b, r_scr, sem):
    sub = lax.broadcasted_iota(jnp.int32, (SUBLANES, LANES), 0)
    lane = lax.broadcasted_iota(jnp.int32, (SUBLANES, LANES), 1)
    top = sub < 4
    n_grp = N_SEL // SUBLANES
    tiles_per_buf = PEER_SB // SUBLANES

    def fold(p, q, width):
        tp = p + pltpu.roll(p, SUBLANES - width, 0)
        tq = q + pltpu.roll(q, SUBLANES - width, 0)
        return jnp.where((sub % (2 * width)) < width, tp, pltpu.roll(tq, width, 0))

    def gather_tile(idx_s, tile, tile_in_buf):
        for tt in range(SUBLANES):
            tl = tile_in_buf * SUBLANES + tt
            xt = x_ref[tile * SUBLANES + tt]
            xsw = pltpu.roll(xt, 4, 0)
            xlo = jnp.where(top, xt, xsw)
            xhi = jnp.where(top, xsw, xt)
            for g in range(n_grp):
                parts = []
                for pr in range(4):
                    ra = idx_s[tl * N_SEL + g * SUBLANES + _BFLY_ORDER[2 * pr]]
                    rb = idx_s[tl * N_SEL + g * SUBLANES + _BFLY_ORDER[2 * pr + 1]]
                    lo, hi = _load_pair(tab_ref, ra, rb)
                    parts.append(lo * xlo + hi * xhi)
                r_scr[tile * N_SEL + tt * n_grp + g] = fold(
                    fold(parts[0], parts[1], 2), fold(parts[2], parts[3], 2), 1)

    def reduce_tile(tile):
        a = jnp.zeros((SUBLANES, LANES), F32)
        for j in range(N_SEL):
            a = jnp.where(lane == j, jnp.sum(r_scr[tile * N_SEL + j], axis=1, keepdims=True), a)
        act_ref[tile] = 0.5 * a * (1.0 + lax.erf(a * (2.0 ** -0.5))) * gate_ref[tile]

    def first(idx_s):
        for k in range(tiles_per_buf):
            if k:
                reduce_tile(k - 1)
            gather_tile(idx_s, k, k)

    def second(idx_s):
        for k in range(tiles_per_buf):
            reduce_tile(tiles_per_buf + k - 1)
            gather_tile(idx_s, tiles_per_buf + k, k)
        reduce_tile(2 * tiles_per_buf - 1)

    _with_index_buffers(idx_hbm, idx_a, idx_b, sem, first, second)


def _peer_v_body(idx_hbm, act_ref, tab_ref, o_ref, idx_a, idx_b, sem):
    sub = lax.broadcasted_iota(jnp.int32, (SUBLANES, LANES), 0)
    lane = lax.broadcasted_iota(jnp.int32, (SUBLANES, LANES), 1)
    top = sub < 4
    n_grp = N_SEL // SUBLANES
    pair_rows = [jnp.where(top, 2 * p, 2 * p + 1) for p in range(4)]

    def process(idx_s, half):
        for tile in range(PEER_SB // SUBLANES):
            a_tile = act_ref[half * (PEER_SB // SUBLANES) + tile]
            for tt in range(SUBLANES):
                tl = tile * SUBLANES + tt
                acc = [jnp.zeros((SUBLANES, LANES), F32) for _ in range(4)]
                for g in range(n_grp):
                    col = jnp.sum(jnp.where(lane == tt * n_grp + g, a_tile, 0.0), axis=1, keepdims=True)
                    actg = jnp.broadcast_to(col, (SUBLANES, LANES))
                    for p in range(4):
                        ra = idx_s[tl * N_SEL + g * SUBLANES + 2 * p]
                        rb = idx_s[tl * N_SEL + g * SUBLANES + 2 * p + 1]
                        lo, hi = _load_pair(tab_ref, ra, rb)
                        av = jnp.take_along_axis(actg, pair_rows[p], axis=0)
                        k = 2 * (p % 2)
                        acc[k] = acc[k] + av * lo
                        acc[k + 1] = acc[k + 1] + av * hi
                lo = acc[0] + acc[2]
                hi = acc[1] + acc[3]
                lo = lo + pltpu.roll(lo, 4, 0)
                hi = hi + pltpu.roll(hi, 4, 0)
                o_ref[half * PEER_SB + tl] = jnp.where(top, lo, hi)

    _with_index_buffers(idx_hbm, idx_a, idx_b, sem,
                        lambda idx_s: process(idx_s, 0), lambda idx_s: process(idx_s, 1))


def _table_spec():
    return pl.BlockSpec((N_EXPERTS * ROWS_PER_EXPERT, LANES), lambda i: (0, 0),
                        pipeline_mode=pl.Buffered(1))


def _peer_idx_scratch():
    return [pltpu.SMEM((PEER_SB * N_SEL,), jnp.int32), pltpu.SMEM((PEER_SB * N_SEL,), jnp.int32)]


def _peer_u(n, idx, x3, gate3, tab):
    tb = 2 * PEER_SB
    return pl.pallas_call(
        _peer_u_body,
        grid=(n // tb,),
        in_specs=[
            pl.BlockSpec(memory_space=pl.ANY),
            pl.BlockSpec((tb, SUBLANES, LANES), lambda i: (i, 0, 0)),
            pl.BlockSpec((tb // SUBLANES, SUBLANES, LANES), lambda i: (i, 0, 0)),
            _table_spec(),
        ],
        out_specs=pl.BlockSpec((tb // SUBLANES, SUBLANES, LANES), lambda i: (i, 0, 0)),
        out_shape=jax.ShapeDtypeStruct((n // SUBLANES, SUBLANES, LANES), F32),
        scratch_shapes=_peer_idx_scratch() + [
            pltpu.VMEM((tb // SUBLANES * N_SEL, SUBLANES, LANES), F32),
            pltpu.SemaphoreType.DMA((2,))],
        compiler_params=_cparams(("arbitrary",)),
        name="peer_u",
    )(idx.reshape(-1), x3, gate3, tab)


def _peer_v(n, idx, act3, tab):
    tb = 2 * PEER_SB
    return pl.pallas_call(
        _peer_v_body,
        grid=(n // tb,),
        in_specs=[
            pl.BlockSpec(memory_space=pl.ANY),
            pl.BlockSpec((tb // SUBLANES, SUBLANES, LANES), lambda i: (i, 0, 0)),
            _table_spec(),
        ],
        out_specs=pl.BlockSpec((tb, SUBLANES, LANES), lambda i: (i, 0, 0)),
        out_shape=jax.ShapeDtypeStruct((n, SUBLANES, LANES), F32),
        scratch_shapes=_peer_idx_scratch() + [pltpu.SemaphoreType.DMA((2,))],
        compiler_params=_cparams(("arbitrary",)),
        name="peer_v",
    )(idx.reshape(-1), act3, tab)


def _final_body(x_ref, peer_ref, g2_ref, g_ref, o_ref):
    x = x_ref[...] + g2_ref[0] * _tile_rows_load(peer_ref)
    o_ref[...] = _rms(x) * g_ref[...]


def _final(x, peer, g2, g):
    row = lambda i: (i, 0)
    return pl.pallas_call(
        _final_body,
        grid=(NB_LAT,),
        in_specs=[
            pl.BlockSpec((TM, D_MODEL), row),
            pl.BlockSpec((TM * SUBLANES, LANES), row),
            pl.BlockSpec((1, 1, D_MODEL), lambda i: (_bid(i), 0, 0)),
            pl.BlockSpec((1, D_MODEL), lambda i: (0, 0)),
        ],
        out_specs=pl.BlockSpec((TM, D_MODEL), row),
        out_shape=jax.ShapeDtypeStruct((T_LAT, D_MODEL), F32),
        compiler_params=_cparams(("parallel",)),
        name="final_norm",
    )(x, peer, g2, g)


def _deinterleave(width):
    return np.concatenate([np.arange(0, width, 2), np.arange(1, width, 2)])


def _rope_tables():
    pos = np.arange(SEQ)
    rowp = jnp.asarray(pos // GRID_W, F32)
    colp = jnp.asarray(pos % GRID_W, F32)

    def angles(dim):
        quarter = dim // 4
        inv = ROPE_BASE ** (-jnp.arange(quarter, dtype=F32) / quarter)
        return jnp.concatenate([rowp[:, None] * inv, colp[:, None] * inv], axis=-1)

    am = angles(MLA_ROPE_DIM)
    ad = angles(DIFF_HEAD_DIM)
    cm, sm_ = jnp.cos(am), jnp.sin(am)
    cd, sd = jnp.cos(ad), jnp.sin(ad)
    one = lambda w: jnp.ones((SEQ, w), F32)
    zero = lambda w: jnp.zeros((SEQ, w), F32)
    t_cm = jnp.concatenate([one(64), cm, cm, one(32)], axis=1)
    t_spm = jnp.concatenate([zero(80), sm_, zero(32)], axis=1)
    t_smm = jnp.concatenate([zero(64), -sm_, zero(48)], axis=1)
    t_cd = jnp.concatenate([cd, cd, cd, cd], axis=1)
    t_spd = jnp.concatenate([zero(32), sd, zero(32), sd], axis=1)
    t_smd = jnp.concatenate([-sd, zero(32), -sd, zero(32)], axis=1)
    lat = jnp.concatenate([t_cm, t_spm, t_smm, t_cd, t_spd, t_smd], axis=1)
    ident = jnp.concatenate([jnp.ones((CTX_LEN, 128), F32), jnp.zeros((CTX_LEN, 256), F32),
                             jnp.ones((CTX_LEN, 128), F32), jnp.zeros((CTX_LEN, 256), F32)], axis=1)
    return jnp.concatenate([lat, ident], axis=0)


def _prep_layer_weights(w_in, wq_up, wq_rope, wk_up, wv_up, w_out, peer_wq, peer_subkeys):
    p32 = _deinterleave(MLA_ROPE_DIM)
    p64 = _deinterleave(DIFF_HEAD_DIM)
    z = lambda w: jnp.zeros((D_MODEL, w), F32)
    kr = w_in[:, 640:672][:, p32]
    qd = w_in[:, 672:1184].reshape(D_MODEL, 8, 64)[:, :, p64].reshape(D_MODEL, 512)
    kd = w_in[:, 1184:1696].reshape(D_MODEL, 8, 64)[:, :, p64].reshape(D_MODEL, 512)
    win = jnp.concatenate([w_in[:, 0:640], z(64), kr, z(32), qd, kd, w_in[:, 1696:2208]], axis=1)
    qn = wq_up.reshape(MLA_Q_RANK, MLA_HEADS, MLA_HEAD_DIM)
    qr = wq_rope.reshape(MLA_Q_RANK, MLA_HEADS, MLA_ROPE_DIM)[:, :, p32]
    wq = jnp.concatenate([qn, qr, jnp.zeros((MLA_Q_RANK, MLA_HEADS, 32), F32)], axis=2)
    kn = wk_up.reshape(MLA_KV_RANK, MLA_HEADS, MLA_HEAD_DIM)
    wk = jnp.concatenate([kn, jnp.zeros((MLA_KV_RANK, MLA_HEADS, 64), F32)], axis=2)
    return dict(
        win=win.astype(BF16),
        wq=wq.reshape(MLA_Q_RANK, MLA_HEADS * LANES).astype(BF16),
        wk=wk.reshape(MLA_KV_RANK, MLA_HEADS * LANES).astype(BF16),
        wv=wv_up.astype(BF16),
        wo_m=w_out[:512].astype(BF16),
        wo_d=w_out[512:].astype(BF16),
        pwq=peer_wq.astype(BF16),
        sk=peer_subkeys.astype(BF16),
    )


def kernel(x, c, ctx, c_ctx, norm_attn_g, norm_ffn_g, w_ada, b_ada, w_in, mla_q_norm_g, mla_wq_up, mla_wq_rope, mla_kv_norm_g, mla_wk_up, mla_wv_up, diff_lambda, diff_subnorm_g, w_out, peer_wq, peer_subkeys, peer_u, peer_v, final_norm_g):
    xs = jnp.concatenate([x.reshape(T_LAT, D_MODEL), ctx.reshape(T_CTX, D_MODEL)], axis=0)
    cc = jnp.concatenate([c, c_ctx[None, :], jnp.zeros((MOD_ROWS - BATCH - 1, D_MODEL), F32)], axis=0)
    mod = _modulation(cc, w_ada, b_ada)
    lam_inits = [0.8 - 0.6 * math.exp(-0.3 * l) for l in range(DEPTH)]
    lam_all = _diff_lambda(diff_lambda, jnp.broadcast_to(jnp.asarray(lam_inits, F32)[:, None], (DEPTH, LANES)))
    tab = _rope_tables()

    peer_out = None
    g2_prev = None
    for l in range(DEPTH):
        last = l == DEPTH - 1
        w = _prep_layer_weights(w_in[l], mla_wq_up[l], mla_wq_rope[l], mla_wk_up[l], mla_wv_up[l],
                                w_out[l], peer_wq[l], peer_subkeys[l])
        m = mod[l].reshape(MOD_ROWS, N_MOD, 1, D_MODEL)
        sh1, sc1, g1, sh2, sc2, g2 = (m[:, j] for j in range(N_MOD))
        xs, (qm, km, vm, qd, kd, vd) = _qkv(
            xs, peer_out, g2_prev, norm_attn_g[l][None, :], sc1, sh1, w["win"],
            mla_q_norm_g[l][None, :], mla_kv_norm_g[l][None, :], w["wq"], w["wk"], w["wv"], tab)
        diff_args = dict(lam=lam_all[l][None, :], g=diff_subnorm_g[l][None, :],
                         lam_scale=1.0 - lam_inits[l])
        om = _attention("mla", qm, km, vm, False)
        od = _attention("diff", qd, kd, vd, False, **diff_args)
        if not last:
            om = jnp.concatenate([om, _attention("mla", qm, km, vm, True)], axis=0)
            od = jnp.concatenate([od, _attention("diff", qd, kd, vd, True, **diff_args)], axis=0)
        n_blocks = NB_LAT if last else NB_ALL
        n = n_blocks * TM
        xs, h2, eidx_t, gate_t = _mid(n_blocks, xs, om, od, w["wo_m"], w["wo_d"], g1,
                                      norm_ffn_g[l][None, :], sc2, sh2, w["pwq"], w["sk"])
        idx = eidx_t.T
        gate3 = gate_t.reshape(16, 8, n // 8, 8).transpose(2, 1, 3, 0).reshape(n // 8, 8, LANES)
        act3 = _peer_u(n, idx, h2.reshape(n, SUBLANES, LANES), gate3, _pack_table(peer_u, l))
        peer_out = _peer_v(n, idx, act3, _pack_table(peer_v, l)).reshape(n * SUBLANES, LANES)
        g2_prev = g2
    out = _final(xs, peer_out, g2_prev, final_norm_g[None, :])
    return out.reshape(BATCH, SEQ, D_MODEL)
```

```python
import functools
import math

import jax
import jax.numpy as jnp
import numpy as np
from jax import lax
from jax.experimental import pallas as pl
from jax.experimental.pallas import tpu as pltpu

F32 = jnp.float32
BF16 = jnp.bfloat16

D_MODEL = 1024
BATCH = 8
SEQ = 4096
DEPTH = 4
GRID_W = 64
CTX_LEN = 256
N_MOD = 6
NORM_EPS = 1e-6
ROPE_BASE = 10000.0
MLA_HEADS = 8
MLA_HEAD_DIM = 64
MLA_ROPE_DIM = 32
MLA_Q_RANK = 384
MLA_KV_RANK = 256
DIFF_HEADS = 4
DIFF_HEAD_DIM = 64
PEER_HEADS = 8
PEER_TOPK = 16
N_KEYS = 128
N_EXPERTS = N_KEYS * N_KEYS

T_LAT = BATCH * SEQ
T_CTX = BATCH * CTX_LEN
TT = T_LAT + T_CTX
TM = 256
NB_LAT = T_LAT // TM
NB_ALL = TT // TM
BLK_PER_BATCH = SEQ // TM
MOD_ROWS = 16
PROJ_PAD = 2304
TQ = 256
TK = 512
PEER_SB = 16
LANES = 128
SUBLANES = 8
VMEM_LIMIT = 56 * 1024 * 1024
LOG2_E = math.log2(math.e)


def _cparams(sem, vmem=VMEM_LIMIT):
    return pltpu.CompilerParams(dimension_semantics=sem, vmem_limit_bytes=vmem)


def _rms(x):
    return x * lax.rsqrt(jnp.mean(x * x, axis=-1, keepdims=True) + NORM_EPS)


def _tile_rows_load(ref):
    return jnp.concatenate([ref[pl.ds(s, TM, stride=SUBLANES), :] for s in range(SUBLANES)], axis=1)


def _tile_rows_store(ref, x):
    for s in range(SUBLANES):
        ref[pl.ds(s, TM, stride=SUBLANES), :] = x[:, s * LANES:(s + 1) * LANES]


def _bid(i):
    return jnp.where(i < NB_LAT, i // BLK_PER_BATCH, BATCH)


def _posblk(i):
    return jnp.where(i < NB_LAT, i % BLK_PER_BATCH, BLK_PER_BATCH)


def _mod_body(cc_ref, w_ref, b_ref, o_ref):
    cc = cc_ref[...]
    s = cc / (1.0 + jnp.exp(-cc))
    o_ref[0] = jnp.dot(s, w_ref[0], precision=lax.Precision.HIGHEST,
                       preferred_element_type=F32) + b_ref[0]


def _modulation(cc, w_ada, b_ada):
    nj = N_MOD
    return pl.pallas_call(
        _mod_body,
        grid=(DEPTH, nj),
        in_specs=[
            pl.BlockSpec((MOD_ROWS, D_MODEL), lambda l, j: (0, 0)),
            pl.BlockSpec((1, D_MODEL, D_MODEL), lambda l, j: (l, 0, j)),
            pl.BlockSpec((1, 1, D_MODEL), lambda l, j: (l, 0, j)),
        ],
        out_specs=pl.BlockSpec((1, MOD_ROWS, D_MODEL), lambda l, j: (l, 0, j)),
        out_shape=jax.ShapeDtypeStruct((DEPTH, MOD_ROWS, N_MOD * D_MODEL), F32),
        compiler_params=_cparams(("arbitrary", "arbitrary")),
        name="modulation",
    )(cc, w_ada, b_ada.reshape(DEPTH, 1, N_MOD * D_MODEL))


def _lam_body(d0_ref, d1_ref, d2_ref, d3_ref, li_ref, o_ref):
    a = jnp.sum(d0_ref[...] * d1_ref[...], axis=-1, keepdims=True)
    b = jnp.sum(d2_ref[...] * d3_ref[...], axis=-1, keepdims=True)
    o_ref[...] = jnp.exp(a) - jnp.exp(b) + li_ref[...]


def _diff_lambda(diff_lambda, lam_init):
    dl = diff_lambda.astype(F32)
    return pl.pallas_call(
        _lam_body,
        out_shape=jax.ShapeDtypeStruct((DEPTH, LANES), F32),
        name="diff_lambda",
    )(dl[:, 0], dl[:, 1], dl[:, 2], dl[:, 3], lam_init)


def _rope(xb, c, sp, sm, shift):
    return xb * c + pltpu.roll(xb, shift, 1) * sp + pltpu.roll(xb, LANES - shift, 1) * sm


def _qkv_body(has_peer, *refs):
    if has_peer:
        x_ref, peer_ref, g2_ref = refs[:3]
        refs = refs[3:]
    else:
        x_ref = refs[0]
        refs = refs[1:]
    (ng_ref, sc_ref, sh_ref, win_ref, gq_ref, gkv_ref, wq_ref, wk_ref, wv_ref, tab_ref) = refs[:10]
    outs = refs[10:]
    if has_peer:
        xo_ref, outs = outs[0], outs[1:]
    qm_ref, km_ref, vm_ref, qd_ref, kd_ref, vd_ref = outs

    x = x_ref[...]
    if has_peer:
        x = x + g2_ref[0] * _tile_rows_load(peer_ref)
        xo_ref[...] = x
    h = _rms(x) * ng_ref[...] * (1.0 + sc_ref[0]) + sh_ref[0]
    proj = jnp.dot(h.astype(BF16), win_ref[...], preferred_element_type=F32)
    cq = _rms(proj[:, 0:384]) * gq_ref[...]
    ckv = _rms(proj[:, 384:640]) * gkv_ref[...]
    kr = proj[:, 640:768]
    q = jnp.dot(cq.astype(BF16), wq_ref[...], preferred_element_type=F32)
    ckv16 = ckv.astype(BF16)
    k = jnp.dot(ckv16, wk_ref[...], preferred_element_type=F32)
    v = jnp.dot(ckv16, wv_ref[...], preferred_element_type=F32)
    tab = tab_ref[...]
    cm, spm, smm = tab[:, 0:128], tab[:, 128:256], tab[:, 256:384]
    cd, spd, smd = tab[:, 384:512], tab[:, 512:640], tab[:, 640:768]
    kr_rot = _rope(kr, cm, spm, smm, MLA_ROPE_DIM // 2)
    scale_m = LOG2_E * (MLA_HEAD_DIM + MLA_ROPE_DIM) ** -0.5
    for hh in range(MLA_HEADS):
        sl = slice(hh * LANES, (hh + 1) * LANES)
        qm_ref[:, sl] = (_rope(q[:, sl], cm, spm, smm, MLA_ROPE_DIM // 2) * scale_m).astype(BF16)
        km_ref[:, sl] = (k[:, sl] + kr_rot).astype(BF16)
    vm_ref[...] = v.T.astype(BF16)
    scale_d = LOG2_E * DIFF_HEAD_DIM ** -0.5
    for hh in range(DIFF_HEADS):
        sl = slice(hh * LANES, (hh + 1) * LANES)
        qd = proj[:, 768 + hh * LANES:768 + (hh + 1) * LANES]
        kd = proj[:, 1280 + hh * LANES:1280 + (hh + 1) * LANES]
        qd_ref[:, sl] = (_rope(qd, cd, spd, smd, DIFF_HEAD_DIM // 2) * scale_d).astype(BF16)
        kd_ref[:, sl] = _rope(kd, cd, spd, smd, DIFF_HEAD_DIM // 2).astype(BF16)
    vd_ref[...] = proj[:, 1792:2304].T.astype(BF16)


def _qkv(x, peer, g2, ng, sc, sh, win, gq, gkv, wq, wk, wv, tab):
    has_peer = peer is not None
    row = lambda i: (i, 0)
    modrow = lambda i: (_bid(i), 0, 0)
    const = lambda i: (0, 0)
    in_specs = [pl.BlockSpec((TM, D_MODEL), row)]
    args = [x]
    if has_peer:
        in_specs += [pl.BlockSpec((TM * SUBLANES, LANES), row), pl.BlockSpec((1, 1, D_MODEL), modrow)]
        args += [peer, g2]
    in_specs += [
        pl.BlockSpec((1, D_MODEL), const),
        pl.BlockSpec((1, 1, D_MODEL), modrow),
        pl.BlockSpec((1, 1, D_MODEL), modrow),
        pl.BlockSpec((D_MODEL, PROJ_PAD), const),
        pl.BlockSpec((1, MLA_Q_RANK), const),
        pl.BlockSpec((1, MLA_KV_RANK), const),
        pl.BlockSpec((MLA_Q_RANK, MLA_HEADS * LANES), const),
        pl.BlockSpec((MLA_KV_RANK, MLA_HEADS * LANES), const),
        pl.BlockSpec((MLA_KV_RANK, 512), const),
        pl.BlockSpec((TM, 768), lambda i: (_posblk(i), 0)),
    ]
    args += [ng, sc, sh, win, gq, gkv, wq, wk, wv, tab]
    out_specs, out_shape = [], []
    if has_peer:
        out_specs.append(pl.BlockSpec((TM, D_MODEL), row))
        out_shape.append(jax.ShapeDtypeStruct((TT, D_MODEL), F32))
    for width, transposed in ((1024, False), (1024, False), (512, True),
                              (512, False), (512, False), (512, True)):
        if transposed:
            out_specs.append(pl.BlockSpec((width, TM), lambda i: (0, i)))
            out_shape.append(jax.ShapeDtypeStruct((width, TT), BF16))
        else:
            out_specs.append(pl.BlockSpec((TM, width), row))
            out_shape.append(jax.ShapeDtypeStruct((TT, width), BF16))
    res = pl.pallas_call(
        functools.partial(_qkv_body, has_peer),
        grid=(NB_ALL,),
        in_specs=in_specs,
        out_specs=out_specs,
        out_shape=out_shape,
        compiler_params=_cparams(("parallel",)),
        name="qkv",
    )(*args)
    if has_peer:
        return res[0], res[1:]
    return x, res


def _flash_pair(tq, qs, ksls, vsls, kc_ref, kl_ref, vtc_ref, vtl_ref, n_pairs, sa_scr, sb_scr):
    n_lat = SEQ // TK

    def col_reduce(parts, op, final):
        while len(parts) > 1:
            parts = [op(parts[i], parts[i + 1]) for i in range(0, len(parts), 2)]
        return final(parts[0], axis=0, keepdims=True)

    def row_groups(x):
        return [x[i:i + SUBLANES] for i in range(0, x.shape[0], SUBLANES)]

    def scores(k_of):
        return [lax.dot_general(k_of(ksl), q, (((1,), (1,)), ((), ())), preferred_element_type=F32)
                for q, ksl in zip(qs, ksls)]

    def softmax_pv(read_s, vt_of, carries):
        mid = []
        for h, (m, l, acc) in enumerate(carries):
            m_new = jnp.maximum(m, col_reduce(row_groups(read_s(h)), jnp.maximum, jnp.max))
            alpha = jnp.exp2(m - m_new)
            p = jnp.exp2(read_s(h) - m_new)
            l = alpha * l + col_reduce(row_groups(p), jnp.add, jnp.sum)
            mid.append((m_new, l, alpha * acc, p.astype(BF16)))
        return tuple((m_new, l, acc + jnp.dot(vt_of(vsl), p, preferred_element_type=F32))
                     for (m_new, l, acc, p), vsl in zip(mid, vsls))

    def k_lat(c):
        return lambda ksl: kl_ref[c * TK:(c + 1) * TK, ksl]

    def vt_lat(c):
        return lambda vsl: vtl_ref[vsl, c * TK:(c + 1) * TK]

    def stage(scr, sc):
        for h, x in enumerate(sc):
            scr[h] = x

    init = tuple((jnp.full((1, tq), -jnp.inf, F32), jnp.zeros((1, tq), F32),
                  jnp.zeros((vsl.stop - vsl.start, tq), F32)) for vsl in vsls)
    ctx_scores = scores(lambda ksl: kc_ref[:, ksl])
    if n_pairs:
        stage(sa_scr, scores(k_lat(0)))
    carries = softmax_pv(lambda h: ctx_scores[h], lambda vsl: vtc_ref[vsl, :], init)

    def pair(j, carries):
        c = 2 * j
        stage(sb_scr, scores(k_lat(c + 1)))
        carries = softmax_pv(lambda h: sa_scr[h], vt_lat(c), carries)
        if c + 2 < n_lat:
            stage(sa_scr, scores(k_lat(c + 2)))
        return softmax_pv(lambda h: sb_scr[h], vt_lat(c + 1), carries)

    for j in range(n_pairs):
        carries = pair(j, carries)
    return [acc / l for (m, l, acc) in carries]


def _attn_mla_body(tq, n_pairs, *refs):
    if n_pairs:
        q_ref, kc_ref, kl_ref, vtc_ref, vtl_ref, o_ref, sa_scr, sb_scr = refs
    else:
        q_ref, kc_ref, vtc_ref, o_ref = refs
        kl_ref = vtl_ref = sa_scr = sb_scr = None
    qs = [q_ref[:, 0:LANES], q_ref[:, LANES:2 * LANES]]
    ksls = [slice(0, LANES), slice(LANES, 2 * LANES)]
    vsls = [slice(0, MLA_HEAD_DIM), slice(MLA_HEAD_DIM, 2 * MLA_HEAD_DIM)]
    ot0, ot1 = _flash_pair(tq, qs, ksls, vsls, kc_ref, kl_ref, vtc_ref, vtl_ref, n_pairs, sa_scr, sb_scr)
    o_ref[...] = jnp.concatenate([ot0, ot1], axis=0).T.astype(o_ref.dtype)


def _attn_diff_body(tq, n_pairs, lam_scale, *refs):
    if n_pairs:
        q_ref, kc_ref, kl_ref, vtc_ref, vtl_ref, lam_ref, g_ref, o_ref, sa_scr, sb_scr = refs
    else:
        q_ref, kc_ref, vtc_ref, lam_ref, g_ref, o_ref = refs
        kl_ref = vtl_ref = sa_scr = sb_scr = None
    lane = lax.broadcasted_iota(jnp.int32, (tq, LANES), 1)
    q = q_ref[...]
    zero = jnp.zeros_like(q)
    qs = [jnp.where(lane < DIFF_HEAD_DIM, q, zero), jnp.where(lane < DIFF_HEAD_DIM, zero, q)]
    ksls = [slice(0, LANES), slice(0, LANES)]
    vsls = [slice(0, LANES), slice(0, LANES)]
    ot0, ot1 = _flash_pair(tq, qs, ksls, vsls, kc_ref, kl_ref, vtc_ref, vtl_ref, n_pairs, sa_scr, sb_scr)
    o = ot0.T - lam_ref[...] * ot1.T
    o_ref[...] = (_rms(o) * g_ref[...] * lam_scale).astype(o_ref.dtype)


def _attention(kind, q, k, vt, ctx_queries, lam=None, g=None, lam_scale=None):
    qw = 2 * LANES if kind == "mla" else LANES
    ctx0 = T_LAT // CTX_LEN
    if ctx_queries:
        tq, n_pairs, steps, rows = CTX_LEN, 0, 1, T_CTX
        q_map = lambda b, p, i: (ctx0 + b, p)
        o_map = lambda b, p, i: (b, p)
    else:
        tq, n_pairs, steps, rows = TQ, SEQ // (2 * TK), SEQ // TQ, T_LAT
        q_map = lambda b, p, i: (b * steps + i, p)
        o_map = q_map
    kc_spec = pl.BlockSpec((CTX_LEN, qw), lambda b, p, i: (ctx0 + b, p))
    vtc_spec = pl.BlockSpec((LANES, CTX_LEN), lambda b, p, i: (p, ctx0 + b))
    in_specs = [pl.BlockSpec((tq, qw), q_map), kc_spec]
    args = [q, k]
    if n_pairs:
        in_specs += [pl.BlockSpec((SEQ, qw), lambda b, p, i: (b, p)), vtc_spec,
                     pl.BlockSpec((LANES, SEQ), lambda b, p, i: (p, b))]
        args += [k, vt, vt]
    else:
        in_specs += [vtc_spec]
        args += [vt]
    if kind == "mla":
        body = functools.partial(_attn_mla_body, tq, n_pairs)
    else:
        body = functools.partial(_attn_diff_body, tq, n_pairs, lam_scale)
        in_specs += [pl.BlockSpec((1, LANES), lambda b, p, i: (0, 0)),
                     pl.BlockSpec((1, LANES), lambda b, p, i: (0, 0))]
        args += [lam, g]
    scratch = [pltpu.VMEM((2, TK, tq), F32), pltpu.VMEM((2, TK, tq), F32)] if n_pairs else []
    return pl.pallas_call(
        body,
        grid=(BATCH, 4, steps),
        in_specs=in_specs,
        out_specs=pl.BlockSpec((tq, LANES), o_map),
        out_shape=jax.ShapeDtypeStruct((rows, 512), BF16),
        scratch_shapes=scratch,
        compiler_params=_cparams(("parallel", "parallel", "arbitrary")),
        name="attn_" + kind + ("_ctx" if ctx_queries else ""),
    )(*args)


def _top16(s, payload=None):
    n_rows = s.shape[0]
    rowf = lax.broadcasted_iota(jnp.int32, s.shape, 0).astype(F32)
    slot = lax.broadcasted_iota(jnp.int32, (PEER_TOPK, s.shape[1]), 0)
    vals = jnp.zeros((PEER_TOPK, s.shape[1]), F32)
    picks = jnp.zeros((PEER_TOPK, s.shape[1]), F32)
    for r in range(PEER_TOPK):
        m = jnp.max(s, axis=0, keepdims=True)
        am = jnp.min(jnp.where(s == m, rowf, float(n_rows)), axis=0, keepdims=True)
        hit = rowf == am
        pick = am if payload is None else jnp.max(jnp.where(hit, payload, -1.0), axis=0, keepdims=True)
        vals = jnp.where(slot == r, m, vals)
        picks = jnp.where(slot == r, pick, picks)
        s = jnp.where(hit, -jnp.inf, s)
    return vals, picks


def _staircase(a16, b16, combine, pad):
    tm = a16.shape[1]
    sub = lax.broadcasted_iota(jnp.int32, (SUBLANES, tm), 0)
    a_lo, a_hi = a16[0:SUBLANES], a16[SUBLANES:]
    b_lo, b_hi = b16[0:SUBLANES], b16[SUBLANES:]
    row = lambda x, r: jnp.broadcast_to(x[r:r + 1], (SUBLANES, tm))
    take = lambda x, idx: jnp.take_along_axis(x, idx, axis=0)
    a3 = jnp.where(sub < 5, 2, 3)
    b3 = jnp.where(sub < 5, sub, sub - 5)
    a4 = jnp.where(sub < 1, 3, jnp.where(sub < 4, 4, jnp.where(sub < 6, 5, 6)))
    b4 = jnp.where(sub < 1, 3, jnp.where(sub < 4, sub - 1, jnp.where(sub < 6, sub - 4, sub - 6)))
    pieces = [
        combine(row(a_lo, 0), b_lo),
        combine(row(a_lo, 0), b_hi),
        combine(row(a_lo, 1), b_lo),
        combine(take(a_lo, a3), take(b_lo, b3)),
        combine(take(a_lo, a4), take(b_lo, b4)),
        jnp.where(sub < 2, combine(row(a_lo, 7), b_lo), pad),
        combine(a_hi, row(b_lo, 0)),
    ]
    return jnp.concatenate(pieces, axis=0)


def _mid_body(x_ref, om_ref, od_ref, wo_m_ref, wo_d_ref, g1_ref, ng_ref, sc_ref, sh_ref,
              wq_ref, sk_ref, xo_ref, h_ref, eidx_ref, gate_ref, q_scr):
    y = (jnp.dot(om_ref[...], wo_m_ref[...], preferred_element_type=F32)
         + jnp.dot(od_ref[...], wo_d_ref[...], preferred_element_type=F32))
    x = x_ref[...] + g1_ref[0] * y
    xo_ref[...] = x
    h = _rms(x) * ng_ref[...] * (1.0 + sc_ref[0]) + sh_ref[0]
    _tile_rows_store(h_ref, h)
    q_scr[...] = jnp.dot(h.astype(BF16), wq_ref[...], preferred_element_type=F32).astype(BF16)

    def head(hh, _):
        sv, si = [], []
        for c in range(2):
            c0 = pl.multiple_of(hh * 2 * N_KEYS + c * N_KEYS, N_KEYS)
            qh = q_scr[:, pl.ds(c0, N_KEYS)]
            s = lax.dot_general(sk_ref[c], qh, (((1,), (1,)), ((), ())), preferred_element_type=F32)
            vals, keys = _top16(s)
            sv.append(vals)
            si.append(keys)
        cand_s = _staircase(sv[0], sv[1], lambda a, b: a + b, -jnp.inf)
        cand_e = _staircase(si[0], si[1], lambda a, b: a * float(N_KEYS) + b, -1.0)
        top_s, top_e = _top16(cand_s, cand_e)
        ex = jnp.exp(top_s - jnp.max(top_s, axis=0, keepdims=True))
        gate = ex / jnp.sum(ex, axis=0, keepdims=True)
        r0 = pl.multiple_of(hh * PEER_TOPK, PEER_TOPK)
        eidx_ref[pl.ds(r0, PEER_TOPK), :] = top_e.astype(jnp.int32) * ROWS_PER_EXPERT
        gate_ref[pl.ds(r0, PEER_TOPK), :] = gate
        return 0

    lax.fori_loop(0, PEER_HEADS, head, 0)


def _mid(n_blocks, x, om, od, wo_m, wo_d, g1, ng, sc, sh, wq, sk):
    n = n_blocks * TM
    row = lambda i: (i, 0)
    modrow = lambda i: (_bid(i), 0, 0)
    const = lambda i: (0, 0)
    return pl.pallas_call(
        _mid_body,
        grid=(n_blocks,),
        in_specs=[
            pl.BlockSpec((TM, D_MODEL), row),
            pl.BlockSpec((TM, 512), row),
            pl.BlockSpec((TM, 512), row),
            pl.BlockSpec((512, D_MODEL), const),
            pl.BlockSpec((512, D_MODEL), const),
            pl.BlockSpec((1, 1, D_MODEL), modrow),
            pl.BlockSpec((1, D_MODEL), const),
            pl.BlockSpec((1, 1, D_MODEL), modrow),
            pl.BlockSpec((1, 1, D_MODEL), modrow),
            pl.BlockSpec((D_MODEL, PEER_HEADS * 2 * N_KEYS), const),
            pl.BlockSpec((2, N_KEYS, N_KEYS), lambda i: (0, 0, 0)),
        ],
        out_specs=[
            pl.BlockSpec((TM, D_MODEL), row),
            pl.BlockSpec((TM * SUBLANES, LANES), row),
            pl.BlockSpec((PEER_HEADS * PEER_TOPK, TM), lambda i: (0, i)),
            pl.BlockSpec((PEER_HEADS * PEER_TOPK, TM), lambda i: (0, i)),
        ],
        out_shape=[
            jax.ShapeDtypeStruct((n, D_MODEL), F32),
            jax.ShapeDtypeStruct((n * SUBLANES, LANES), F32),
            jax.ShapeDtypeStruct((PEER_HEADS * PEER_TOPK, n), jnp.int32),
            jax.ShapeDtypeStruct((PEER_HEADS * PEER_TOPK, n), F32),
        ],
        scratch_shapes=[pltpu.VMEM((TM, PEER_HEADS * 2 * N_KEYS), BF16)],
        compiler_params=_cparams(("parallel",)),
        name="mid",
    )(x, om, od, wo_m, wo_d, g1, ng, sc, sh, wq, sk)


N_SEL = PEER_HEADS * PEER_TOPK
HALF = D_MODEL // 2
ROWS_PER_EXPERT = HALF // LANES
_BFLY_ORDER = (0, 4, 2, 6, 1, 5, 3, 7)


PACK_BLOCK = 512


def _pack_body(t_ref, o_ref):
    u = pltpu.bitcast(t_ref[0], jnp.uint32)
    r = u + jnp.uint32(0x7FFF) + ((u >> 16) & jnp.uint32(1))
    w = (r[:, :HALF] >> 16) | (r[:, HALF:] & jnp.uint32(0xFFFF0000))
    for s in range(ROWS_PER_EXPERT):
        o_ref[pl.ds(s, PACK_BLOCK, stride=ROWS_PER_EXPERT), :] = w[:, s * LANES:(s + 1) * LANES]


def _pack_table(tabs, layer):
    n = tabs.shape[1]
    return pl.pallas_call(
        _pack_body,
        grid=(n // PACK_BLOCK,),
        in_specs=[pl.BlockSpec((1, PACK_BLOCK, D_MODEL), lambda i: (layer, i, 0))],
        out_specs=pl.BlockSpec((PACK_BLOCK * ROWS_PER_EXPERT, LANES), lambda i: (i, 0)),
        out_shape=jax.ShapeDtypeStruct((n * ROWS_PER_EXPERT, LANES), jnp.uint32),
        compiler_params=_cparams(("parallel",)),
        name="pack_table",
    )(tabs)


def _load_pair(tab_ref, ra, rb):
    w = jnp.concatenate([tab_ref[pl.ds(pl.multiple_of(ra, ROWS_PER_EXPERT), ROWS_PER_EXPERT), :],
                         tab_ref[pl.ds(pl.multiple_of(rb, ROWS_PER_EXPERT), ROWS_PER_EXPERT), :]], axis=0)
    lo = pltpu.bitcast(w << 16, F32)
    hi = pltpu.bitcast(w & jnp.uint32(0xFFFF0000), F32)
    return lo, hi


def _with_index_buffers(idx_hbm, idx_a, idx_b, sem, first, second):
    step = pl.program_id(0)
    words = PEER_SB * N_SEL

    def idx_copy(block, dst, k):
        return pltpu.make_async_copy(idx_hbm.at[pl.ds(block * words, words)], dst, sem.at[k])

    @pl.when(step == 0)
    def _():
        idx_copy(0, idx_a, 0).start()

    idx_copy(2 * step + 1, idx_b, 1).start()
    idx_copy(2 * step, idx_a, 0).wait()
    first(idx_a)

    @pl.when(step + 1 < pl.num_programs(0))
    def _():
        idx_copy(2 * step + 2, idx_a, 0).start()

    idx_copy(2 * step + 1, idx_b, 1).wait()
    second(idx_b)


def _peer_u_body(idx_hbm, x_ref, gate_ref, tab_ref, act_ref, idx_a, idx_b, r_scr, sem):
    sub = lax.broadcasted_iota(jnp.int32, (SUBLANES, LANES), 0)
    lane = lax.broadcasted_iota(jnp.int32, (SUBLANES, LANES), 1)
    top = sub < 4
    n_grp = N_SEL // SUBLANES
    tiles_per_buf = PEER_SB // SUBLANES

    def fold(p, q, width):
        tp = p + pltpu.roll(p, SUBLANES - width, 0)
        tq = q + pltpu.roll(q, SUBLANES - width, 0)
        return jnp.where((sub % (2 * width)) < width, tp, pltpu.roll(tq, width, 0))

    def gather_tile(idx_s, tile, tile_in_buf):
        for tt in range(SUBLANES):
            tl = tile_in_buf * SUBLANES + tt
            xt = x_ref[tile * SUBLANES + tt]
            xsw = pltpu.roll(xt, 4, 0)
            xlo = jnp.where(top, xt, xsw)
            xhi = jnp.where(top, xsw, xt)
            for g in range(n_grp):
                parts = []
                for pr in range(4):
                    ra = idx_s[tl * N_SEL + g * SUBLANES + _BFLY_ORDER[2 * pr]]
                    rb = idx_s[tl * N_SEL + g * SUBLANES + _BFLY_ORDER[2 * pr + 1]]
                    lo, hi = _load_pair(tab_ref, ra, rb)
                    parts.append(lo * xlo + hi * xhi)
                r_scr[tile * N_SEL + tt * n_grp + g] = fold(
                    fold(parts[0], parts[1], 2), fold(parts[2], parts[3], 2), 1)

    def reduce_tile(tile):
        a = jnp.zeros((SUBLANES, LANES), F32)
        for j in range(N_SEL):
            a = jnp.where(lane == j, jnp.sum(r_scr[tile * N_SEL + j], axis=1, keepdims=True), a)
        act_ref[tile] = 0.5 * a * (1.0 + lax.erf(a * (2.0 ** -0.5))) * gate_ref[tile]

    def first(idx_s):
        for k in range(tiles_per_buf):
            if k:
                reduce_tile(k - 1)
            gather_tile(idx_s, k, k)

    def second(idx_s):
        for k in range(tiles_per_buf):
            reduce_tile(tiles_per_buf + k - 1)
            gather_tile(idx_s, tiles_per_buf + k, k)
        reduce_tile(2 * tiles_per_buf - 1)

    _with_index_buffers(idx_hbm, idx_a, idx_b, sem, first, second)


def _peer_v_body(idx_hbm, act_ref, tab_ref, o_ref, idx_a, idx_b, sem):
    sub = lax.broadcasted_iota(jnp.int32, (SUBLANES, LANES), 0)
    lane = lax.broadcasted_iota(jnp.int32, (SUBLANES, LANES), 1)
    top = sub < 4
    n_grp = N_SEL // SUBLANES
    pair_rows = [jnp.where(top, 2 * p, 2 * p + 1) for p in range(4)]

    def process(idx_s, half):
        for tile in range(PEER_SB // SUBLANES):
            a_tile = act_ref[half * (PEER_SB // SUBLANES) + tile]
            for tt in range(SUBLANES):
                tl = tile * SUBLANES + tt
                acc = [jnp.zeros((SUBLANES, LANES), F32) for _ in range(4)]
                for g in range(n_grp):
                    col = jnp.sum(jnp.where(lane == tt * n_grp + g, a_tile, 0.0), axis=1, keepdims=True)
                    actg = jnp.broadcast_to(col, (SUBLANES, LANES))
                    for p in range(4):
                        ra = idx_s[tl * N_SEL + g * SUBLANES + 2 * p]
                        rb = idx_s[tl * N_SEL + g * SUBLANES + 2 * p + 1]
                        lo, hi = _load_pair(tab_ref, ra, rb)
                        av = jnp.take_along_axis(actg, pair_rows[p], axis=0)
                        k = 2 * (p % 2)
                        acc[k] = acc[k] + av * lo
                        acc[k + 1] = acc[k + 1] + av * hi
                lo = acc[0] + acc[2]
                hi = acc[1] + acc[3]
                lo = lo + pltpu.roll(lo, 4, 0)
                hi = hi + pltpu.roll(hi, 4, 0)
                o_ref[half * PEER_SB + tl] = jnp.where(top, lo, hi)

    _with_index_buffers(idx_hbm, idx_a, idx_b, sem,
                        lambda idx_s: process(idx_s, 0), lambda idx_s: process(idx_s, 1))


def _table_spec():
    return pl.BlockSpec((N_EXPERTS * ROWS_PER_EXPERT, LANES), lambda i: (0, 0),
                        pipeline_mode=pl.Buffered(1))


def _peer_idx_scratch():
    return [pltpu.SMEM((PEER_SB * N_SEL,), jnp.int32), pltpu.SMEM((PEER_SB * N_SEL,), jnp.int32)]


def _peer_u(n, idx, x3, gate3, tab):
    tb = 2 * PEER_SB
    return pl.pallas_call(
        _peer_u_body,
        grid=(n // tb,),
        in_specs=[
            pl.BlockSpec(memory_space=pl.ANY),
            pl.BlockSpec((tb, SUBLANES, LANES), lambda i: (i, 0, 0)),
            pl.BlockSpec((tb // SUBLANES, SUBLANES, LANES), lambda i: (i, 0, 0)),
            _table_spec(),
        ],
        out_specs=pl.BlockSpec((tb // SUBLANES, SUBLANES, LANES), lambda i: (i, 0, 0)),
        out_shape=jax.ShapeDtypeStruct((n // SUBLANES, SUBLANES, LANES), F32),
        scratch_shapes=_peer_idx_scratch() + [
            pltpu.VMEM((tb // SUBLANES * N_SEL, SUBLANES, LANES), F32),
            pltpu.SemaphoreType.DMA((2,))],
        compiler_params=_cparams(("arbitrary",)),
        name="peer_u",
    )(idx.reshape(-1), x3, gate3, tab)


def _peer_v(n, idx, act3, tab):
    tb = 2 * PEER_SB
    return pl.pallas_call(
        _peer_v_body,
        grid=(n // tb,),
        in_specs=[
            pl.BlockSpec(memory_space=pl.ANY),
            pl.BlockSpec((tb // SUBLANES, SUBLANES, LANES), lambda i: (i, 0, 0)),
            _table_spec(),
        ],
        out_specs=pl.BlockSpec((tb, SUBLANES, LANES), lambda i: (i, 0, 0)),
        out_shape=jax.ShapeDtypeStruct((n, SUBLANES, LANES), F32),
        scratch_shapes=_peer_idx_scratch() + [pltpu.SemaphoreType.DMA((2,))],
        compiler_params=_cparams(("arbitrary",)),
        name="peer_v",
    )(idx.reshape(-1), act3, tab)


def _final_body(x_ref, peer_ref, g2_ref, g_ref, o_ref):
    x = x_ref[...] + g2_ref[0] * _tile_rows_load(peer_ref)
    o_ref[...] = _rms(x) * g_ref[...]


def _final(x, peer, g2, g):
    row = lambda i: (i, 0)
    return pl.pallas_call(
        _final_body,
        grid=(NB_LAT,),
        in_specs=[
            pl.BlockSpec((TM, D_MODEL), row),
            pl.BlockSpec((TM * SUBLANES, LANES), row),
            pl.BlockSpec((1, 1, D_MODEL), lambda i: (_bid(i), 0, 0)),
            pl.BlockSpec((1, D_MODEL), lambda i: (0, 0)),
        ],
        out_specs=pl.BlockSpec((TM, D_MODEL), row),
        out_shape=jax.ShapeDtypeStruct((T_LAT, D_MODEL), F32),
        compiler_params=_cparams(("parallel",)),
        name="final_norm",
    )(x, peer, g2, g)


def _deinterleave(width):
    return np.concatenate([np.arange(0, width, 2), np.arange(1, width, 2)])


def _rope_tables():
    pos = np.arange(SEQ)
    rowp = jnp.asarray(pos // GRID_W, F32)
    colp = jnp.asarray(pos % GRID_W, F32)

    def angles(dim):
        quarter = dim // 4
        inv = ROPE_BASE ** (-jnp.arange(quarter, dtype=F32) / quarter)
        return jnp.concatenate([rowp[:, None] * inv, colp[:, None] * inv], axis=-1)

    am = angles(MLA_ROPE_DIM)
    ad = angles(DIFF_HEAD_DIM)
    cm, sm_ = jnp.cos(am), jnp.sin(am)
    cd, sd = jnp.cos(ad), jnp.sin(ad)
    one = lambda w: jnp.ones((SEQ, w), F32)
    zero = lambda w: jnp.zeros((SEQ, w), F32)
    t_cm = jnp.concatenate([one(64), cm, cm, one(32)], axis=1)
    t_spm = jnp.concatenate([zero(80), sm_, zero(32)], axis=1)
    t_smm = jnp.concatenate([zero(64), -sm_, zero(48)], axis=1)
    t_cd = jnp.concatenate([cd, cd, cd, cd], axis=1)
    t_spd = jnp.concatenate([zero(32), sd, zero(32), sd], axis=1)
    t_smd = jnp.concatenate([-sd, zero(32), -sd, zero(32)], axis=1)
    lat = jnp.concatenate([t_cm, t_spm, t_smm, t_cd, t_spd, t_smd], axis=1)
    ident = jnp.concatenate([jnp.ones((CTX_LEN, 128), F32), jnp.zeros((CTX_LEN, 256), F32),
                             jnp.ones((CTX_LEN, 128), F32), jnp.zeros((CTX_LEN, 256), F32)], axis=1)
    return jnp.concatenate([lat, ident], axis=0)


def _prep_layer_weights(w_in, wq_up, wq_rope, wk_up, wv_up, w_out, peer_wq, peer_subkeys):
    p32 = _deinterleave(MLA_ROPE_DIM)
    p64 = _deinterleave(DIFF_HEAD_DIM)
    z = lambda w: jnp.zeros((D_MODEL, w), F32)
    kr = w_in[:, 640:672][:, p32]
    qd = w_in[:, 672:1184].reshape(D_MODEL, 8, 64)[:, :, p64].reshape(D_MODEL, 512)
    kd = w_in[:, 1184:1696].reshape(D_MODEL, 8, 64)[:, :, p64].reshape(D_MODEL, 512)
    win = jnp.concatenate([w_in[:, 0:640], z(64), kr, z(32), qd, kd, w_in[:, 1696:2208]], axis=1)
    qn = wq_up.reshape(MLA_Q_RANK, MLA_HEADS, MLA_HEAD_DIM)
    qr = wq_rope.reshape(MLA_Q_RANK, MLA_HEADS, MLA_ROPE_DIM)[:, :, p32]
    wq = jnp.concatenate([qn, qr, jnp.zeros((MLA_Q_RANK, MLA_HEADS, 32), F32)], axis=2)
    kn = wk_up.reshape(MLA_KV_RANK, MLA_HEADS, MLA_HEAD_DIM)
    wk = jnp.concatenate([kn, jnp.zeros((MLA_KV_RANK, MLA_HEADS, 64), F32)], axis=2)
    return dict(
        win=win.astype(BF16),
        wq=wq.reshape(MLA_Q_RANK, MLA_HEADS * LANES).astype(BF16),
        wk=wk.reshape(MLA_KV_RANK, MLA_HEADS * LANES).astype(BF16),
        wv=wv_up.astype(BF16),
        wo_m=w_out[:512].astype(BF16),
        wo_d=w_out[512:].astype(BF16),
        pwq=peer_wq.astype(BF16),
        sk=peer_subkeys.astype(BF16),
    )


def kernel(x, c, ctx, c_ctx, norm_attn_g, norm_ffn_g, w_ada, b_ada, w_in, mla_q_norm_g, mla_wq_up, mla_wq_rope, mla_kv_norm_g, mla_wk_up, mla_wv_up, diff_lambda, diff_subnorm_g, w_out, peer_wq, peer_subkeys, peer_u, peer_v, final_norm_g):
    xs = jnp.concatenate([x.reshape(T_LAT, D_MODEL), ctx.reshape(T_CTX, D_MODEL)], axis=0)
    cc = jnp.concatenate([c, c_ctx[None, :], jnp.zeros((MOD_ROWS - BATCH - 1, D_MODEL), F32)], axis=0)
    mod = _modulation(cc, w_ada, b_ada)
    lam_inits = [0.8 - 0.6 * math.exp(-0.3 * l) for l in range(DEPTH)]
    lam_all = _diff_lambda(diff_lambda, jnp.broadcast_to(jnp.asarray(lam_inits, F32)[:, None], (DEPTH, LANES)))
    tab = _rope_tables()

    peer_out = None
    g2_prev = None
    for l in range(DEPTH):
        last = l == DEPTH - 1
        w = _prep_layer_weights(w_in[l], mla_wq_up[l], mla_wq_rope[l], mla_wk_up[l], mla_wv_up[l],
                                w_out[l], peer_wq[l], peer_subkeys[l])
        m = mod[l].reshape(MOD_ROWS, N_MOD, 1, D_MODEL)
        sh1, sc1, g1, sh2, sc2, g2 = (m[:, j] for j in range(N_MOD))
        xs, (qm, km, vm, qd, kd, vd) = _qkv(
            xs, peer_out, g2_prev, norm_attn_g[l][None, :], sc1, sh1, w["win"],
            mla_q_norm_g[l][None, :], mla_kv_norm_g[l][None, :], w["wq"], w["wk"], w["wv"], tab)
        diff_args = dict(lam=lam_all[l][None, :], g=diff_subnorm_g[l][None, :],
                         lam_scale=1.0 - lam_inits[l])
        om = _attention("mla", qm, km, vm, False)
        od = _attention("diff", qd, kd, vd, False, **diff_args)
        if not last:
            om = jnp.concatenate([om, _attention("mla", qm, km, vm, True)], axis=0)
            od = jnp.concatenate([od, _attention("diff", qd, kd, vd, True, **diff_args)], axis=0)
        n_blocks = NB_LAT if last else NB_ALL
        n = n_blocks * TM
        xs, h2, eidx_t, gate_t = _mid(n_blocks, xs, om, od, w["wo_m"], w["wo_d"], g1,
                                      norm_ffn_g[l][None, :], sc2, sh2, w["pwq"], w["sk"])
        idx = eidx_t.T
        gate3 = gate_t.reshape(16, 8, n // 8, 8).transpose(2, 1, 3, 0).reshape(n // 8, 8, LANES)
        act3 = _peer_u(n, idx, h2.reshape(n, SUBLANES, LANES), gate3, _pack_table(peer_u, l))
        peer_out = _peer_v(n, idx, act3, _pack_table(peer_v, l)).reshape(n * SUBLANES, LANES)
        g2_prev = g2
    out = _final(xs, peer_out, g2_prev, final_norm_g[None, :])
    return out.reshape(BATCH, SEQ, D_MODEL)
```

```python
import functools
import math

import jax
import jax.numpy as jnp
import numpy as np
from jax import lax
from jax.experimental import pallas as pl
from jax.experimental.pallas import tpu as pltpu

F32 = jnp.float32
BF16 = jnp.bfloat16

D_MODEL = 1024
BATCH = 8
SEQ = 4096
DEPTH = 4
GRID_W = 64
CTX_LEN = 256
N_MOD = 6
NORM_EPS = 1e-6
ROPE_BASE = 10000.0
MLA_HEADS = 8
MLA_HEAD_DIM = 64
MLA_ROPE_DIM = 32
MLA_Q_RANK = 384
MLA_KV_RANK = 256
DIFF_HEADS = 4
DIFF_HEAD_DIM = 64
PEER_HEADS = 8
PEER_TOPK = 16
N_KEYS = 128
N_EXPERTS = N_KEYS * N_KEYS

T_LAT = BATCH * SEQ
T_CTX = BATCH * CTX_LEN
TT = T_LAT + T_CTX
TM = 256
NB_LAT = T_LAT // TM
NB_ALL = TT // TM
BLK_PER_BATCH = SEQ // TM
MOD_ROWS = 16
PROJ_PAD = 2304
TQ = 256
TK = 512
PEER_SB = 16
LANES = 128
SUBLANES = 8
VMEM_LIMIT = 56 * 1024 * 1024
LOG2_E = math.log2(math.e)


def _cparams(sem, vmem=VMEM_LIMIT):
    return pltpu.CompilerParams(dimension_semantics=sem, vmem_limit_bytes=vmem)


def _rms(x):
    return x * lax.rsqrt(jnp.mean(x * x, axis=-1, keepdims=True) + NORM_EPS)


def _tile_rows_load(ref):
    return jnp.concatenate([ref[pl.ds(s, TM, stride=SUBLANES), :] for s in range(SUBLANES)], axis=1)


def _tile_rows_store(ref, x):
    for s in range(SUBLANES):
        ref[pl.ds(s, TM, stride=SUBLANES), :] = x[:, s * LANES:(s + 1) * LANES]


def _bid(i):
    return jnp.where(i < NB_LAT, i // BLK_PER_BATCH, BATCH)


def _posblk(i):
    return jnp.where(i < NB_LAT, i % BLK_PER_BATCH, BLK_PER_BATCH)


def _mod_body(cc_ref, w_ref, b_ref, o_ref):
    cc = cc_ref[...]
    s = cc / (1.0 + jnp.exp(-cc))
    o_ref[0] = jnp.dot(s, w_ref[0], precision=lax.Precision.HIGHEST,
                       preferred_element_type=F32) + b_ref[0]


def _modulation(cc, w_ada, b_ada):
    nj = N_MOD
    return pl.pallas_call(
        _mod_body,
        grid=(DEPTH, nj),
        in_specs=[
            pl.BlockSpec((MOD_ROWS, D_MODEL), lambda l, j: (0, 0)),
            pl.BlockSpec((1, D_MODEL, D_MODEL), lambda l, j: (l, 0, j)),
            pl.BlockSpec((1, 1, D_MODEL), lambda l, j: (l, 0, j)),
        ],
        out_specs=pl.BlockSpec((1, MOD_ROWS, D_MODEL), lambda l, j: (l, 0, j)),
        out_shape=jax.ShapeDtypeStruct((DEPTH, MOD_ROWS, N_MOD * D_MODEL), F32),
        compiler_params=_cparams(("arbitrary", "arbitrary")),
        name="modulation",
    )(cc, w_ada, b_ada.reshape(DEPTH, 1, N_MOD * D_MODEL))


def _lam_body(d0_ref, d1_ref, d2_ref, d3_ref, li_ref, o_ref):
    a = jnp.sum(d0_ref[...] * d1_ref[...], axis=-1, keepdims=True)
    b = jnp.sum(d2_ref[...] * d3_ref[...], axis=-1, keepdims=True)
    o_ref[...] = jnp.exp(a) - jnp.exp(b) + li_ref[...]


def _diff_lambda(diff_lambda, lam_init):
    dl = diff_lambda.astype(F32)
    return pl.pallas_call(
        _lam_body,
        out_shape=jax.ShapeDtypeStruct((DEPTH, LANES), F32),
        name="diff_lambda",
    )(dl[:, 0], dl[:, 1], dl[:, 2], dl[:, 3], lam_init)


def _rope(xb, c, sp, sm, shift):
    return xb * c + pltpu.roll(xb, shift, 1) * sp + pltpu.roll(xb, LANES - shift, 1) * sm


def _qkv_body(has_peer, *refs):
    if has_peer:
        x_ref, peer_ref, g2_ref = refs[:3]
        refs = refs[3:]
    else:
        x_ref = refs[0]
        refs = refs[1:]
    (ng_ref, sc_ref, sh_ref, win_ref, gq_ref, gkv_ref, wq_ref, wk_ref, wv_ref, tab_ref) = refs[:10]
    outs = refs[10:]
    if has_peer:
        xo_ref, outs = outs[0], outs[1:]
    qm_ref, km_ref, vm_ref, qd_ref, kd_ref, vd_ref = outs

    x = x_ref[...]
    if has_peer:
        x = x + g2_ref[0] * _tile_rows_load(peer_ref)
        xo_ref[...] = x
    h = _rms(x) * ng_ref[...] * (1.0 + sc_ref[0]) + sh_ref[0]
    proj = jnp.dot(h.astype(BF16), win_ref[...], preferred_element_type=F32)
    cq = _rms(proj[:, 0:384]) * gq_ref[...]
    ckv = _rms(proj[:, 384:640]) * gkv_ref[...]
    kr = proj[:, 640:768]
    q = jnp.dot(cq.astype(BF16), wq_ref[...], preferred_element_type=F32)
    ckv16 = ckv.astype(BF16)
    k = jnp.dot(ckv16, wk_ref[...], preferred_element_type=F32)
    v = jnp.dot(ckv16, wv_ref[...], preferred_element_type=F32)
    tab = tab_ref[...]
    cm, spm, smm = tab[:, 0:128], tab[:, 128:256], tab[:, 256:384]
    cd, spd, smd = tab[:, 384:512], tab[:, 512:640], tab[:, 640:768]
    kr_rot = _rope(kr, cm, spm, smm, MLA_ROPE_DIM // 2)
    scale_m = LOG2_E * (MLA_HEAD_DIM + MLA_ROPE_DIM) ** -0.5
    for hh in range(MLA_HEADS):
        sl = slice(hh * LANES, (hh + 1) * LANES)
        qm_ref[:, sl] = (_rope(q[:, sl], cm, spm, smm, MLA_ROPE_DIM // 2) * scale_m).astype(BF16)
        km_ref[:, sl] = (k[:, sl] + kr_rot).astype(BF16)
    vm_ref[...] = v.T.astype(BF16)
    scale_d = LOG2_E * DIFF_HEAD_DIM ** -0.5
    for hh in range(DIFF_HEADS):
        sl = slice(hh * LANES, (hh + 1) * LANES)
        qd = proj[:, 768 + hh * LANES:768 + (hh + 1) * LANES]
        kd = proj[:, 1280 + hh * LANES:1280 + (hh + 1) * LANES]
        qd_ref[:, sl] = (_rope(qd, cd, spd, smd, DIFF_HEAD_DIM // 2) * scale_d).astype(BF16)
        kd_ref[:, sl] = _rope(kd, cd, spd, smd, DIFF_HEAD_DIM // 2).astype(BF16)
    vd_ref[...] = proj[:, 1792:2304].T.astype(BF16)


def _qkv(x, peer, g2, ng, sc, sh, win, gq, gkv, wq, wk, wv, tab):
    has_peer = peer is not None
    row = lambda i: (i, 0)
    modrow = lambda i: (_bid(i), 0, 0)
    const = lambda i: (0, 0)
    in_specs = [pl.BlockSpec((TM, D_MODEL), row)]
    args = [x]
    if has_peer:
        in_specs += [pl.BlockSpec((TM * SUBLANES, LANES), row), pl.BlockSpec((1, 1, D_MODEL), modrow)]
        args += [peer, g2]
    in_specs += [
        pl.BlockSpec((1, D_MODEL), const),
        pl.BlockSpec((1, 1, D_MODEL), modrow),
        pl.BlockSpec((1, 1, D_MODEL), modrow),
        pl.BlockSpec((D_MODEL, PROJ_PAD), const),
        pl.BlockSpec((1, MLA_Q_RANK), const),
        pl.BlockSpec((1, MLA_KV_RANK), const),
        pl.BlockSpec((MLA_Q_RANK, MLA_HEADS * LANES), const),
        pl.BlockSpec((MLA_KV_RANK, MLA_HEADS * LANES), const),
        pl.BlockSpec((MLA_KV_RANK, 512), const),
        pl.BlockSpec((TM, 768), lambda i: (_posblk(i), 0)),
    ]
    args += [ng, sc, sh, win, gq, gkv, wq, wk, wv, tab]
    out_specs, out_shape = [], []
    if has_peer:
        out_specs.append(pl.BlockSpec((TM, D_MODEL), row))
        out_shape.append(jax.ShapeDtypeStruct((TT, D_MODEL), F32))
    for width, transposed in ((1024, False), (1024, False), (512, True),
                              (512, False), (512, False), (512, True)):
        if transposed:
            out_specs.append(pl.BlockSpec((width, TM), lambda i: (0, i)))
            out_shape.append(jax.ShapeDtypeStruct((width, TT), BF16))
        else:
            out_specs.append(pl.BlockSpec((TM, width), row))
            out_shape.append(jax.ShapeDtypeStruct((TT, width), BF16))
    res = pl.pallas_call(
        functools.partial(_qkv_body, has_peer),
        grid=(NB_ALL,),
        in_specs=in_specs,
        out_specs=out_specs,
        out_shape=out_shape,
        compiler_params=_cparams(("parallel",)),
        name="qkv",
    )(*args)
    if has_peer:
        return res[0], res[1:]
    return x, res


def _flash_pair(tq, qs, ksls, vsls, kc_ref, kl_ref, vtc_ref, vtl_ref, n_pairs, sa_scr, sb_scr):
    n_lat = SEQ // TK

    def col_reduce(parts, op, final):
        while len(parts) > 1:
            parts = [op(parts[i], parts[i + 1]) for i in range(0, len(parts), 2)]
        return final(parts[0], axis=0, keepdims=True)

    def row_groups(x):
        return [x[i:i + SUBLANES] for i in range(0, x.shape[0], SUBLANES)]

    def scores(k_of):
        return [lax.dot_general(k_of(ksl), q, (((1,), (1,)), ((), ())), preferred_element_type=F32)
                for q, ksl in zip(qs, ksls)]

    def softmax_pv(read_s, vt_of, carries):
        mid = []
        for h, (m, l, acc) in enumerate(carries):
            m_new = jnp.maximum(m, col_reduce(row_groups(read_s(h)), jnp.maximum, jnp.max))
            alpha = jnp.exp2(m - m_new)
            p = jnp.exp2(read_s(h) - m_new)
            l = alpha * l + col_reduce(row_groups(p), jnp.add, jnp.sum)
            mid.append((m_new, l, alpha * acc, p.astype(BF16)))
        return tuple((m_new, l, acc + jnp.dot(vt_of(vsl), p, preferred_element_type=F32))
                     for (m_new, l, acc, p), vsl in zip(mid, vsls))

    def k_lat(c):
        return lambda ksl: kl_ref[c * TK:(c + 1) * TK, ksl]

    def vt_lat(c):
        return lambda vsl: vtl_ref[vsl, c * TK:(c + 1) * TK]

    def stage(scr, sc):
        for h, x in enumerate(sc):
            scr[h] = x

    init = tuple((jnp.full((1, tq), -jnp.inf, F32), jnp.zeros((1, tq), F32),
                  jnp.zeros((vsl.stop - vsl.start, tq), F32)) for vsl in vsls)
    ctx_scores = scores(lambda ksl: kc_ref[:, ksl])
    if n_pairs:
        stage(sa_scr, scores(k_lat(0)))
    carries = softmax_pv(lambda h: ctx_scores[h], lambda vsl: vtc_ref[vsl, :], init)

    def pair(j, carries):
        c = 2 * j
        stage(sb_scr, scores(k_lat(c + 1)))
        carries = softmax_pv(lambda h: sa_scr[h], vt_lat(c), carries)
        if c + 2 < n_lat:
            stage(sa_scr, scores(k_lat(c + 2)))
        return softmax_pv(lambda h: sb_scr[h], vt_lat(c + 1), carries)

    for j in range(n_pairs):
        carries = pair(j, carries)
    return [acc / l for (m, l, acc) in carries]


def _attn_mla_body(tq, n_pairs, *refs):
    if n_pairs:
        q_ref, kc_ref, kl_ref, vtc_ref, vtl_ref, o_ref, sa_scr, sb_scr = refs
    else:
        q_ref, kc_ref, vtc_ref, o_ref = refs
        kl_ref = vtl_ref = sa_scr = sb_scr = None
    qs = [q_ref[:, 0:LANES], q_ref[:, LANES:2 * LANES]]
    ksls = [slice(0, LANES), slice(LANES, 2 * LANES)]
    vsls = [slice(0, MLA_HEAD_DIM), slice(MLA_HEAD_DIM, 2 * MLA_HEAD_DIM)]
    ot0, ot1 = _flash_pair(tq, qs, ksls, vsls, kc_ref, kl_ref, vtc_ref, vtl_ref, n_pairs, sa_scr, sb_scr)
    o_ref[...] = jnp.concatenate([ot0, ot1], axis=0).T.astype(o_ref.dtype)


def _attn_diff_body(tq, n_pairs, lam_scale, *refs):
    if n_pairs:
        q_ref, kc_ref, kl_ref, vtc_ref, vtl_ref, lam_ref, g_ref, o_ref, sa_scr, sb_scr = refs
    else:
        q_ref, kc_ref, vtc_ref, lam_ref, g_ref, o_ref = refs
        kl_ref = vtl_ref = sa_scr = sb_scr = None
    lane = lax.broadcasted_iota(jnp.int32, (tq, LANES), 1)
    q = q_ref[...]
    zero = jnp.zeros_like(q)
    qs = [jnp.where(lane < DIFF_HEAD_DIM, q, zero), jnp.where(lane < DIFF_HEAD_DIM, zero, q)]
    ksls = [slice(0, LANES), slice(0, LANES)]
    vsls = [slice(0, LANES), slice(0, LANES)]
    ot0, ot1 = _flash_pair(tq, qs, ksls, vsls, kc_ref, kl_ref, vtc_ref, vtl_ref, n_pairs, sa_scr, sb_scr)
    o = ot0.T - lam_ref[...] * ot1.T
    o_ref[...] = (_rms(o) * g_ref[...] * lam_scale).astype(o_ref.dtype)


def _attention(kind, q, k, vt, ctx_queries, lam=None, g=None, lam_scale=None):
    qw = 2 * LANES if kind == "mla" else LANES
    ctx0 = T_LAT // CTX_LEN
    if ctx_queries:
        tq, n_pairs, steps, rows = CTX_LEN, 0, 1, T_CTX
        q_map = lambda b, p, i: (ctx0 + b, p)
        o_map = lambda b, p, i: (b, p)
    else:
        tq, n_pairs, steps, rows = TQ, SEQ // (2 * TK), SEQ // TQ, T_LAT
        q_map = lambda b, p, i: (b * steps + i, p)
        o_map = q_map
    kc_spec = pl.BlockSpec((CTX_LEN, qw), lambda b, p, i: (ctx0 + b, p))
    vtc_spec = pl.BlockSpec((LANES, CTX_LEN), lambda b, p, i: (p, ctx0 + b))
    in_specs = [pl.BlockSpec((tq, qw), q_map), kc_spec]
    args = [q, k]
    if n_pairs:
        in_specs += [pl.BlockSpec((SEQ, qw), lambda b, p, i: (b, p)), vtc_spec,
                     pl.BlockSpec((LANES, SEQ), lambda b, p, i: (p, b))]
        args += [k, vt, vt]
    else:
        in_specs += [vtc_spec]
        args += [vt]
    if kind == "mla":
        body = functools.partial(_attn_mla_body, tq, n_pairs)
    else:
        body = functools.partial(_attn_diff_body, tq, n_pairs, lam_scale)
        in_specs += [pl.BlockSpec((1, LANES), lambda b, p, i: (0, 0)),
                     pl.BlockSpec((1, LANES), lambda b, p, i: (0, 0))]
        args += [lam, g]
    scratch = [pltpu.VMEM((2, TK, tq), F32), pltpu.VMEM((2, TK, tq), F32)] if n_pairs else []
    return pl.pallas_call(
        body,
        grid=(BATCH, 4, steps),
        in_specs=in_specs,
        out_specs=pl.BlockSpec((tq, LANES), o_map),
        out_shape=jax.ShapeDtypeStruct((rows, 512), BF16),
        scratch_shapes=scratch,
        compiler_params=_cparams(("parallel", "parallel", "arbitrary")),
        name="attn_" + kind + ("_ctx" if ctx_queries else ""),
    )(*args)


HEADS_PER_TRIP = 4


def _top16(s, payload=None):
    n_rows = s.shape[0]
    rowf = lax.broadcasted_iota(jnp.int32, s.shape, 0).astype(F32)
    slot = lax.broadcasted_iota(jnp.int32, (PEER_TOPK, s.shape[1]), 0)
    vals = jnp.zeros((PEER_TOPK, s.shape[1]), F32)
    picks = jnp.zeros((PEER_TOPK, s.shape[1]), F32)
    for r in range(PEER_TOPK):
        m = jnp.max(s, axis=0, keepdims=True)
        am = jnp.min(jnp.where(s == m, rowf, float(n_rows)), axis=0, keepdims=True)
        hit = rowf == am
        pick = am if payload is None else jnp.max(jnp.where(hit, payload, -1.0), axis=0, keepdims=True)
        vals = jnp.where(slot == r, m, vals)
        picks = jnp.where(slot == r, pick, picks)
        s = jnp.where(hit, -jnp.inf, s)
    return vals, picks


def _staircase(a16, b16, combine, pad):
    tm = a16.shape[1]
    sub = lax.broadcasted_iota(jnp.int32, (SUBLANES, tm), 0)
    a_lo, a_hi = a16[0:SUBLANES], a16[SUBLANES:]
    b_lo, b_hi = b16[0:SUBLANES], b16[SUBLANES:]
    row = lambda x, r: jnp.broadcast_to(x[r:r + 1], (SUBLANES, tm))
    take = lambda x, idx: jnp.take_along_axis(x, idx, axis=0)
    a3 = jnp.where(sub < 5, 2, 3)
    b3 = jnp.where(sub < 5, sub, sub - 5)
    a4 = jnp.where(sub < 1, 3, jnp.where(sub < 4, 4, jnp.where(sub < 6, 5, 6)))
    b4 = jnp.where(sub < 1, 3, jnp.where(sub < 4, sub - 1, jnp.where(sub < 6, sub - 4, sub - 6)))
    pieces = [
        combine(row(a_lo, 0), b_lo),
        combine(row(a_lo, 0), b_hi),
        combine(row(a_lo, 1), b_lo),
        combine(take(a_lo, a3), take(b_lo, b3)),
        combine(take(a_lo, a4), take(b_lo, b4)),
        jnp.where(sub < 2, combine(row(a_lo, 7), b_lo), pad),
        combine(a_hi, row(b_lo, 0)),
    ]
    return jnp.concatenate(pieces, axis=0)


def _mid_body(x_ref, om_ref, od_ref, wo_m_ref, wo_d_ref, g1_ref, ng_ref, sc_ref, sh_ref,
              wq_ref, sk_ref, xo_ref, h_ref, eidx_ref, gate_ref, q_scr):
    y = (jnp.dot(om_ref[...], wo_m_ref[...], preferred_element_type=F32)
         + jnp.dot(od_ref[...], wo_d_ref[...], preferred_element_type=F32))
    x = x_ref[...] + g1_ref[0] * y
    xo_ref[...] = x
    h = _rms(x) * ng_ref[...] * (1.0 + sc_ref[0]) + sh_ref[0]
    _tile_rows_store(h_ref, h)
    q_scr[...] = jnp.dot(h.astype(BF16), wq_ref[...], preferred_element_type=F32).astype(BF16)

    def head(hh):
        sv, si = [], []
        for c in range(2):
            c0 = pl.multiple_of(hh * 2 * N_KEYS + c * N_KEYS, N_KEYS)
            qh = q_scr[:, pl.ds(c0, N_KEYS)]
            s = lax.dot_general(sk_ref[c], qh, (((1,), (1,)), ((), ())), preferred_element_type=F32)
            vals, keys = _top16(s)
            sv.append(vals)
            si.append(keys)
        cand_s = _staircase(sv[0], sv[1], lambda a, b: a + b, -jnp.inf)
        cand_e = _staircase(si[0], si[1], lambda a, b: a * float(N_KEYS) + b, -1.0)
        top_s, top_e = _top16(cand_s, cand_e)
        ex = jnp.exp(top_s - jnp.max(top_s, axis=0, keepdims=True))
        gate = ex / jnp.sum(ex, axis=0, keepdims=True)
        r0 = pl.multiple_of(hh * PEER_TOPK, PEER_TOPK)
        eidx_ref[pl.ds(r0, PEER_TOPK), :] = top_e.astype(jnp.int32) * ROWS_PER_EXPERT
        gate_ref[pl.ds(r0, PEER_TOPK), :] = gate

    def head_group(j, _):
        for i in range(HEADS_PER_TRIP):
            head(HEADS_PER_TRIP * j + i)
        return 0

    lax.fori_loop(0, PEER_HEADS // HEADS_PER_TRIP, head_group, 0)


def _mid(n_blocks, x, om, od, wo_m, wo_d, g1, ng, sc, sh, wq, sk):
    n = n_blocks * TM
    row = lambda i: (i, 0)
    modrow = lambda i: (_bid(i), 0, 0)
    const = lambda i: (0, 0)
    return pl.pallas_call(
        _mid_body,
        grid=(n_blocks,),
        in_specs=[
            pl.BlockSpec((TM, D_MODEL), row),
            pl.BlockSpec((TM, 512), row),
            pl.BlockSpec((TM, 512), row),
            pl.BlockSpec((512, D_MODEL), const),
            pl.BlockSpec((512, D_MODEL), const),
            pl.BlockSpec((1, 1, D_MODEL), modrow),
            pl.BlockSpec((1, D_MODEL), const),
            pl.BlockSpec((1, 1, D_MODEL), modrow),
            pl.BlockSpec((1, 1, D_MODEL), modrow),
            pl.BlockSpec((D_MODEL, PEER_HEADS * 2 * N_KEYS), const),
            pl.BlockSpec((2, N_KEYS, N_KEYS), lambda i: (0, 0, 0)),
        ],
        out_specs=[
            pl.BlockSpec((TM, D_MODEL), row),
            pl.BlockSpec((TM * SUBLANES, LANES), row),
            pl.BlockSpec((PEER_HEADS * PEER_TOPK, TM), lambda i: (0, i)),
            pl.BlockSpec((PEER_HEADS * PEER_TOPK, TM), lambda i: (0, i)),
        ],
        out_shape=[
            jax.ShapeDtypeStruct((n, D_MODEL), F32),
            jax.ShapeDtypeStruct((n * SUBLANES, LANES), F32),
            jax.ShapeDtypeStruct((PEER_HEADS * PEER_TOPK, n), jnp.int32),
            jax.ShapeDtypeStruct((PEER_HEADS * PEER_TOPK, n), F32),
        ],
        scratch_shapes=[pltpu.VMEM((TM, PEER_HEADS * 2 * N_KEYS), BF16)],
        compiler_params=_cparams(("parallel",)),
        name="mid",
    )(x, om, od, wo_m, wo_d, g1, ng, sc, sh, wq, sk)


N_SEL = PEER_HEADS * PEER_TOPK
HALF = D_MODEL // 2
ROWS_PER_EXPERT = HALF // LANES
_BFLY_ORDER = (0, 4, 2, 6, 1, 5, 3, 7)


PACK_BLOCK = 512


def _pack_body(t_ref, o_ref):
    u = pltpu.bitcast(t_ref[0], jnp.uint32)
    r = u + jnp.uint32(0x7FFF) + ((u >> 16) & jnp.uint32(1))
    w = (r[:, :HALF] >> 16) | (r[:, HALF:] & jnp.uint32(0xFFFF0000))
    for s in range(ROWS_PER_EXPERT):
        o_ref[pl.ds(s, PACK_BLOCK, stride=ROWS_PER_EXPERT), :] = w[:, s * LANES:(s + 1) * LANES]


def _pack_table(tabs, layer):
    n = tabs.shape[1]
    return pl.pallas_call(
        _pack_body,
        grid=(n // PACK_BLOCK,),
        in_specs=[pl.BlockSpec((1, PACK_BLOCK, D_MODEL), lambda i: (layer, i, 0))],
        out_specs=pl.BlockSpec((PACK_BLOCK * ROWS_PER_EXPERT, LANES), lambda i: (i, 0)),
        out_shape=jax.ShapeDtypeStruct((n * ROWS_PER_EXPERT, LANES), jnp.uint32),
        compiler_params=_cparams(("parallel",)),
        name="pack_table",
    )(tabs)


def _load_pair(tab_ref, ra, rb):
    w = jnp.concatenate([tab_ref[pl.ds(pl.multiple_of(ra, ROWS_PER_EXPERT), ROWS_PER_EXPERT), :],
                         tab_ref[pl.ds(pl.multiple_of(rb, ROWS_PER_EXPERT), ROWS_PER_EXPERT), :]], axis=0)
    lo = pltpu.bitcast(w << 16, F32)
    hi = pltpu.bitcast(w & jnp.uint32(0xFFFF0000), F32)
    return lo, hi


def _with_index_buffers(idx_hbm, idx_a, idx_b, sem, first, second):
    step = pl.program_id(0)
    words = PEER_SB * N_SEL

    def idx_copy(block, dst, k):
        return pltpu.make_async_copy(idx_hbm.at[pl.ds(block * words, words)], dst, sem.at[k])

    @pl.when(step == 0)
    def _():
        idx_copy(0, idx_a, 0).start()

    idx_copy(2 * step + 1, idx_b, 1).start()
    idx_copy(2 * step, idx_a, 0).wait()
    first(idx_a)

    @pl.when(step + 1 < pl.num_programs(0))
    def _():
        idx_copy(2 * step + 2, idx_a, 0).start()

    idx_copy(2 * step + 1, idx_b, 1).wait()
    second(idx_b)


def _peer_u_body(idx_hbm, x_ref, gate_ref, tab_ref, act_ref, idx_a, idx_b, r_scr, sem):
    sub = lax.broadcasted_iota(jnp.int32, (SUBLANES, LANES), 0)
    lane = lax.broadcasted_iota(jnp.int32, (SUBLANES, LANES), 1)
    top = sub < 4
    n_grp = N_SEL // SUBLANES
    tiles_per_buf = PEER_SB // SUBLANES

    def fold(p, q, width):
        tp = p + pltpu.roll(p, SUBLANES - width, 0)
        tq = q + pltpu.roll(q, SUBLANES - width, 0)
        return jnp.where((sub % (2 * width)) < width, tp, pltpu.roll(tq, width, 0))

    def gather_tile(idx_s, tile, tile_in_buf):
        for tt in range(SUBLANES):
            tl = tile_in_buf * SUBLANES + tt
            xt = x_ref[tile * SUBLANES + tt]
            xsw = pltpu.roll(xt, 4, 0)
            xlo = jnp.where(top, xt, xsw)
            xhi = jnp.where(top, xsw, xt)
            for g in range(n_grp):
                parts = []
                for pr in range(4):
                    ra = idx_s[tl * N_SEL + g * SUBLANES + _BFLY_ORDER[2 * pr]]
                    rb = idx_s[tl * N_SEL + g * SUBLANES + _BFLY_ORDER[2 * pr + 1]]
                    lo, hi = _load_pair(tab_ref, ra, rb)
                    parts.append(lo * xlo + hi * xhi)
                r_scr[tile * N_SEL + tt * n_grp + g] = fold(
                    fold(parts[0], parts[1], 2), fold(parts[2], parts[3], 2), 1)

    def reduce_tile(tile):
        a = jnp.zeros((SUBLANES, LANES), F32)
        for j in range(N_SEL):
            a = jnp.where(lane == j, jnp.sum(r_scr[tile * N_SEL + j], axis=1, keepdims=True), a)
        act_ref[tile] = 0.5 * a * (1.0 + lax.erf(a * (2.0 ** -0.5))) * gate_ref[tile]

    def first(idx_s):
        for k in range(tiles_per_buf):
            if k:
                reduce_tile(k - 1)
            gather_tile(idx_s, k, k)

    def second(idx_s):
        for k in range(tiles_per_buf):
            reduce_tile(tiles_per_buf + k - 1)
            gather_tile(idx_s, tiles_per_buf + k, k)
        reduce_tile(2 * tiles_per_buf - 1)

    _with_index_buffers(idx_hbm, idx_a, idx_b, sem, first, second)


def _peer_v_body(idx_hbm, act_ref, tab_ref, o_ref, idx_a, idx_b, sem):
    sub = lax.broadcasted_iota(jnp.int32, (SUBLANES, LANES), 0)
    lane = lax.broadcasted_iota(jnp.int32, (SUBLANES, LANES), 1)
    top = sub < 4
    n_grp = N_SEL // SUBLANES
    pair_rows = [jnp.where(top, 2 * p, 2 * p + 1) for p in range(4)]

    def process(idx_s, half):
        for tile in range(PEER_SB // SUBLANES):
            a_tile = act_ref[half * (PEER_SB // SUBLANES) + tile]
            for tt in range(SUBLANES):
                tl = tile * SUBLANES + tt
                acc = [jnp.zeros((SUBLANES, LANES), F32) for _ in range(4)]
                for g in range(n_grp):
                    col = jnp.sum(jnp.where(lane == tt * n_grp + g, a_tile, 0.0), axis=1, keepdims=True)
                    actg = jnp.broadcast_to(col, (SUBLANES, LANES))
                    for p in range(4):
                        ra = idx_s[tl * N_SEL + g * SUBLANES + 2 * p]
                        rb = idx_s[tl * N_SEL + g * SUBLANES + 2 * p + 1]
                        lo, hi = _load_pair(tab_ref, ra, rb)
                        av = jnp.take_along_axis(actg, pair_rows[p], axis=0)
                        k = 2 * (p % 2)
                        acc[k] = acc[k] + av * lo
                        acc[k + 1] = acc[k + 1] + av * hi
                lo = acc[0] + acc[2]
                hi = acc[1] + acc[3]
                lo = lo + pltpu.roll(lo, 4, 0)
                hi = hi + pltpu.roll(hi, 4, 0)
                o_ref[half * PEER_SB + tl] = jnp.where(top, lo, hi)

    _with_index_buffers(idx_hbm, idx_a, idx_b, sem,
                        lambda idx_s: process(idx_s, 0), lambda idx_s: process(idx_s, 1))


def _table_spec():
    return pl.BlockSpec((N_EXPERTS * ROWS_PER_EXPERT, LANES), lambda i: (0, 0),
                        pipeline_mode=pl.Buffered(1))


def _peer_idx_scratch():
    return [pltpu.SMEM((PEER_SB * N_SEL,), jnp.int32), pltpu.SMEM((PEER_SB * N_SEL,), jnp.int32)]


def _peer_u(n, idx, x3, gate3, tab):
    tb = 2 * PEER_SB
    return pl.pallas_call(
        _peer_u_body,
        grid=(n // tb,),
        in_specs=[
            pl.BlockSpec(memory_space=pl.ANY),
            pl.BlockSpec((tb, SUBLANES, LANES), lambda i: (i, 0, 0)),
            pl.BlockSpec((tb // SUBLANES, SUBLANES, LANES), lambda i: (i, 0, 0)),
            _table_spec(),
        ],
        out_specs=pl.BlockSpec((tb // SUBLANES, SUBLANES, LANES), lambda i: (i, 0, 0)),
        out_shape=jax.ShapeDtypeStruct((n // SUBLANES, SUBLANES, LANES), F32),
        scratch_shapes=_peer_idx_scratch() + [
            pltpu.VMEM((tb // SUBLANES * N_SEL, SUBLANES, LANES), F32),
            pltpu.SemaphoreType.DMA((2,))],
        compiler_params=_cparams(("arbitrary",)),
        name="peer_u",
    )(idx.reshape(-1), x3, gate3, tab)


def _peer_v(n, idx, act3, tab):
    tb = 2 * PEER_SB
    return pl.pallas_call(
        _peer_v_body,
        grid=(n // tb,),
        in_specs=[
            pl.BlockSpec(memory_space=pl.ANY),
            pl.BlockSpec((tb // SUBLANES, SUBLANES, LANES), lambda i: (i, 0, 0)),
            _table_spec(),
        ],
        out_specs=pl.BlockSpec((tb, SUBLANES, LANES), lambda i: (i, 0, 0)),
        out_shape=jax.ShapeDtypeStruct((n, SUBLANES, LANES), F32),
        scratch_shapes=_peer_idx_scratch() + [pltpu.SemaphoreType.DMA((2,))],
        compiler_params=_cparams(("arbitrary",)),
        name="peer_v",
    )(idx.reshape(-1), act3, tab)


def _final_body(x_ref, peer_ref, g2_ref, g_ref, o_ref):
    x = x_ref[...] + g2_ref[0] * _tile_rows_load(peer_ref)
    o_ref[...] = _rms(x) * g_ref[...]


def _final(x, peer, g2, g):
    row = lambda i: (i, 0)
    return pl.pallas_call(
        _final_body,
        grid=(NB_LAT,),
        in_specs=[
            pl.BlockSpec((TM, D_MODEL), row),
            pl.BlockSpec((TM * SUBLANES, LANES), row),
            pl.BlockSpec((1, 1, D_MODEL), lambda i: (_bid(i), 0, 0)),
            pl.BlockSpec((1, D_MODEL), lambda i: (0, 0)),
        ],
        out_specs=pl.BlockSpec((TM, D_MODEL), row),
        out_shape=jax.ShapeDtypeStruct((T_LAT, D_MODEL), F32),
        compiler_params=_cparams(("parallel",)),
        name="final_norm",
    )(x, peer, g2, g)


def _deinterleave(width):
    return np.concatenate([np.arange(0, width, 2), np.arange(1, width, 2)])


def _rope_tables():
    pos = np.arange(SEQ)
    rowp = jnp.asarray(pos // GRID_W, F32)
    colp = jnp.asarray(pos % GRID_W, F32)

    def angles(dim):
        quarter = dim // 4
        inv = ROPE_BASE ** (-jnp.arange(quarter, dtype=F32) / quarter)
        return jnp.concatenate([rowp[:, None] * inv, colp[:, None] * inv], axis=-1)

    am = angles(MLA_ROPE_DIM)
    ad = angles(DIFF_HEAD_DIM)
    cm, sm_ = jnp.cos(am), jnp.sin(am)
    cd, sd = jnp.cos(ad), jnp.sin(ad)
    one = lambda w: jnp.ones((SEQ, w), F32)
    zero = lambda w: jnp.zeros((SEQ, w), F32)
    t_cm = jnp.concatenate([one(64), cm, cm, one(32)], axis=1)
    t_spm = jnp.concatenate([zero(80), sm_, zero(32)], axis=1)
    t_smm = jnp.concatenate([zero(64), -sm_, zero(48)], axis=1)
    t_cd = jnp.concatenate([cd, cd, cd, cd], axis=1)
    t_spd = jnp.concatenate([zero(32), sd, zero(32), sd], axis=1)
    t_smd = jnp.concatenate([-sd, zero(32), -sd, zero(32)], axis=1)
    lat = jnp.concatenate([t_cm, t_spm, t_smm, t_cd, t_spd, t_smd], axis=1)
    ident = jnp.concatenate([jnp.ones((CTX_LEN, 128), F32), jnp.zeros((CTX_LEN, 256), F32),
                             jnp.ones((CTX_LEN, 128), F32), jnp.zeros((CTX_LEN, 256), F32)], axis=1)
    return jnp.concatenate([lat, ident], axis=0)


def _prep_layer_weights(w_in, wq_up, wq_rope, wk_up, wv_up, w_out, peer_wq, peer_subkeys):
    p32 = _deinterleave(MLA_ROPE_DIM)
    p64 = _deinterleave(DIFF_HEAD_DIM)
    z = lambda w: jnp.zeros((D_MODEL, w), F32)
    kr = w_in[:, 640:672][:, p32]
    qd = w_in[:, 672:1184].reshape(D_MODEL, 8, 64)[:, :, p64].reshape(D_MODEL, 512)
    kd = w_in[:, 1184:1696].reshape(D_MODEL, 8, 64)[:, :, p64].reshape(D_MODEL, 512)
    win = jnp.concatenate([w_in[:, 0:640], z(64), kr, z(32), qd, kd, w_in[:, 1696:2208]], axis=1)
    qn = wq_up.reshape(MLA_Q_RANK, MLA_HEADS, MLA_HEAD_DIM)
    qr = wq_rope.reshape(MLA_Q_RANK, MLA_HEADS, MLA_ROPE_DIM)[:, :, p32]
    wq = jnp.concatenate([qn, qr, jnp.zeros((MLA_Q_RANK, MLA_HEADS, 32), F32)], axis=2)
    kn = wk_up.reshape(MLA_KV_RANK, MLA_HEADS, MLA_HEAD_DIM)
    wk = jnp.concatenate([kn, jnp.zeros((MLA_KV_RANK, MLA_HEADS, 64), F32)], axis=2)
    return dict(
        win=win.astype(BF16),
        wq=wq.reshape(MLA_Q_RANK, MLA_HEADS * LANES).astype(BF16),
        wk=wk.reshape(MLA_KV_RANK, MLA_HEADS * LANES).astype(BF16),
        wv=wv_up.astype(BF16),
        wo_m=w_out[:512].astype(BF16),
        wo_d=w_out[512:].astype(BF16),
        pwq=peer_wq.astype(BF16),
        sk=peer_subkeys.astype(BF16),
    )


def kernel(x, c, ctx, c_ctx, norm_attn_g, norm_ffn_g, w_ada, b_ada, w_in, mla_q_norm_g, mla_wq_up, mla_wq_rope, mla_kv_norm_g, mla_wk_up, mla_wv_up, diff_lambda, diff_subnorm_g, w_out, peer_wq, peer_subkeys, peer_u, peer_v, final_norm_g):
    xs = jnp.concatenate([x.reshape(T_LAT, D_MODEL), ctx.reshape(T_CTX, D_MODEL)], axis=0)
    cc = jnp.concatenate([c, c_ctx[None, :], jnp.zeros((MOD_ROWS - BATCH - 1, D_MODEL), F32)], axis=0)
    mod = _modulation(cc, w_ada, b_ada)
    lam_inits = [0.8 - 0.6 * math.exp(-0.3 * l) for l in range(DEPTH)]
    lam_all = _diff_lambda(diff_lambda, jnp.broadcast_to(jnp.asarray(lam_inits, F32)[:, None], (DEPTH, LANES)))
    tab = _rope_tables()

    peer_out = None
    g2_prev = None
    for l in range(DEPTH):
        last = l == DEPTH - 1
        w = _prep_layer_weights(w_in[l], mla_wq_up[l], mla_wq_rope[l], mla_wk_up[l], mla_wv_up[l],
                                w_out[l], peer_wq[l], peer_subkeys[l])
        m = mod[l].reshape(MOD_ROWS, N_MOD, 1, D_MODEL)
        sh1, sc1, g1, sh2, sc2, g2 = (m[:, j] for j in range(N_MOD))
        xs, (qm, km, vm, qd, kd, vd) = _qkv(
            xs, peer_out, g2_prev, norm_attn_g[l][None, :], sc1, sh1, w["win"],
            mla_q_norm_g[l][None, :], mla_kv_norm_g[l][None, :], w["wq"], w["wk"], w["wv"], tab)
        diff_args = dict(lam=lam_all[l][None, :], g=diff_subnorm_g[l][None, :],
                         lam_scale=1.0 - lam_inits[l])
        om = _attention("mla", qm, km, vm, False)
        od = _attention("diff", qd, kd, vd, False, **diff_args)
        if not last:
            om = jnp.concatenate([om, _attention("mla", qm, km, vm, True)], axis=0)
            od = jnp.concatenate([od, _attention("diff", qd, kd, vd, True, **diff_args)], axis=0)
        n_blocks = NB_LAT if last else NB_ALL
        n = n_blocks * TM
        xs, h2, eidx_t, gate_t = _mid(n_blocks, xs, om, od, w["wo_m"], w["wo_d"], g1,
                                      norm_ffn_g[l][None, :], sc2, sh2, w["pwq"], w["sk"])
        idx = eidx_t.T
        gate3 = gate_t.reshape(16, 8, n // 8, 8).transpose(2, 1, 3, 0).reshape(n // 8, 8, LANES)
        act3 = _peer_u(n, idx, h2.reshape(n, SUBLANES, LANES), gate3, _pack_table(peer_u, l))
        peer_out = _peer_v(n, idx, act3, _pack_table(peer_v, l)).reshape(n * SUBLANES, LANES)
        g2_prev = g2
    out = _final(xs, peer_out, g2_prev, final_norm_g[None, :])
    return out.reshape(BATCH, SEQ, D_MODEL)
```

```python
import functools
import math

import jax
import jax.numpy as jnp
import numpy as np
from jax import lax
from jax.experimental import pallas as pl
from jax.experimental.pallas import tpu as pltpu

F32 = jnp.float32
BF16 = jnp.bfloat16

D_MODEL = 1024
BATCH = 8
SEQ = 4096
DEPTH = 4
GRID_W = 64
CTX_LEN = 256
N_MOD = 6
NORM_EPS = 1e-6
ROPE_BASE = 10000.0
MLA_HEADS = 8
MLA_HEAD_DIM = 64
MLA_ROPE_DIM = 32
MLA_Q_RANK = 384
MLA_KV_RANK = 256
DIFF_HEADS = 4
DIFF_HEAD_DIM = 64
PEER_HEADS = 8
PEER_TOPK = 16
N_KEYS = 128
N_EXPERTS = N_KEYS * N_KEYS

T_LAT = BATCH * SEQ
T_CTX = BATCH * CTX_LEN
TT = T_LAT + T_CTX
TM = 256
NB_LAT = T_LAT // TM
NB_ALL = TT // TM
BLK_PER_BATCH = SEQ // TM
MOD_ROWS = 16
PROJ_PAD = 2304
TQ = 256
TK = 512
PEER_SB = 16
LANES = 128
SUBLANES = 8
VMEM_LIMIT = 56 * 1024 * 1024
LOG2_E = math.log2(math.e)


def _cparams(sem, vmem=VMEM_LIMIT):
    return pltpu.CompilerParams(dimension_semantics=sem, vmem_limit_bytes=vmem)


def _rms(x):
    return x * lax.rsqrt(jnp.mean(x * x, axis=-1, keepdims=True) + NORM_EPS)


def _tile_rows_load(ref):
    return jnp.concatenate([ref[pl.ds(s, TM, stride=SUBLANES), :] for s in range(SUBLANES)], axis=1)


def _tile_rows_store(ref, x):
    for s in range(SUBLANES):
        ref[pl.ds(s, TM, stride=SUBLANES), :] = x[:, s * LANES:(s + 1) * LANES]


def _bid(i):
    return jnp.where(i < NB_LAT, i // BLK_PER_BATCH, BATCH)


def _posblk(i):
    return jnp.where(i < NB_LAT, i % BLK_PER_BATCH, BLK_PER_BATCH)


def _mod_body(cc_ref, w_ref, b_ref, o_ref):
    cc = cc_ref[...]
    s = cc / (1.0 + jnp.exp(-cc))
    o_ref[0] = jnp.dot(s, w_ref[0], precision=lax.Precision.HIGHEST,
                       preferred_element_type=F32) + b_ref[0]


def _modulation(cc, w_ada, b_ada):
    nj = N_MOD
    return pl.pallas_call(
        _mod_body,
        grid=(DEPTH, nj),
        in_specs=[
            pl.BlockSpec((MOD_ROWS, D_MODEL), lambda l, j: (0, 0)),
            pl.BlockSpec((1, D_MODEL, D_MODEL), lambda l, j: (l, 0, j)),
            pl.BlockSpec((1, 1, D_MODEL), lambda l, j: (l, 0, j)),
        ],
        out_specs=pl.BlockSpec((1, MOD_ROWS, D_MODEL), lambda l, j: (l, 0, j)),
        out_shape=jax.ShapeDtypeStruct((DEPTH, MOD_ROWS, N_MOD * D_MODEL), F32),
        compiler_params=_cparams(("arbitrary", "arbitrary")),
        name="modulation",
    )(cc, w_ada, b_ada.reshape(DEPTH, 1, N_MOD * D_MODEL))


def _lam_body(d0_ref, d1_ref, d2_ref, d3_ref, li_ref, o_ref):
    a = jnp.sum(d0_ref[...] * d1_ref[...], axis=-1, keepdims=True)
    b = jnp.sum(d2_ref[...] * d3_ref[...], axis=-1, keepdims=True)
    o_ref[...] = jnp.exp(a) - jnp.exp(b) + li_ref[...]


def _diff_lambda(diff_lambda, lam_init):
    dl = diff_lambda.astype(F32)
    return pl.pallas_call(
        _lam_body,
        out_shape=jax.ShapeDtypeStruct((DEPTH, LANES), F32),
        name="diff_lambda",
    )(dl[:, 0], dl[:, 1], dl[:, 2], dl[:, 3], lam_init)


def _rope(xb, c, sp, sm, shift):
    return xb * c + pltpu.roll(xb, shift, 1) * sp + pltpu.roll(xb, LANES - shift, 1) * sm


def _qkv_body(has_peer, *refs):
    if has_peer:
        x_ref, peer_ref, g2_ref = refs[:3]
        refs = refs[3:]
    else:
        x_ref = refs[0]
        refs = refs[1:]
    (ng_ref, sc_ref, sh_ref, win_ref, gq_ref, gkv_ref, wq_ref, wk_ref, wv_ref, tab_ref) = refs[:10]
    outs = refs[10:]
    if has_peer:
        xo_ref, outs = outs[0], outs[1:]
    qm_ref, km_ref, vm_ref, qd_ref, kd_ref, vd_ref = outs

    x = x_ref[...]
    if has_peer:
        x = x + g2_ref[0] * _tile_rows_load(peer_ref)
        xo_ref[...] = x
    h = _rms(x) * ng_ref[...] * (1.0 + sc_ref[0]) + sh_ref[0]
    proj = jnp.dot(h.astype(BF16), win_ref[...], preferred_element_type=F32)
    cq = _rms(proj[:, 0:384]) * gq_ref[...]
    ckv = _rms(proj[:, 384:640]) * gkv_ref[...]
    kr = proj[:, 640:768]
    q = jnp.dot(cq.astype(BF16), wq_ref[...], preferred_element_type=F32)
    ckv16 = ckv.astype(BF16)
    k = jnp.dot(ckv16, wk_ref[...], preferred_element_type=F32)
    v = jnp.dot(ckv16, wv_ref[...], preferred_element_type=F32)
    tab = tab_ref[...]
    cm, spm, smm = tab[:, 0:128], tab[:, 128:256], tab[:, 256:384]
    cd, spd, smd = tab[:, 384:512], tab[:, 512:640], tab[:, 640:768]
    kr_rot = _rope(kr, cm, spm, smm, MLA_ROPE_DIM // 2)
    scale_m = LOG2_E * (MLA_HEAD_DIM + MLA_ROPE_DIM) ** -0.5
    for hh in range(MLA_HEADS):
        sl = slice(hh * LANES, (hh + 1) * LANES)
        qm_ref[:, sl] = (_rope(q[:, sl], cm, spm, smm, MLA_ROPE_DIM // 2) * scale_m).astype(BF16)
        km_ref[:, sl] = (k[:, sl] + kr_rot).astype(BF16)
    vm_ref[...] = v.T.astype(BF16)
    scale_d = LOG2_E * DIFF_HEAD_DIM ** -0.5
    for hh in range(DIFF_HEADS):
        sl = slice(hh * LANES, (hh + 1) * LANES)
        qd = proj[:, 768 + hh * LANES:768 + (hh + 1) * LANES]
        kd = proj[:, 1280 + hh * LANES:1280 + (hh + 1) * LANES]
        qd_ref[:, sl] = (_rope(qd, cd, spd, smd, DIFF_HEAD_DIM // 2) * scale_d).astype(BF16)
        kd_ref[:, sl] = _rope(kd, cd, spd, smd, DIFF_HEAD_DIM // 2).astype(BF16)
    vd_ref[...] = proj[:, 1792:2304].T.astype(BF16)


def _qkv(x, peer, g2, ng, sc, sh, win, gq, gkv, wq, wk, wv, tab):
    has_peer = peer is not None
    row = lambda i: (i, 0)
    modrow = lambda i: (_bid(i), 0, 0)
    const = lambda i: (0, 0)
    in_specs = [pl.BlockSpec((TM, D_MODEL), row)]
    args = [x]
    if has_peer:
        in_specs += [pl.BlockSpec((TM * SUBLANES, LANES), row), pl.BlockSpec((1, 1, D_MODEL), modrow)]
        args += [peer, g2]
    in_specs += [
        pl.BlockSpec((1, D_MODEL), const),
        pl.BlockSpec((1, 1, D_MODEL), modrow),
        pl.BlockSpec((1, 1, D_MODEL), modrow),
        pl.BlockSpec((D_MODEL, PROJ_PAD), const),
        pl.BlockSpec((1, MLA_Q_RANK), const),
        pl.BlockSpec((1, MLA_KV_RANK), const),
        pl.BlockSpec((MLA_Q_RANK, MLA_HEADS * LANES), const),
        pl.BlockSpec((MLA_KV_RANK, MLA_HEADS * LANES), const),
        pl.BlockSpec((MLA_KV_RANK, 512), const),
        pl.BlockSpec((TM, 768), lambda i: (_posblk(i), 0)),
    ]
    args += [ng, sc, sh, win, gq, gkv, wq, wk, wv, tab]
    out_specs, out_shape = [], []
    if has_peer:
        out_specs.append(pl.BlockSpec((TM, D_MODEL), row))
        out_shape.append(jax.ShapeDtypeStruct((TT, D_MODEL), F32))
    for width, transposed in ((1024, False), (1024, False), (512, True),
                              (512, False), (512, False), (512, True)):
        if transposed:
            out_specs.append(pl.BlockSpec((width, TM), lambda i: (0, i)))
            out_shape.append(jax.ShapeDtypeStruct((width, TT), BF16))
        else:
            out_specs.append(pl.BlockSpec((TM, width), row))
            out_shape.append(jax.ShapeDtypeStruct((TT, width), BF16))
    res = pl.pallas_call(
        functools.partial(_qkv_body, has_peer),
        grid=(NB_ALL,),
        in_specs=in_specs,
        out_specs=out_specs,
        out_shape=out_shape,
        compiler_params=_cparams(("parallel",)),
        name="qkv",
    )(*args)
    if has_peer:
        return res[0], res[1:]
    return x, res


def _flash_pair(tq, qs, ksls, vsls, kc_ref, kl_ref, vtc_ref, vtl_ref, n_pairs, sa_scr, sb_scr):
    n_lat = SEQ // TK

    def col_reduce(parts, op, final):
        while len(parts) > 1:
            parts = [op(parts[i], parts[i + 1]) for i in range(0, len(parts), 2)]
        return final(parts[0], axis=0, keepdims=True)

    def row_groups(x):
        return [x[i:i + SUBLANES] for i in range(0, x.shape[0], SUBLANES)]

    def scores(k_of):
        return [lax.dot_general(k_of(ksl), q, (((1,), (1,)), ((), ())), preferred_element_type=F32)
                for q, ksl in zip(qs, ksls)]

    def softmax_pv(read_s, vt_of, carries):
        mid = []
        for h, (m, l, acc) in enumerate(carries):
            m_new = jnp.maximum(m, col_reduce(row_groups(read_s(h)), jnp.maximum, jnp.max))
            alpha = jnp.exp2(m - m_new)
            p = jnp.exp2(read_s(h) - m_new)
            l = alpha * l + col_reduce(row_groups(p), jnp.add, jnp.sum)
            mid.append((m_new, l, alpha * acc, p.astype(BF16)))
        return tuple((m_new, l, acc + jnp.dot(vt_of(vsl), p, preferred_element_type=F32))
                     for (m_new, l, acc, p), vsl in zip(mid, vsls))

    def k_lat(c):
        return lambda ksl: kl_ref[c * TK:(c + 1) * TK, ksl]

    def vt_lat(c):
        return lambda vsl: vtl_ref[vsl, c * TK:(c + 1) * TK]

    def stage(scr, sc):
        for h, x in enumerate(sc):
            scr[h] = x

    init = tuple((jnp.full((1, tq), -jnp.inf, F32), jnp.zeros((1, tq), F32),
                  jnp.zeros((vsl.stop - vsl.start, tq), F32)) for vsl in vsls)
    ctx_scores = scores(lambda ksl: kc_ref[:, ksl])
    if n_pairs:
        stage(sa_scr, scores(k_lat(0)))
    carries = softmax_pv(lambda h: ctx_scores[h], lambda vsl: vtc_ref[vsl, :], init)

    def pair(j, carries):
        c = 2 * j
        stage(sb_scr, scores(k_lat(c + 1)))
        carries = softmax_pv(lambda h: sa_scr[h], vt_lat(c), carries)
        if c + 2 < n_lat:
            stage(sa_scr, scores(k_lat(c + 2)))
        return softmax_pv(lambda h: sb_scr[h], vt_lat(c + 1), carries)

    for j in range(n_pairs):
        carries = pair(j, carries)
    return [acc / l for (m, l, acc) in carries]


def _attn_mla_body(tq, n_pairs, *refs):
    if n_pairs:
        q_ref, kc_ref, kl_ref, vtc_ref, vtl_ref, o_ref, sa_scr, sb_scr = refs
    else:
        q_ref, kc_ref, vtc_ref, o_ref = refs
        kl_ref = vtl_ref = sa_scr = sb_scr = None
    qs = [q_ref[:, 0:LANES], q_ref[:, LANES:2 * LANES]]
    ksls = [slice(0, LANES), slice(LANES, 2 * LANES)]
    vsls = [slice(0, MLA_HEAD_DIM), slice(MLA_HEAD_DIM, 2 * MLA_HEAD_DIM)]
    ot0, ot1 = _flash_pair(tq, qs, ksls, vsls, kc_ref, kl_ref, vtc_ref, vtl_ref, n_pairs, sa_scr, sb_scr)
    o_ref[...] = jnp.concatenate([ot0, ot1], axis=0).T.astype(o_ref.dtype)


def _attn_diff_body(tq, n_pairs, lam_scale, *refs):
    if n_pairs:
        q_ref, kc_ref, kl_ref, vtc_ref, vtl_ref, lam_ref, g_ref, o_ref, sa_scr, sb_scr = refs
    else:
        q_ref, kc_ref, vtc_ref, lam_ref, g_ref, o_ref = refs
        kl_ref = vtl_ref = sa_scr = sb_scr = None
    lane = lax.broadcasted_iota(jnp.int32, (tq, LANES), 1)
    q = q_ref[...]
    zero = jnp.zeros_like(q)
    qs = [jnp.where(lane < DIFF_HEAD_DIM, q, zero), jnp.where(lane < DIFF_HEAD_DIM, zero, q)]
    ksls = [slice(0, LANES), slice(0, LANES)]
    vsls = [slice(0, LANES), slice(0, LANES)]
    ot0, ot1 = _flash_pair(tq, qs, ksls, vsls, kc_ref, kl_ref, vtc_ref, vtl_ref, n_pairs, sa_scr, sb_scr)
    o = ot0.T - lam_ref[...] * ot1.T
    o_ref[...] = (_rms(o) * g_ref[...] * lam_scale).astype(o_ref.dtype)


def _attention(kind, q, k, vt, ctx_queries, lam=None, g=None, lam_scale=None):
    qw = 2 * LANES if kind == "mla" else LANES
    ctx0 = T_LAT // CTX_LEN
    if ctx_queries:
        tq, n_pairs, steps, rows = CTX_LEN, 0, 1, T_CTX
        q_map = lambda b, p, i: (ctx0 + b, p)
        o_map = lambda b, p, i: (b, p)
    else:
        tq, n_pairs, steps, rows = TQ, SEQ // (2 * TK), SEQ // TQ, T_LAT
        q_map = lambda b, p, i: (b * steps + i, p)
        o_map = q_map
    kc_spec = pl.BlockSpec((CTX_LEN, qw), lambda b, p, i: (ctx0 + b, p))
    vtc_spec = pl.BlockSpec((LANES, CTX_LEN), lambda b, p, i: (p, ctx0 + b))
    in_specs = [pl.BlockSpec((tq, qw), q_map), kc_spec]
    args = [q, k]
    if n_pairs:
        in_specs += [pl.BlockSpec((SEQ, qw), lambda b, p, i: (b, p)), vtc_spec,
                     pl.BlockSpec((LANES, SEQ), lambda b, p, i: (p, b))]
        args += [k, vt, vt]
    else:
        in_specs += [vtc_spec]
        args += [vt]
    if kind == "mla":
        body = functools.partial(_attn_mla_body, tq, n_pairs)
    else:
        body = functools.partial(_attn_diff_body, tq, n_pairs, lam_scale)
        in_specs += [pl.BlockSpec((1, LANES), lambda b, p, i: (0, 0)),
                     pl.BlockSpec((1, LANES), lambda b, p, i: (0, 0))]
        args += [lam, g]
    scratch = [pltpu.VMEM((2, TK, tq), F32), pltpu.VMEM((2, TK, tq), F32)] if n_pairs else []
    return pl.pallas_call(
        body,
        grid=(BATCH, 4, steps),
        in_specs=in_specs,
        out_specs=pl.BlockSpec((tq, LANES), o_map),
        out_shape=jax.ShapeDtypeStruct((rows, 512), BF16),
        scratch_shapes=scratch,
        compiler_params=_cparams(("parallel", "parallel", "arbitrary")),
        name="attn_" + kind + ("_ctx" if ctx_queries else ""),
    )(*args)


HEADS_PER_TRIP = 4


def _top16(s, payload=None):
    n_rows = s.shape[0]
    rowf = lax.broadcasted_iota(jnp.int32, s.shape, 0).astype(F32)
    slot = lax.broadcasted_iota(jnp.int32, (PEER_TOPK, s.shape[1]), 0)
    vals = jnp.zeros((PEER_TOPK, s.shape[1]), F32)
    picks = jnp.zeros((PEER_TOPK, s.shape[1]), F32)
    for r in range(PEER_TOPK):
        m = jnp.max(s, axis=0, keepdims=True)
        am = jnp.min(jnp.where(s == m, rowf, float(n_rows)), axis=0, keepdims=True)
        hit = rowf == am
        pick = am if payload is None else jnp.max(jnp.where(hit, payload, -1.0), axis=0, keepdims=True)
        vals = jnp.where(slot == r, m, vals)
        picks = jnp.where(slot == r, pick, picks)
        s = jnp.where(hit, -jnp.inf, s)
    return vals, picks


def _staircase(a16, b16, combine, pad):
    tm = a16.shape[1]
    sub = lax.broadcasted_iota(jnp.int32, (SUBLANES, tm), 0)
    a_lo, a_hi = a16[0:SUBLANES], a16[SUBLANES:]
    b_lo, b_hi = b16[0:SUBLANES], b16[SUBLANES:]
    row = lambda x, r: jnp.broadcast_to(x[r:r + 1], (SUBLANES, tm))
    take = lambda x, idx: jnp.take_along_axis(x, idx, axis=0)
    a3 = jnp.where(sub < 5, 2, 3)
    b3 = jnp.where(sub < 5, sub, sub - 5)
    a4 = jnp.where(sub < 1, 3, jnp.where(sub < 4, 4, jnp.where(sub < 6, 5, 6)))
    b4 = jnp.where(sub < 1, 3, jnp.where(sub < 4, sub - 1, jnp.where(sub < 6, sub - 4, sub - 6)))
    pieces = [
        combine(row(a_lo, 0), b_lo),
        combine(row(a_lo, 0), b_hi),
        combine(row(a_lo, 1), b_lo),
        combine(take(a_lo, a3), take(b_lo, b3)),
        combine(take(a_lo, a4), take(b_lo, b4)),
        jnp.where(sub < 2, combine(row(a_lo, 7), b_lo), pad),
        combine(a_hi, row(b_lo, 0)),
    ]
    return jnp.concatenate(pieces, axis=0)


def _mid_body(x_ref, om_ref, od_ref, wo_m_ref, wo_d_ref, g1_ref, ng_ref, sc_ref, sh_ref,
              wq_ref, sk_ref, xo_ref, h_ref, eidx_ref, gate_ref, q_scr):
    y = (jnp.dot(om_ref[...], wo_m_ref[...], preferred_element_type=F32)
         + jnp.dot(od_ref[...], wo_d_ref[...], preferred_element_type=F32))
    x = x_ref[...] + g1_ref[0] * y
    xo_ref[...] = x
    h = _rms(x) * ng_ref[...] * (1.0 + sc_ref[0]) + sh_ref[0]
    _tile_rows_store(h_ref, h)
    q_scr[...] = jnp.dot(h.astype(BF16), wq_ref[...], preferred_element_type=F32).astype(BF16)

    def head(hh):
        sv, si = [], []
        for c in range(2):
            c0 = pl.multiple_of(hh * 2 * N_KEYS + c * N_KEYS, N_KEYS)
            qh = q_scr[:, pl.ds(c0, N_KEYS)]
            s = lax.dot_general(sk_ref[c], qh, (((1,), (1,)), ((), ())), preferred_element_type=F32)
            vals, keys = _top16(s)
            sv.append(vals)
            si.append(keys)
        cand_s = _staircase(sv[0], sv[1], lambda a, b: a + b, -jnp.inf)
        cand_e = _staircase(si[0], si[1], lambda a, b: a * float(N_KEYS) + b, -1.0)
        top_s, top_e = _top16(cand_s, cand_e)
        ex = jnp.exp(top_s - jnp.max(top_s, axis=0, keepdims=True))
        gate = ex / jnp.sum(ex, axis=0, keepdims=True)
        r0 = pl.multiple_of(hh * PEER_TOPK, PEER_TOPK)
        eidx_ref[pl.ds(r0, PEER_TOPK), :] = top_e.astype(jnp.int32) * ROWS_PER_EXPERT
        gate_ref[pl.ds(r0, PEER_TOPK), :] = gate

    def head_group(j, _):
        for i in range(HEADS_PER_TRIP):
            head(HEADS_PER_TRIP * j + i)
        return 0

    lax.fori_loop(0, PEER_HEADS // HEADS_PER_TRIP, head_group, 0)


def _mid(n_blocks, x, om, od, wo_m, wo_d, g1, ng, sc, sh, wq, sk):
    n = n_blocks * TM
    row = lambda i: (i, 0)
    modrow = lambda i: (_bid(i), 0, 0)
    const = lambda i: (0, 0)
    return pl.pallas_call(
        _mid_body,
        grid=(n_blocks,),
        in_specs=[
            pl.BlockSpec((TM, D_MODEL), row),
            pl.BlockSpec((TM, 512), row),
            pl.BlockSpec((TM, 512), row),
            pl.BlockSpec((512, D_MODEL), const),
            pl.BlockSpec((512, D_MODEL), const),
            pl.BlockSpec((1, 1, D_MODEL), modrow),
            pl.BlockSpec((1, D_MODEL), const),
            pl.BlockSpec((1, 1, D_MODEL), modrow),
            pl.BlockSpec((1, 1, D_MODEL), modrow),
            pl.BlockSpec((D_MODEL, PEER_HEADS * 2 * N_KEYS), const),
            pl.BlockSpec((2, N_KEYS, N_KEYS), lambda i: (0, 0, 0)),
        ],
        out_specs=[
            pl.BlockSpec((TM, D_MODEL), row),
            pl.BlockSpec((TM * SUBLANES, LANES), row),
            pl.BlockSpec((PEER_HEADS * PEER_TOPK, TM), lambda i: (0, i)),
            pl.BlockSpec((PEER_HEADS * PEER_TOPK, TM), lambda i: (0, i)),
        ],
        out_shape=[
            jax.ShapeDtypeStruct((n, D_MODEL), F32),
            jax.ShapeDtypeStruct((n * SUBLANES, LANES), F32),
            jax.ShapeDtypeStruct((PEER_HEADS * PEER_TOPK, n), jnp.int32),
            jax.ShapeDtypeStruct((PEER_HEADS * PEER_TOPK, n), F32),
        ],
        scratch_shapes=[pltpu.VMEM((TM, PEER_HEADS * 2 * N_KEYS), BF16)],
        compiler_params=_cparams(("parallel",)),
        name="mid",
    )(x, om, od, wo_m, wo_d, g1, ng, sc, sh, wq, sk)


N_SEL = PEER_HEADS * PEER_TOPK
HALF = D_MODEL // 2
ROWS_PER_EXPERT = HALF // LANES
_BFLY_ORDER = (0, 4, 2, 6, 1, 5, 3, 7)


PACK_BLOCK = 512


def _pack_body(t_ref, o_ref):
    u = pltpu.bitcast(t_ref[0], jnp.uint32)
    r = u + jnp.uint32(0x7FFF) + ((u >> 16) & jnp.uint32(1))
    w = (r[:, :HALF] >> 16) | (r[:, HALF:] & jnp.uint32(0xFFFF0000))
    for s in range(ROWS_PER_EXPERT):
        o_ref[pl.ds(s, PACK_BLOCK, stride=ROWS_PER_EXPERT), :] = w[:, s * LANES:(s + 1) * LANES]


def _pack_table(tabs, layer):
    n = tabs.shape[1]
    return pl.pallas_call(
        _pack_body,
        grid=(n // PACK_BLOCK,),
        in_specs=[pl.BlockSpec((1, PACK_BLOCK, D_MODEL), lambda i: (layer, i, 0))],
        out_specs=pl.BlockSpec((PACK_BLOCK * ROWS_PER_EXPERT, LANES), lambda i: (i, 0)),
        out_shape=jax.ShapeDtypeStruct((n * ROWS_PER_EXPERT, LANES), jnp.uint32),
        compiler_params=_cparams(("parallel",)),
        name="pack_table",
    )(tabs)


def _load_pair(tab_ref, ra, rb):
    w = jnp.concatenate([tab_ref[pl.ds(pl.multiple_of(ra, ROWS_PER_EXPERT), ROWS_PER_EXPERT), :],
                         tab_ref[pl.ds(pl.multiple_of(rb, ROWS_PER_EXPERT), ROWS_PER_EXPERT), :]], axis=0)
    lo = pltpu.bitcast(w << 16, F32)
    hi = pltpu.bitcast(w & jnp.uint32(0xFFFF0000), F32)
    return lo, hi


def _with_index_buffers(idx_hbm, idx_a, idx_b, sem, first, second):
    step = pl.program_id(0)
    words = PEER_SB * N_SEL

    def idx_copy(block, dst, k):
        return pltpu.make_async_copy(idx_hbm.at[pl.ds(block * words, words)], dst, sem.at[k])

    @pl.when(step == 0)
    def _():
        idx_copy(0, idx_a, 0).start()

    idx_copy(2 * step + 1, idx_b, 1).start()
    idx_copy(2 * step, idx_a, 0).wait()
    first(idx_a)

    @pl.when(step + 1 < pl.num_programs(0))
    def _():
        idx_copy(2 * step + 2, idx_a, 0).start()

    idx_copy(2 * step + 1, idx_b, 1).wait()
    second(idx_b)


def _peer_u_body(idx_hbm, x_ref, gate_ref, tab_ref, act_ref, idx_a, idx_b, r_scr, score_scr, sem):
    sub = lax.broadcasted_iota(jnp.int32, (SUBLANES, LANES), 0)
    lane = lax.broadcasted_iota(jnp.int32, (SUBLANES, LANES), 1)
    top = sub < 4
    n_grp = N_SEL // SUBLANES
    tiles_per_buf = PEER_SB // SUBLANES

    def fold(p, q, width):
        tp = p + pltpu.roll(p, SUBLANES - width, 0)
        tq = q + pltpu.roll(q, SUBLANES - width, 0)
        return jnp.where((sub % (2 * width)) < width, tp, pltpu.roll(tq, width, 0))

    def gather_tile(idx_s, tile, tile_in_buf):
        for tt in range(SUBLANES):
            tl = tile_in_buf * SUBLANES + tt
            xt = x_ref[tile * SUBLANES + tt]
            xsw = pltpu.roll(xt, 4, 0)
            xlo = jnp.where(top, xt, xsw)
            xhi = jnp.where(top, xsw, xt)
            for g in range(n_grp):
                parts = []
                for pr in range(4):
                    ra = idx_s[tl * N_SEL + g * SUBLANES + _BFLY_ORDER[2 * pr]]
                    rb = idx_s[tl * N_SEL + g * SUBLANES + _BFLY_ORDER[2 * pr + 1]]
                    lo, hi = _load_pair(tab_ref, ra, rb)
                    parts.append(lo * xlo + hi * xhi)
                r_scr[tile * N_SEL + tt * n_grp + g] = fold(
                    fold(parts[0], parts[1], 2), fold(parts[2], parts[3], 2), 1)

    quarter = pl.program_id(0) % STEPS_PER_ACT_BLOCK

    def reduce_tile(tile):
        lane0 = quarter * (2 * PEER_SB) + tile * SUBLANES
        for g in range(n_grp):
            a = score_scr[g]
            for tt in range(SUBLANES):
                col = jnp.sum(r_scr[tile * N_SEL + tt * n_grp + g], axis=1, keepdims=True)
                a = jnp.where(lane == lane0 + tt, col, a)
            score_scr[g] = a

    @pl.when(quarter == 0)
    def _():
        score_scr[...] = jnp.zeros(score_scr.shape, F32)

    def first(idx_s):
        for k in range(tiles_per_buf):
            if k:
                reduce_tile(k - 1)
            gather_tile(idx_s, k, k)

    def second(idx_s):
        for k in range(tiles_per_buf):
            reduce_tile(tiles_per_buf + k - 1)
            gather_tile(idx_s, tiles_per_buf + k, k)
        reduce_tile(2 * tiles_per_buf - 1)

    _with_index_buffers(idx_hbm, idx_a, idx_b, sem, first, second)

    @pl.when(quarter == STEPS_PER_ACT_BLOCK - 1)
    def _():
        a = score_scr[...]
        act_ref[...] = 0.5 * a * (1.0 + lax.erf(a * (2.0 ** -0.5))) * gate_ref[...]


def _peer_v_body(idx_hbm, act_ref, tab_ref, o_ref, idx_a, idx_b, sem):
    sub = lax.broadcasted_iota(jnp.int32, (SUBLANES, LANES), 0)
    lane = lax.broadcasted_iota(jnp.int32, (SUBLANES, LANES), 1)
    top = sub < 4
    n_grp = N_SEL // SUBLANES
    pair_rows = [jnp.where(top, 2 * p, 2 * p + 1) for p in range(4)]

    quarter = pl.program_id(0) % STEPS_PER_ACT_BLOCK

    def process(idx_s, half):
        for tile in range(PEER_SB // SUBLANES):
            for tt in range(SUBLANES):
                tl = tile * SUBLANES + tt
                token_lane = quarter * (2 * PEER_SB) + half * PEER_SB + tl
                acc = [jnp.zeros((SUBLANES, LANES), F32) for _ in range(4)]
                for g in range(n_grp):
                    col = jnp.sum(jnp.where(lane == token_lane, act_ref[g], 0.0), axis=1, keepdims=True)
                    actg = jnp.broadcast_to(col, (SUBLANES, LANES))
                    for p in range(4):
                        ra = idx_s[tl * N_SEL + g * SUBLANES + 2 * p]
                        rb = idx_s[tl * N_SEL + g * SUBLANES + 2 * p + 1]
                        lo, hi = _load_pair(tab_ref, ra, rb)
                        av = jnp.take_along_axis(actg, pair_rows[p], axis=0)
                        k = 2 * (p % 2)
                        acc[k] = acc[k] + av * lo
                        acc[k + 1] = acc[k + 1] + av * hi
                lo = acc[0] + acc[2]
                hi = acc[1] + acc[3]
                lo = lo + pltpu.roll(lo, 4, 0)
                hi = hi + pltpu.roll(hi, 4, 0)
                o_ref[half * PEER_SB + tl] = jnp.where(top, lo, hi)

    _with_index_buffers(idx_hbm, idx_a, idx_b, sem,
                        lambda idx_s: process(idx_s, 0), lambda idx_s: process(idx_s, 1))


def _table_spec():
    return pl.BlockSpec((N_EXPERTS * ROWS_PER_EXPERT, LANES), lambda i: (0, 0),
                        pipeline_mode=pl.Buffered(1))


STEPS_PER_ACT_BLOCK = LANES // (2 * PEER_SB)


def _act_block_spec():
    return pl.BlockSpec((N_SEL // SUBLANES, SUBLANES, LANES), lambda i: (0, 0, i // STEPS_PER_ACT_BLOCK))


def _peer_idx_scratch():
    return [pltpu.SMEM((PEER_SB * N_SEL,), jnp.int32), pltpu.SMEM((PEER_SB * N_SEL,), jnp.int32)]


def _peer_u(n, idx, x3, gate3, tab):
    tb = 2 * PEER_SB
    return pl.pallas_call(
        _peer_u_body,
        grid=(n // tb,),
        in_specs=[
            pl.BlockSpec(memory_space=pl.ANY),
            pl.BlockSpec((tb, SUBLANES, LANES), lambda i: (i, 0, 0)),
            _act_block_spec(),
            _table_spec(),
        ],
        out_specs=_act_block_spec(),
        out_shape=jax.ShapeDtypeStruct((N_SEL // SUBLANES, SUBLANES, n), F32),
        scratch_shapes=_peer_idx_scratch() + [
            pltpu.VMEM((tb // SUBLANES * N_SEL, SUBLANES, LANES), F32),
            pltpu.VMEM((N_SEL // SUBLANES, SUBLANES, LANES), F32),
            pltpu.SemaphoreType.DMA((2,))],
        compiler_params=_cparams(("arbitrary",)),
        name="peer_u",
    )(idx.reshape(-1), x3, gate3, tab)


def _peer_v(n, idx, act3, tab):
    tb = 2 * PEER_SB
    return pl.pallas_call(
        _peer_v_body,
        grid=(n // tb,),
        in_specs=[
            pl.BlockSpec(memory_space=pl.ANY),
            _act_block_spec(),
            _table_spec(),
        ],
        out_specs=pl.BlockSpec((tb, SUBLANES, LANES), lambda i: (i, 0, 0)),
        out_shape=jax.ShapeDtypeStruct((n, SUBLANES, LANES), F32),
        scratch_shapes=_peer_idx_scratch() + [pltpu.SemaphoreType.DMA((2,))],
        compiler_params=_cparams(("arbitrary",)),
        name="peer_v",
    )(idx.reshape(-1), act3, tab)


def _final_body(x_ref, peer_ref, g2_ref, g_ref, o_ref):
    x = x_ref[...] + g2_ref[0] * _tile_rows_load(peer_ref)
    o_ref[...] = _rms(x) * g_ref[...]


def _final(x, peer, g2, g):
    row = lambda i: (i, 0)
    return pl.pallas_call(
        _final_body,
        grid=(NB_LAT,),
        in_specs=[
            pl.BlockSpec((TM, D_MODEL), row),
            pl.BlockSpec((TM * SUBLANES, LANES), row),
            pl.BlockSpec((1, 1, D_MODEL), lambda i: (_bid(i), 0, 0)),
            pl.BlockSpec((1, D_MODEL), lambda i: (0, 0)),
        ],
        out_specs=pl.BlockSpec((TM, D_MODEL), row),
        out_shape=jax.ShapeDtypeStruct((T_LAT, D_MODEL), F32),
        compiler_params=_cparams(("parallel",)),
        name="final_norm",
    )(x, peer, g2, g)


def _deinterleave(width):
    return np.concatenate([np.arange(0, width, 2), np.arange(1, width, 2)])


def _rope_tables():
    pos = np.arange(SEQ)
    rowp = jnp.asarray(pos // GRID_W, F32)
    colp = jnp.asarray(pos % GRID_W, F32)

    def angles(dim):
        quarter = dim // 4
        inv = ROPE_BASE ** (-jnp.arange(quarter, dtype=F32) / quarter)
        return jnp.concatenate([rowp[:, None] * inv, colp[:, None] * inv], axis=-1)

    am = angles(MLA_ROPE_DIM)
    ad = angles(DIFF_HEAD_DIM)
    cm, sm_ = jnp.cos(am), jnp.sin(am)
    cd, sd = jnp.cos(ad), jnp.sin(ad)
    one = lambda w: jnp.ones((SEQ, w), F32)
    zero = lambda w: jnp.zeros((SEQ, w), F32)
    t_cm = jnp.concatenate([one(64), cm, cm, one(32)], axis=1)
    t_spm = jnp.concatenate([zero(80), sm_, zero(32)], axis=1)
    t_smm = jnp.concatenate([zero(64), -sm_, zero(48)], axis=1)
    t_cd = jnp.concatenate([cd, cd, cd, cd], axis=1)
    t_spd = jnp.concatenate([zero(32), sd, zero(32), sd], axis=1)
    t_smd = jnp.concatenate([-sd, zero(32), -sd, zero(32)], axis=1)
    lat = jnp.concatenate([t_cm, t_spm, t_smm, t_cd, t_spd, t_smd], axis=1)
    ident = jnp.concatenate([jnp.ones((CTX_LEN, 128), F32), jnp.zeros((CTX_LEN, 256), F32),
                             jnp.ones((CTX_LEN, 128), F32), jnp.zeros((CTX_LEN, 256), F32)], axis=1)
    return jnp.concatenate([lat, ident], axis=0)


def _prep_layer_weights(w_in, wq_up, wq_rope, wk_up, wv_up, w_out, peer_wq, peer_subkeys):
    p32 = _deinterleave(MLA_ROPE_DIM)
    p64 = _deinterleave(DIFF_HEAD_DIM)
    z = lambda w: jnp.zeros((D_MODEL, w), F32)
    kr = w_in[:, 640:672][:, p32]
    qd = w_in[:, 672:1184].reshape(D_MODEL, 8, 64)[:, :, p64].reshape(D_MODEL, 512)
    kd = w_in[:, 1184:1696].reshape(D_MODEL, 8, 64)[:, :, p64].reshape(D_MODEL, 512)
    win = jnp.concatenate([w_in[:, 0:640], z(64), kr, z(32), qd, kd, w_in[:, 1696:2208]], axis=1)
    qn = wq_up.reshape(MLA_Q_RANK, MLA_HEADS, MLA_HEAD_DIM)
    qr = wq_rope.reshape(MLA_Q_RANK, MLA_HEADS, MLA_ROPE_DIM)[:, :, p32]
    wq = jnp.concatenate([qn, qr, jnp.zeros((MLA_Q_RANK, MLA_HEADS, 32), F32)], axis=2)
    kn = wk_up.reshape(MLA_KV_RANK, MLA_HEADS, MLA_HEAD_DIM)
    wk = jnp.concatenate([kn, jnp.zeros((MLA_KV_RANK, MLA_HEADS, 64), F32)], axis=2)
    return dict(
        win=win.astype(BF16),
        wq=wq.reshape(MLA_Q_RANK, MLA_HEADS * LANES).astype(BF16),
        wk=wk.reshape(MLA_KV_RANK, MLA_HEADS * LANES).astype(BF16),
        wv=wv_up.astype(BF16),
        wo_m=w_out[:512].astype(BF16),
        wo_d=w_out[512:].astype(BF16),
        pwq=peer_wq.astype(BF16),
        sk=peer_subkeys.astype(BF16),
    )


def kernel(x, c, ctx, c_ctx, norm_attn_g, norm_ffn_g, w_ada, b_ada, w_in, mla_q_norm_g, mla_wq_up, mla_wq_rope, mla_kv_norm_g, mla_wk_up, mla_wv_up, diff_lambda, diff_subnorm_g, w_out, peer_wq, peer_subkeys, peer_u, peer_v, final_norm_g):
    xs = jnp.concatenate([x.reshape(T_LAT, D_MODEL), ctx.reshape(T_CTX, D_MODEL)], axis=0)
    cc = jnp.concatenate([c, c_ctx[None, :], jnp.zeros((MOD_ROWS - BATCH - 1, D_MODEL), F32)], axis=0)
    mod = _modulation(cc, w_ada, b_ada)
    lam_inits = [0.8 - 0.6 * math.exp(-0.3 * l) for l in range(DEPTH)]
    lam_all = _diff_lambda(diff_lambda, jnp.broadcast_to(jnp.asarray(lam_inits, F32)[:, None], (DEPTH, LANES)))
    tab = _rope_tables()

    peer_out = None
    g2_prev = None
    for l in range(DEPTH):
        last = l == DEPTH - 1
        w = _prep_layer_weights(w_in[l], mla_wq_up[l], mla_wq_rope[l], mla_wk_up[l], mla_wv_up[l],
                                w_out[l], peer_wq[l], peer_subkeys[l])
        m = mod[l].reshape(MOD_ROWS, N_MOD, 1, D_MODEL)
        sh1, sc1, g1, sh2, sc2, g2 = (m[:, j] for j in range(N_MOD))
        xs, (qm, km, vm, qd, kd, vd) = _qkv(
            xs, peer_out, g2_prev, norm_attn_g[l][None, :], sc1, sh1, w["win"],
            mla_q_norm_g[l][None, :], mla_kv_norm_g[l][None, :], w["wq"], w["wk"], w["wv"], tab)
        diff_args = dict(lam=lam_all[l][None, :], g=diff_subnorm_g[l][None, :],
                         lam_scale=1.0 - lam_inits[l])
        om = _attention("mla", qm, km, vm, False)
        od = _attention("diff", qd, kd, vd, False, **diff_args)
        if not last:
            om = jnp.concatenate([om, _attention("mla", qm, km, vm, True)], axis=0)
            od = jnp.concatenate([od, _attention("diff", qd, kd, vd, True, **diff_args)], axis=0)
        n_blocks = NB_LAT if last else NB_ALL
        n = n_blocks * TM
        xs, h2, eidx_t, gate_t = _mid(n_blocks, xs, om, od, w["wo_m"], w["wo_d"], g1,
                                      norm_ffn_g[l][None, :], sc2, sh2, w["pwq"], w["sk"])
        idx = eidx_t.T
        gate3 = gate_t.reshape(N_SEL // SUBLANES, SUBLANES, n)
        act3 = _peer_u(n, idx, h2.reshape(n, SUBLANES, LANES), gate3, _pack_table(peer_u, l))
        peer_out = _peer_v(n, idx, act3, _pack_table(peer_v, l)).reshape(n * SUBLANES, LANES)
        g2_prev = g2
    out = _final(xs, peer_out, g2_prev, final_norm_g[None, :])
    return out.reshape(BATCH, SEQ, D_MODEL)
```

```python
import functools
import math

import jax
import jax.numpy as jnp
import numpy as np
from jax import lax
from jax.experimental import pallas as pl
from jax.experimental.pallas import tpu as pltpu

F32 = jnp.float32
BF16 = jnp.bfloat16

D_MODEL = 1024
BATCH = 8
SEQ = 4096
DEPTH = 4
GRID_W = 64
CTX_LEN = 256
N_MOD = 6
NORM_EPS = 1e-6
ROPE_BASE = 10000.0
MLA_HEADS = 8
MLA_HEAD_DIM = 64
MLA_ROPE_DIM = 32
MLA_Q_RANK = 384
MLA_KV_RANK = 256
DIFF_HEADS = 4
DIFF_HEAD_DIM = 64
PEER_HEADS = 8
PEER_TOPK = 16
N_KEYS = 128
N_EXPERTS = N_KEYS * N_KEYS

T_LAT = BATCH * SEQ
T_CTX = BATCH * CTX_LEN
TT = T_LAT + T_CTX
TM = 256
NB_LAT = T_LAT // TM
NB_ALL = TT // TM
BLK_PER_BATCH = SEQ // TM
MOD_ROWS = 16
PROJ_PAD = 2304
TQ = 256
TK = 512
PEER_SB = 32
LANES = 128
SUBLANES = 8
VMEM_LIMIT = 56 * 1024 * 1024
LOG2_E = math.log2(math.e)


def _cparams(sem, vmem=VMEM_LIMIT):
    return pltpu.CompilerParams(dimension_semantics=sem, vmem_limit_bytes=vmem)


def _rms(x):
    return x * lax.rsqrt(jnp.mean(x * x, axis=-1, keepdims=True) + NORM_EPS)


def _tile_rows_load(ref):
    return jnp.concatenate([ref[pl.ds(s, TM, stride=SUBLANES), :] for s in range(SUBLANES)], axis=1)


def _tile_rows_store(ref, x):
    for s in range(SUBLANES):
        ref[pl.ds(s, TM, stride=SUBLANES), :] = x[:, s * LANES:(s + 1) * LANES]


def _bid(i):
    return jnp.where(i < NB_LAT, i // BLK_PER_BATCH, BATCH)


def _posblk(i):
    return jnp.where(i < NB_LAT, i % BLK_PER_BATCH, BLK_PER_BATCH)


def _mod_body(cc_ref, w_ref, b_ref, o_ref):
    cc = cc_ref[...]
    s = cc / (1.0 + jnp.exp(-cc))
    o_ref[0] = jnp.dot(s, w_ref[0], precision=lax.Precision.HIGHEST,
                       preferred_element_type=F32) + b_ref[0]


def _modulation(cc, w_ada, b_ada):
    nj = N_MOD
    return pl.pallas_call(
        _mod_body,
        grid=(DEPTH, nj),
        in_specs=[
            pl.BlockSpec((MOD_ROWS, D_MODEL), lambda l, j: (0, 0)),
            pl.BlockSpec((1, D_MODEL, D_MODEL), lambda l, j: (l, 0, j)),
            pl.BlockSpec((1, 1, D_MODEL), lambda l, j: (l, 0, j)),
        ],
        out_specs=pl.BlockSpec((1, MOD_ROWS, D_MODEL), lambda l, j: (l, 0, j)),
        out_shape=jax.ShapeDtypeStruct((DEPTH, MOD_ROWS, N_MOD * D_MODEL), F32),
        compiler_params=_cparams(("arbitrary", "arbitrary")),
        name="modulation",
    )(cc, w_ada, b_ada.reshape(DEPTH, 1, N_MOD * D_MODEL))


def _lam_body(d0_ref, d1_ref, d2_ref, d3_ref, li_ref, o_ref):
    a = jnp.sum(d0_ref[...] * d1_ref[...], axis=-1, keepdims=True)
    b = jnp.sum(d2_ref[...] * d3_ref[...], axis=-1, keepdims=True)
    o_ref[...] = jnp.exp(a) - jnp.exp(b) + li_ref[...]


def _diff_lambda(diff_lambda, lam_init):
    dl = diff_lambda.astype(F32)
    return pl.pallas_call(
        _lam_body,
        out_shape=jax.ShapeDtypeStruct((DEPTH, LANES), F32),
        name="diff_lambda",
    )(dl[:, 0], dl[:, 1], dl[:, 2], dl[:, 3], lam_init)


def _rope(xb, c, sp, sm, shift):
    return xb * c + pltpu.roll(xb, shift, 1) * sp + pltpu.roll(xb, LANES - shift, 1) * sm


def _qkv_body(has_peer, *refs):
    if has_peer:
        x_ref, peer_ref, g2_ref = refs[:3]
        refs = refs[3:]
    else:
        x_ref = refs[0]
        refs = refs[1:]
    (ng_ref, sc_ref, sh_ref, win_ref, gq_ref, gkv_ref, wq_ref, wk_ref, wv_ref, tab_ref) = refs[:10]
    outs = refs[10:]
    if has_peer:
        xo_ref, outs = outs[0], outs[1:]
    qm_ref, km_ref, vm_ref, qd_ref, kd_ref, vd_ref = outs

    x = x_ref[...]
    if has_peer:
        x = x + g2_ref[0] * _tile_rows_load(peer_ref)
        xo_ref[...] = x
    h = _rms(x) * ng_ref[...] * (1.0 + sc_ref[0]) + sh_ref[0]
    proj = jnp.dot(h.astype(BF16), win_ref[...], preferred_element_type=F32)
    cq = _rms(proj[:, 0:384]) * gq_ref[...]
    ckv = _rms(proj[:, 384:640]) * gkv_ref[...]
    kr = proj[:, 640:768]
    q = jnp.dot(cq.astype(BF16), wq_ref[...], preferred_element_type=F32)
    ckv16 = ckv.astype(BF16)
    k = jnp.dot(ckv16, wk_ref[...], preferred_element_type=F32)
    v = jnp.dot(ckv16, wv_ref[...], preferred_element_type=F32)
    tab = tab_ref[...]
    cm, spm, smm = tab[:, 0:128], tab[:, 128:256], tab[:, 256:384]
    cd, spd, smd = tab[:, 384:512], tab[:, 512:640], tab[:, 640:768]
    kr_rot = _rope(kr, cm, spm, smm, MLA_ROPE_DIM // 2)
    scale_m = LOG2_E * (MLA_HEAD_DIM + MLA_ROPE_DIM) ** -0.5
    for hh in range(MLA_HEADS):
        sl = slice(hh * LANES, (hh + 1) * LANES)
        qm_ref[:, sl] = (_rope(q[:, sl], cm, spm, smm, MLA_ROPE_DIM // 2) * scale_m).astype(BF16)
        km_ref[:, sl] = (k[:, sl] + kr_rot).astype(BF16)
    vm_ref[...] = v.T.astype(BF16)
    scale_d = LOG2_E * DIFF_HEAD_DIM ** -0.5
    for hh in range(DIFF_HEADS):
        sl = slice(hh * LANES, (hh + 1) * LANES)
        qd = proj[:, 768 + hh * LANES:768 + (hh + 1) * LANES]
        kd = proj[:, 1280 + hh * LANES:1280 + (hh + 1) * LANES]
        qd_ref[:, sl] = (_rope(qd, cd, spd, smd, DIFF_HEAD_DIM // 2) * scale_d).astype(BF16)
        kd_ref[:, sl] = _rope(kd, cd, spd, smd, DIFF_HEAD_DIM // 2).astype(BF16)
    vd_ref[...] = proj[:, 1792:2304].T.astype(BF16)


def _qkv(x, peer, g2, ng, sc, sh, win, gq, gkv, wq, wk, wv, tab):
    has_peer = peer is not None
    row = lambda i: (i, 0)
    modrow = lambda i: (_bid(i), 0, 0)
    const = lambda i: (0, 0)
    in_specs = [pl.BlockSpec((TM, D_MODEL), row)]
    args = [x]
    if has_peer:
        in_specs += [pl.BlockSpec((TM * SUBLANES, LANES), row), pl.BlockSpec((1, 1, D_MODEL), modrow)]
        args += [peer, g2]
    in_specs += [
        pl.BlockSpec((1, D_MODEL), const),
        pl.BlockSpec((1, 1, D_MODEL), modrow),
        pl.BlockSpec((1, 1, D_MODEL), modrow),
        pl.BlockSpec((D_MODEL, PROJ_PAD), const),
        pl.BlockSpec((1, MLA_Q_RANK), const),
        pl.BlockSpec((1, MLA_KV_RANK), const),
        pl.BlockSpec((MLA_Q_RANK, MLA_HEADS * LANES), const),
        pl.BlockSpec((MLA_KV_RANK, MLA_HEADS * LANES), const),
        pl.BlockSpec((MLA_KV_RANK, 512), const),
        pl.BlockSpec((TM, 768), lambda i: (_posblk(i), 0)),
    ]
    args += [ng, sc, sh, win, gq, gkv, wq, wk, wv, tab]
    out_specs, out_shape = [], []
    if has_peer:
        out_specs.append(pl.BlockSpec((TM, D_MODEL), row))
        out_shape.append(jax.ShapeDtypeStruct((TT, D_MODEL), F32))
    for width, transposed in ((1024, False), (1024, False), (512, True),
                              (512, False), (512, False), (512, True)):
        if transposed:
            out_specs.append(pl.BlockSpec((width, TM), lambda i: (0, i)))
            out_shape.append(jax.ShapeDtypeStruct((width, TT), BF16))
        else:
            out_specs.append(pl.BlockSpec((TM, width), row))
            out_shape.append(jax.ShapeDtypeStruct((TT, width), BF16))
    res = pl.pallas_call(
        functools.partial(_qkv_body, has_peer),
        grid=(NB_ALL,),
        in_specs=in_specs,
        out_specs=out_specs,
        out_shape=out_shape,
        compiler_params=_cparams(("parallel",)),
        name="qkv",
    )(*args)
    if has_peer:
        return res[0], res[1:]
    return x, res


def _flash_pair(tq, qs, ksls, vsls, kc_ref, kl_ref, vtc_ref, vtl_ref, n_pairs, sa_scr, sb_scr):
    n_lat = SEQ // TK

    def col_reduce(parts, op, final):
        while len(parts) > 1:
            parts = [op(parts[i], parts[i + 1]) for i in range(0, len(parts), 2)]
        return final(parts[0], axis=0, keepdims=True)

    def row_groups(x):
        return [x[i:i + SUBLANES] for i in range(0, x.shape[0], SUBLANES)]

    def scores(k_of):
        return [lax.dot_general(k_of(ksl), q, (((1,), (1,)), ((), ())), preferred_element_type=F32)
                for q, ksl in zip(qs, ksls)]

    def softmax_pv(read_s, vt_of, carries):
        mid = []
        for h, (m, l, acc) in enumerate(carries):
            m_new = jnp.maximum(m, col_reduce(row_groups(read_s(h)), jnp.maximum, jnp.max))
            alpha = jnp.exp2(m - m_new)
            p = jnp.exp2(read_s(h) - m_new)
            l = alpha * l + col_reduce(row_groups(p), jnp.add, jnp.sum)
            mid.append((m_new, l, alpha * acc, p.astype(BF16)))
        return tuple((m_new, l, acc + jnp.dot(vt_of(vsl), p, preferred_element_type=F32))
                     for (m_new, l, acc, p), vsl in zip(mid, vsls))

    def k_lat(c):
        return lambda ksl: kl_ref[c * TK:(c + 1) * TK, ksl]

    def vt_lat(c):
        return lambda vsl: vtl_ref[vsl, c * TK:(c + 1) * TK]

    def stage(scr, sc):
        for h, x in enumerate(sc):
            scr[h] = x

    init = tuple((jnp.full((1, tq), -jnp.inf, F32), jnp.zeros((1, tq), F32),
                  jnp.zeros((vsl.stop - vsl.start, tq), F32)) for vsl in vsls)
    ctx_scores = scores(lambda ksl: kc_ref[:, ksl])
    if n_pairs:
        stage(sa_scr, scores(k_lat(0)))
    carries = softmax_pv(lambda h: ctx_scores[h], lambda vsl: vtc_ref[vsl, :], init)

    def pair(j, carries):
        c = 2 * j
        stage(sb_scr, scores(k_lat(c + 1)))
        carries = softmax_pv(lambda h: sa_scr[h], vt_lat(c), carries)
        if c + 2 < n_lat:
            stage(sa_scr, scores(k_lat(c + 2)))
        return softmax_pv(lambda h: sb_scr[h], vt_lat(c + 1), carries)

    for j in range(n_pairs):
        carries = pair(j, carries)
    return [acc / l for (m, l, acc) in carries]


def _attn_mla_body(tq, n_pairs, *refs):
    if n_pairs:
        q_ref, kc_ref, kl_ref, vtc_ref, vtl_ref, o_ref, sa_scr, sb_scr = refs
    else:
        q_ref, kc_ref, vtc_ref, o_ref = refs
        kl_ref = vtl_ref = sa_scr = sb_scr = None
    qs = [q_ref[:, 0:LANES], q_ref[:, LANES:2 * LANES]]
    ksls = [slice(0, LANES), slice(LANES, 2 * LANES)]
    vsls = [slice(0, MLA_HEAD_DIM), slice(MLA_HEAD_DIM, 2 * MLA_HEAD_DIM)]
    ot0, ot1 = _flash_pair(tq, qs, ksls, vsls, kc_ref, kl_ref, vtc_ref, vtl_ref, n_pairs, sa_scr, sb_scr)
    o_ref[...] = jnp.concatenate([ot0, ot1], axis=0).T.astype(o_ref.dtype)


def _attn_diff_body(tq, n_pairs, lam_scale, *refs):
    if n_pairs:
        q_ref, kc_ref, kl_ref, vtc_ref, vtl_ref, lam_ref, g_ref, o_ref, sa_scr, sb_scr = refs
    else:
        q_ref, kc_ref, vtc_ref, lam_ref, g_ref, o_ref = refs
        kl_ref = vtl_ref = sa_scr = sb_scr = None
    lane = lax.broadcasted_iota(jnp.int32, (tq, LANES), 1)
    q = q_ref[...]
    zero = jnp.zeros_like(q)
    qs = [jnp.where(lane < DIFF_HEAD_DIM, q, zero), jnp.where(lane < DIFF_HEAD_DIM, zero, q)]
    ksls = [slice(0, LANES), slice(0, LANES)]
    vsls = [slice(0, LANES), slice(0, LANES)]
    ot0, ot1 = _flash_pair(tq, qs, ksls, vsls, kc_ref, kl_ref, vtc_ref, vtl_ref, n_pairs, sa_scr, sb_scr)
    o = ot0.T - lam_ref[...] * ot1.T
    o_ref[...] = (_rms(o) * g_ref[...] * lam_scale).astype(o_ref.dtype)


def _attention(kind, q, k, vt, ctx_queries, lam=None, g=None, lam_scale=None):
    qw = 2 * LANES if kind == "mla" else LANES
    ctx0 = T_LAT // CTX_LEN
    if ctx_queries:
        tq, n_pairs, steps, rows = CTX_LEN, 0, 1, T_CTX
        q_map = lambda b, p, i: (ctx0 + b, p)
        o_map = lambda b, p, i: (b, p)
    else:
        tq, n_pairs, steps, rows = TQ, SEQ // (2 * TK), SEQ // TQ, T_LAT
        q_map = lambda b, p, i: (b * steps + i, p)
        o_map = q_map
    kc_spec = pl.BlockSpec((CTX_LEN, qw), lambda b, p, i: (ctx0 + b, p))
    vtc_spec = pl.BlockSpec((LANES, CTX_LEN), lambda b, p, i: (p, ctx0 + b))
    in_specs = [pl.BlockSpec((tq, qw), q_map), kc_spec]
    args = [q, k]
    if n_pairs:
        in_specs += [pl.BlockSpec((SEQ, qw), lambda b, p, i: (b, p)), vtc_spec,
                     pl.BlockSpec((LANES, SEQ), lambda b, p, i: (p, b))]
        args += [k, vt, vt]
    else:
        in_specs += [vtc_spec]
        args += [vt]
    if kind == "mla":
        body = functools.partial(_attn_mla_body, tq, n_pairs)
    else:
        body = functools.partial(_attn_diff_body, tq, n_pairs, lam_scale)
        in_specs += [pl.BlockSpec((1, LANES), lambda b, p, i: (0, 0)),
                     pl.BlockSpec((1, LANES), lambda b, p, i: (0, 0))]
        args += [lam, g]
    scratch = [pltpu.VMEM((2, TK, tq), F32), pltpu.VMEM((2, TK, tq), F32)] if n_pairs else []
    return pl.pallas_call(
        body,
        grid=(BATCH, 4, steps),
        in_specs=in_specs,
        out_specs=pl.BlockSpec((tq, LANES), o_map),
        out_shape=jax.ShapeDtypeStruct((rows, 512), BF16),
        scratch_shapes=scratch,
        compiler_params=_cparams(("parallel", "parallel", "arbitrary")),
        name="attn_" + kind + ("_ctx" if ctx_queries else ""),
    )(*args)


HEADS_PER_TRIP = 4


def _top16(s, payload=None):
    n_rows = s.shape[0]
    rowf = lax.broadcasted_iota(jnp.int32, s.shape, 0).astype(F32)
    slot = lax.broadcasted_iota(jnp.int32, (PEER_TOPK, s.shape[1]), 0)
    vals = jnp.zeros((PEER_TOPK, s.shape[1]), F32)
    picks = jnp.zeros((PEER_TOPK, s.shape[1]), F32)
    for r in range(PEER_TOPK):
        m = jnp.max(s, axis=0, keepdims=True)
        am = jnp.min(jnp.where(s == m, rowf, float(n_rows)), axis=0, keepdims=True)
        hit = rowf == am
        pick = am if payload is None else jnp.max(jnp.where(hit, payload, -1.0), axis=0, keepdims=True)
        vals = jnp.where(slot == r, m, vals)
        picks = jnp.where(slot == r, pick, picks)
        s = jnp.where(hit, -jnp.inf, s)
    return vals, picks


def _staircase(a16, b16, combine, pad):
    tm = a16.shape[1]
    sub = lax.broadcasted_iota(jnp.int32, (SUBLANES, tm), 0)
    a_lo, a_hi = a16[0:SUBLANES], a16[SUBLANES:]
    b_lo, b_hi = b16[0:SUBLANES], b16[SUBLANES:]
    row = lambda x, r: jnp.broadcast_to(x[r:r + 1], (SUBLANES, tm))
    take = lambda x, idx: jnp.take_along_axis(x, idx, axis=0)
    a3 = jnp.where(sub < 5, 2, 3)
    b3 = jnp.where(sub < 5, sub, sub - 5)
    a4 = jnp.where(sub < 1, 3, jnp.where(sub < 4, 4, jnp.where(sub < 6, 5, 6)))
    b4 = jnp.where(sub < 1, 3, jnp.where(sub < 4, sub - 1, jnp.where(sub < 6, sub - 4, sub - 6)))
    pieces = [
        combine(row(a_lo, 0), b_lo),
        combine(row(a_lo, 0), b_hi),
        combine(row(a_lo, 1), b_lo),
        combine(take(a_lo, a3), take(b_lo, b3)),
        combine(take(a_lo, a4), take(b_lo, b4)),
        jnp.where(sub < 2, combine(row(a_lo, 7), b_lo), pad),
        combine(a_hi, row(b_lo, 0)),
    ]
    return jnp.concatenate(pieces, axis=0)


def _mid_body(x_ref, om_ref, od_ref, wo_m_ref, wo_d_ref, g1_ref, ng_ref, sc_ref, sh_ref,
              wq_ref, sk_ref, xo_ref, h_ref, eidx_ref, gate_ref, q_scr):
    y = (jnp.dot(om_ref[...], wo_m_ref[...], preferred_element_type=F32)
         + jnp.dot(od_ref[...], wo_d_ref[...], preferred_element_type=F32))
    x = x_ref[...] + g1_ref[0] * y
    xo_ref[...] = x
    h = _rms(x) * ng_ref[...] * (1.0 + sc_ref[0]) + sh_ref[0]
    _tile_rows_store(h_ref, h)
    q_scr[...] = jnp.dot(h.astype(BF16), wq_ref[...], preferred_element_type=F32).astype(BF16)

    def head(hh):
        sv, si = [], []
        for c in range(2):
            c0 = pl.multiple_of(hh * 2 * N_KEYS + c * N_KEYS, N_KEYS)
            qh = q_scr[:, pl.ds(c0, N_KEYS)]
            s = lax.dot_general(sk_ref[c], qh, (((1,), (1,)), ((), ())), preferred_element_type=F32)
            vals, keys = _top16(s)
            sv.append(vals)
            si.append(keys)
        cand_s = _staircase(sv[0], sv[1], lambda a, b: a + b, -jnp.inf)
        cand_e = _staircase(si[0], si[1], lambda a, b: a * float(N_KEYS) + b, -1.0)
        top_s, top_e = _top16(cand_s, cand_e)
        ex = jnp.exp(top_s - jnp.max(top_s, axis=0, keepdims=True))
        gate = ex / jnp.sum(ex, axis=0, keepdims=True)
        r0 = pl.multiple_of(hh * PEER_TOPK, PEER_TOPK)
        eidx_ref[pl.ds(r0, PEER_TOPK), :] = top_e.astype(jnp.int32) * ROWS_PER_EXPERT
        gate_ref[pl.ds(r0, PEER_TOPK), :] = gate

    def head_group(j, _):
        for i in range(HEADS_PER_TRIP):
            head(HEADS_PER_TRIP * j + i)
        return 0

    lax.fori_loop(0, PEER_HEADS // HEADS_PER_TRIP, head_group, 0)


def _mid(n_blocks, x, om, od, wo_m, wo_d, g1, ng, sc, sh, wq, sk):
    n = n_blocks * TM
    row = lambda i: (i, 0)
    modrow = lambda i: (_bid(i), 0, 0)
    const = lambda i: (0, 0)
    return pl.pallas_call(
        _mid_body,
        grid=(n_blocks,),
        in_specs=[
            pl.BlockSpec((TM, D_MODEL), row),
            pl.BlockSpec((TM, 512), row),
            pl.BlockSpec((TM, 512), row),
            pl.BlockSpec((512, D_MODEL), const),
            pl.BlockSpec((512, D_MODEL), const),
            pl.BlockSpec((1, 1, D_MODEL), modrow),
            pl.BlockSpec((1, D_MODEL), const),
            pl.BlockSpec((1, 1, D_MODEL), modrow),
            pl.BlockSpec((1, 1, D_MODEL), modrow),
            pl.BlockSpec((D_MODEL, PEER_HEADS * 2 * N_KEYS), const),
            pl.BlockSpec((2, N_KEYS, N_KEYS), lambda i: (0, 0, 0)),
        ],
        out_specs=[
            pl.BlockSpec((TM, D_MODEL), row),
            pl.BlockSpec((TM * SUBLANES, LANES), row),
            pl.BlockSpec((PEER_HEADS * PEER_TOPK, TM), lambda i: (0, i)),
            pl.BlockSpec((PEER_HEADS * PEER_TOPK, TM), lambda i: (0, i)),
        ],
        out_shape=[
            jax.ShapeDtypeStruct((n, D_MODEL), F32),
            jax.ShapeDtypeStruct((n * SUBLANES, LANES), F32),
            jax.ShapeDtypeStruct((PEER_HEADS * PEER_TOPK, n), jnp.int32),
            jax.ShapeDtypeStruct((PEER_HEADS * PEER_TOPK, n), F32),
        ],
        scratch_shapes=[pltpu.VMEM((TM, PEER_HEADS * 2 * N_KEYS), BF16)],
        compiler_params=_cparams(("parallel",)),
        name="mid",
    )(x, om, od, wo_m, wo_d, g1, ng, sc, sh, wq, sk)


N_SEL = PEER_HEADS * PEER_TOPK
HALF = D_MODEL // 2
ROWS_PER_EXPERT = HALF // LANES


PACK_BLOCK = 512


def _pack_body(t_ref, o_ref):
    u = pltpu.bitcast(t_ref[0], jnp.uint32)
    r = u + jnp.uint32(0x7FFF) + ((u >> 16) & jnp.uint32(1))
    w = (r[:, :HALF] >> 16) | (r[:, HALF:] & jnp.uint32(0xFFFF0000))
    for s in range(ROWS_PER_EXPERT):
        o_ref[pl.ds(s, PACK_BLOCK, stride=ROWS_PER_EXPERT), :] = w[:, s * LANES:(s + 1) * LANES]


def _pack_table(tabs, layer):
    n = tabs.shape[1]
    return pl.pallas_call(
        _pack_body,
        grid=(n // PACK_BLOCK,),
        in_specs=[pl.BlockSpec((1, PACK_BLOCK, D_MODEL), lambda i: (layer, i, 0))],
        out_specs=pl.BlockSpec((PACK_BLOCK * ROWS_PER_EXPERT, LANES), lambda i: (i, 0)),
        out_shape=jax.ShapeDtypeStruct((n * ROWS_PER_EXPERT, LANES), jnp.uint32),
        compiler_params=_cparams(("parallel",)),
        name="pack_table",
    )(tabs)


def _load_pair(tab_ref, ra, rb):
    w = jnp.concatenate([tab_ref[pl.ds(pl.multiple_of(ra, ROWS_PER_EXPERT), ROWS_PER_EXPERT), :],
                         tab_ref[pl.ds(pl.multiple_of(rb, ROWS_PER_EXPERT), ROWS_PER_EXPERT), :]], axis=0)
    lo = pltpu.bitcast(w << 16, F32)
    hi = pltpu.bitcast(w & jnp.uint32(0xFFFF0000), F32)
    return lo, hi


def _with_index_buffers(idx_hbm, idx_a, idx_b, sem, first, second):
    step = pl.program_id(0)
    words = PEER_SB * N_SEL

    def idx_copy(block, dst, k):
        return pltpu.make_async_copy(idx_hbm.at[pl.ds(block * words, words)], dst, sem.at[k])

    @pl.when(step == 0)
    def _():
        idx_copy(0, idx_a, 0).start()

    idx_copy(2 * step + 1, idx_b, 1).start()
    idx_copy(2 * step, idx_a, 0).wait()
    first(idx_a)

    @pl.when(step + 1 < pl.num_programs(0))
    def _():
        idx_copy(2 * step + 2, idx_a, 0).start()

    idx_copy(2 * step + 1, idx_b, 1).wait()
    second(idx_b)


def _peer_u_body(idx_hbm, x_ref, gate_ref, tab_ref, act_ref, idx_a, idx_b, r_scr, score_scr, sem):
    sub = lax.broadcasted_iota(jnp.int32, (SUBLANES, LANES), 0)
    lane = lax.broadcasted_iota(jnp.int32, (SUBLANES, LANES), 1)
    top = sub < 4
    n_pair = N_SEL // 2
    quarter = pl.program_id(0) % STEPS_PER_ACT_BLOCK

    def gather_token(idx_s, tok, tok_in_buf):
        xt = x_ref[tok]
        xsw = pltpu.roll(xt, 4, 0)
        xlo = jnp.where(top, xt, xsw)
        xhi = jnp.where(top, xsw, xt)
        for p in range(n_pair):
            ra = idx_s[tok_in_buf * N_SEL + 2 * p]
            rb = idx_s[tok_in_buf * N_SEL + 2 * p + 1]
            lo, hi = _load_pair(tab_ref, ra, rb)
            r_scr[tok % R_SLOTS * n_pair + p] = lo * xlo + hi * xhi

    def reduce_token(tok):
        token_lane = quarter * (2 * PEER_SB) + tok
        for p in range(n_pair):
            col = jnp.sum(r_scr[tok % R_SLOTS * n_pair + p], axis=1, keepdims=True)
            score_scr[p] = jnp.where(lane == token_lane, col, score_scr[p])

    @pl.when(quarter == 0)
    def _():
        score_scr[...] = jnp.zeros(score_scr.shape, F32)

    def first(idx_s):
        for i in range(PEER_SB):
            gather_token(idx_s, i, i)
            if i:
                reduce_token(i - 1)

    def second(idx_s):
        for i in range(PEER_SB):
            gather_token(idx_s, PEER_SB + i, i)
            reduce_token(PEER_SB + i - 1)
        reduce_token(2 * PEER_SB - 1)

    _with_index_buffers(idx_hbm, idx_a, idx_b, sem, first, second)

    @pl.when(quarter == STEPS_PER_ACT_BLOCK - 1)
    def _():
        for g in range(N_SEL // SUBLANES):
            scores = jnp.zeros((SUBLANES, LANES), F32)
            for pp in range(4):
                s = score_scr[4 * g + pp]
                t = s + pltpu.roll(s, 6, 0)
                t = t + pltpu.roll(t, 7, 0)
                scores = jnp.where(sub == 2 * pp, pltpu.roll(t, 2 * pp, 0), scores)
                scores = jnp.where(sub == 2 * pp + 1, pltpu.roll(t, (2 * pp + 5) % SUBLANES, 0), scores)
            act_ref[g] = 0.5 * scores * (1.0 + lax.erf(scores * (2.0 ** -0.5))) * gate_ref[g]


def _peer_v_body(idx_hbm, act_ref, tab_ref, o_ref, idx_a, idx_b, sem):
    sub = lax.broadcasted_iota(jnp.int32, (SUBLANES, LANES), 0)
    lane = lax.broadcasted_iota(jnp.int32, (SUBLANES, LANES), 1)
    top = sub < 4
    n_grp = N_SEL // SUBLANES
    pair_rows = [jnp.where(top, 2 * p, 2 * p + 1) for p in range(4)]

    quarter = pl.program_id(0) % STEPS_PER_ACT_BLOCK

    def process(idx_s, half):
        for tile in range(PEER_SB // SUBLANES):
            for tt in range(SUBLANES):
                tl = tile * SUBLANES + tt
                token_lane = quarter * (2 * PEER_SB) + half * PEER_SB + tl
                acc = [jnp.zeros((SUBLANES, LANES), F32) for _ in range(4)]
                for g in range(n_grp):
                    col = jnp.sum(jnp.where(lane == token_lane, act_ref[g], 0.0), axis=1, keepdims=True)
                    actg = jnp.broadcast_to(col, (SUBLANES, LANES))
                    for p in range(4):
                        ra = idx_s[tl * N_SEL + g * SUBLANES + 2 * p]
                        rb = idx_s[tl * N_SEL + g * SUBLANES + 2 * p + 1]
                        lo, hi = _load_pair(tab_ref, ra, rb)
                        av = jnp.take_along_axis(actg, pair_rows[p], axis=0)
                        k = 2 * (p % 2)
                        acc[k] = acc[k] + av * lo
                        acc[k + 1] = acc[k + 1] + av * hi
                lo = acc[0] + acc[2]
                hi = acc[1] + acc[3]
                lo = lo + pltpu.roll(lo, 4, 0)
                hi = hi + pltpu.roll(hi, 4, 0)
                o_ref[half * PEER_SB + tl] = jnp.where(top, lo, hi)

    _with_index_buffers(idx_hbm, idx_a, idx_b, sem,
                        lambda idx_s: process(idx_s, 0), lambda idx_s: process(idx_s, 1))


def _table_spec():
    return pl.BlockSpec((N_EXPERTS * ROWS_PER_EXPERT, LANES), lambda i: (0, 0),
                        pipeline_mode=pl.Buffered(1))


R_SLOTS = 3
STEPS_PER_ACT_BLOCK = LANES // (2 * PEER_SB)


def _act_block_spec():
    return pl.BlockSpec((N_SEL // SUBLANES, SUBLANES, LANES), lambda i: (0, 0, i // STEPS_PER_ACT_BLOCK))


def _peer_idx_scratch():
    return [pltpu.SMEM((PEER_SB * N_SEL,), jnp.int32), pltpu.SMEM((PEER_SB * N_SEL,), jnp.int32)]


def _peer_u(n, idx, x3, gate3, tab):
    tb = 2 * PEER_SB
    return pl.pallas_call(
        _peer_u_body,
        grid=(n // tb,),
        in_specs=[
            pl.BlockSpec(memory_space=pl.ANY),
            pl.BlockSpec((tb, SUBLANES, LANES), lambda i: (i, 0, 0)),
            _act_block_spec(),
            _table_spec(),
        ],
        out_specs=_act_block_spec(),
        out_shape=jax.ShapeDtypeStruct((N_SEL // SUBLANES, SUBLANES, n), F32),
        scratch_shapes=_peer_idx_scratch() + [
            pltpu.VMEM((R_SLOTS * (N_SEL // 2), SUBLANES, LANES), F32),
            pltpu.VMEM((N_SEL // 2, SUBLANES, LANES), F32),
            pltpu.SemaphoreType.DMA((2,))],
        compiler_params=_cparams(("arbitrary",)),
        name="peer_u",
    )(idx.reshape(-1), x3, gate3, tab)


def _peer_v(n, idx, act3, tab):
    tb = 2 * PEER_SB
    return pl.pallas_call(
        _peer_v_body,
        grid=(n // tb,),
        in_specs=[
            pl.BlockSpec(memory_space=pl.ANY),
            _act_block_spec(),
            _table_spec(),
        ],
        out_specs=pl.BlockSpec((tb, SUBLANES, LANES), lambda i: (i, 0, 0)),
        out_shape=jax.ShapeDtypeStruct((n, SUBLANES, LANES), F32),
        scratch_shapes=_peer_idx_scratch() + [pltpu.SemaphoreType.DMA((2,))],
        compiler_params=_cparams(("arbitrary",)),
        name="peer_v",
    )(idx.reshape(-1), act3, tab)


def _final_body(x_ref, peer_ref, g2_ref, g_ref, o_ref):
    x = x_ref[...] + g2_ref[0] * _tile_rows_load(peer_ref)
    o_ref[...] = _rms(x) * g_ref[...]


def _final(x, peer, g2, g):
    row = lambda i: (i, 0)
    return pl.pallas_call(
        _final_body,
        grid=(NB_LAT,),
        in_specs=[
            pl.BlockSpec((TM, D_MODEL), row),
            pl.BlockSpec((TM * SUBLANES, LANES), row),
            pl.BlockSpec((1, 1, D_MODEL), lambda i: (_bid(i), 0, 0)),
            pl.BlockSpec((1, D_MODEL), lambda i: (0, 0)),
        ],
        out_specs=pl.BlockSpec((TM, D_MODEL), row),
        out_shape=jax.ShapeDtypeStruct((T_LAT, D_MODEL), F32),
        compiler_params=_cparams(("parallel",)),
        name="final_norm",
    )(x, peer, g2, g)


def _deinterleave(width):
    return np.concatenate([np.arange(0, width, 2), np.arange(1, width, 2)])


def _rope_tables():
    pos = np.arange(SEQ)
    rowp = jnp.asarray(pos // GRID_W, F32)
    colp = jnp.asarray(pos % GRID_W, F32)

    def angles(dim):
        quarter = dim // 4
        inv = ROPE_BASE ** (-jnp.arange(quarter, dtype=F32) / quarter)
        return jnp.concatenate([rowp[:, None] * inv, colp[:, None] * inv], axis=-1)

    am = angles(MLA_ROPE_DIM)
    ad = angles(DIFF_HEAD_DIM)
    cm, sm_ = jnp.cos(am), jnp.sin(am)
    cd, sd = jnp.cos(ad), jnp.sin(ad)
    one = lambda w: jnp.ones((SEQ, w), F32)
    zero = lambda w: jnp.zeros((SEQ, w), F32)
    t_cm = jnp.concatenate([one(64), cm, cm, one(32)], axis=1)
    t_spm = jnp.concatenate([zero(80), sm_, zero(32)], axis=1)
    t_smm = jnp.concatenate([zero(64), -sm_, zero(48)], axis=1)
    t_cd = jnp.concatenate([cd, cd, cd, cd], axis=1)
    t_spd = jnp.concatenate([zero(32), sd, zero(32), sd], axis=1)
    t_smd = jnp.concatenate([-sd, zero(32), -sd, zero(32)], axis=1)
    lat = jnp.concatenate([t_cm, t_spm, t_smm, t_cd, t_spd, t_smd], axis=1)
    ident = jnp.concatenate([jnp.ones((CTX_LEN, 128), F32), jnp.zeros((CTX_LEN, 256), F32),
                             jnp.ones((CTX_LEN, 128), F32), jnp.zeros((CTX_LEN, 256), F32)], axis=1)
    return jnp.concatenate([lat, ident], axis=0)


def _prep_layer_weights(w_in, wq_up, wq_rope, wk_up, wv_up, w_out, peer_wq, peer_subkeys):
    p32 = _deinterleave(MLA_ROPE_DIM)
    p64 = _deinterleave(DIFF_HEAD_DIM)
    z = lambda w: jnp.zeros((D_MODEL, w), F32)
    kr = w_in[:, 640:672][:, p32]
    qd = w_in[:, 672:1184].reshape(D_MODEL, 8, 64)[:, :, p64].reshape(D_MODEL, 512)
    kd = w_in[:, 1184:1696].reshape(D_MODEL, 8, 64)[:, :, p64].reshape(D_MODEL, 512)
    win = jnp.concatenate([w_in[:, 0:640], z(64), kr, z(32), qd, kd, w_in[:, 1696:2208]], axis=1)
    qn = wq_up.reshape(MLA_Q_RANK, MLA_HEADS, MLA_HEAD_DIM)
    qr = wq_rope.reshape(MLA_Q_RANK, MLA_HEADS, MLA_ROPE_DIM)[:, :, p32]
    wq = jnp.concatenate([qn, qr, jnp.zeros((MLA_Q_RANK, MLA_HEADS, 32), F32)], axis=2)
    kn = wk_up.reshape(MLA_KV_RANK, MLA_HEADS, MLA_HEAD_DIM)
    wk = jnp.concatenate([kn, jnp.zeros((MLA_KV_RANK, MLA_HEADS, 64), F32)], axis=2)
    return dict(
        win=win.astype(BF16),
        wq=wq.reshape(MLA_Q_RANK, MLA_HEADS * LANES).astype(BF16),
        wk=wk.reshape(MLA_KV_RANK, MLA_HEADS * LANES).astype(BF16),
        wv=wv_up.astype(BF16),
        wo_m=w_out[:512].astype(BF16),
        wo_d=w_out[512:].astype(BF16),
        pwq=peer_wq.astype(BF16),
        sk=peer_subkeys.astype(BF16),
    )


def kernel(x, c, ctx, c_ctx, norm_attn_g, norm_ffn_g, w_ada, b_ada, w_in, mla_q_norm_g, mla_wq_up, mla_wq_rope, mla_kv_norm_g, mla_wk_up, mla_wv_up, diff_lambda, diff_subnorm_g, w_out, peer_wq, peer_subkeys, peer_u, peer_v, final_norm_g):
    xs = jnp.concatenate([x.reshape(T_LAT, D_MODEL), ctx.reshape(T_CTX, D_MODEL)], axis=0)
    cc = jnp.concatenate([c, c_ctx[None, :], jnp.zeros((MOD_ROWS - BATCH - 1, D_MODEL), F32)], axis=0)
    mod = _modulation(cc, w_ada, b_ada)
    lam_inits = [0.8 - 0.6 * math.exp(-0.3 * l) for l in range(DEPTH)]
    lam_all = _diff_lambda(diff_lambda, jnp.broadcast_to(jnp.asarray(lam_inits, F32)[:, None], (DEPTH, LANES)))
    tab = _rope_tables()

    peer_out = None
    g2_prev = None
    for l in range(DEPTH):
        last = l == DEPTH - 1
        w = _prep_layer_weights(w_in[l], mla_wq_up[l], mla_wq_rope[l], mla_wk_up[l], mla_wv_up[l],
                                w_out[l], peer_wq[l], peer_subkeys[l])
        m = mod[l].reshape(MOD_ROWS, N_MOD, 1, D_MODEL)
        sh1, sc1, g1, sh2, sc2, g2 = (m[:, j] for j in range(N_MOD))
        xs, (qm, km, vm, qd, kd, vd) = _qkv(
            xs, peer_out, g2_prev, norm_attn_g[l][None, :], sc1, sh1, w["win"],
            mla_q_norm_g[l][None, :], mla_kv_norm_g[l][None, :], w["wq"], w["wk"], w["wv"], tab)
        diff_args = dict(lam=lam_all[l][None, :], g=diff_subnorm_g[l][None, :],
                         lam_scale=1.0 - lam_inits[l])
        om = _attention("mla", qm, km, vm, False)
        od = _attention("diff", qd, kd, vd, False, **diff_args)
        if not last:
            om = jnp.concatenate([om, _attention("mla", qm, km, vm, True)], axis=0)
            od = jnp.concatenate([od, _attention("diff", qd, kd, vd, True, **diff_args)], axis=0)
        n_blocks = NB_LAT if last else NB_ALL
        n = n_blocks * TM
        xs, h2, eidx_t, gate_t = _mid(n_blocks, xs, om, od, w["wo_m"], w["wo_d"], g1,
                                      norm_ffn_g[l][None, :], sc2, sh2, w["pwq"], w["sk"])
        idx = eidx_t.T
        gate3 = gate_t.reshape(N_SEL // SUBLANES, SUBLANES, n)
        act3 = _peer_u(n, idx, h2.reshape(n, SUBLANES, LANES), gate3, _pack_table(peer_u, l))
        peer_out = _peer_v(n, idx, act3, _pack_table(peer_v, l)).reshape(n * SUBLANES, LANES)
        g2_prev = g2
    out = _final(xs, peer_out, g2_prev, final_norm_g[None, :])
    return out.reshape(BATCH, SEQ, D_MODEL)
```

```python
import functools
import math

import jax
import jax.numpy as jnp
import numpy as np
from jax import lax
from jax.experimental import pallas as pl
from jax.experimental.pallas import tpu as pltpu

F32 = jnp.float32
BF16 = jnp.bfloat16

D_MODEL = 1024
BATCH = 8
SEQ = 4096
DEPTH = 4
GRID_W = 64
CTX_LEN = 256
N_MOD = 6
NORM_EPS = 1e-6
ROPE_BASE = 10000.0
MLA_HEADS = 8
MLA_HEAD_DIM = 64
MLA_ROPE_DIM = 32
MLA_Q_RANK = 384
MLA_KV_RANK = 256
DIFF_HEADS = 4
DIFF_HEAD_DIM = 64
PEER_HEADS = 8
PEER_TOPK = 16
N_KEYS = 128
N_EXPERTS = N_KEYS * N_KEYS

T_LAT = BATCH * SEQ
T_CTX = BATCH * CTX_LEN
TT = T_LAT + T_CTX
TM = 256
NB_LAT = T_LAT // TM
NB_ALL = TT // TM
BLK_PER_BATCH = SEQ // TM
MOD_ROWS = 16
PROJ_PAD = 2304
TQ = 256
TK = 512
P_ROWS = 128
PEER_SB = 32
LANES = 128
SUBLANES = 8
VMEM_LIMIT = 56 * 1024 * 1024
LOG2_E = math.log2(math.e)


def _cparams(sem, vmem=VMEM_LIMIT):
    return pltpu.CompilerParams(dimension_semantics=sem, vmem_limit_bytes=vmem)


def _rms(x):
    return x * lax.rsqrt(jnp.mean(x * x, axis=-1, keepdims=True) + NORM_EPS)


def _tile_rows_load(ref):
    return jnp.concatenate([ref[pl.ds(s, TM, stride=SUBLANES), :] for s in range(SUBLANES)], axis=1)


def _tile_rows_store(ref, x):
    for s in range(SUBLANES):
        ref[pl.ds(s, TM, stride=SUBLANES), :] = x[:, s * LANES:(s + 1) * LANES]


def _bid(i):
    return jnp.where(i < NB_LAT, i // BLK_PER_BATCH, BATCH)


def _posblk(i):
    return jnp.where(i < NB_LAT, i % BLK_PER_BATCH, BLK_PER_BATCH)


def _mod_body(cc_ref, w_ref, b_ref, o_ref):
    cc = cc_ref[...]
    s = cc / (1.0 + jnp.exp(-cc))
    o_ref[0] = jnp.dot(s, w_ref[0], precision=lax.Precision.HIGHEST,
                       preferred_element_type=F32) + b_ref[0]


def _modulation(cc, w_ada, b_ada):
    nj = N_MOD
    return pl.pallas_call(
        _mod_body,
        grid=(DEPTH, nj),
        in_specs=[
            pl.BlockSpec((MOD_ROWS, D_MODEL), lambda l, j: (0, 0)),
            pl.BlockSpec((1, D_MODEL, D_MODEL), lambda l, j: (l, 0, j)),
            pl.BlockSpec((1, 1, D_MODEL), lambda l, j: (l, 0, j)),
        ],
        out_specs=pl.BlockSpec((1, MOD_ROWS, D_MODEL), lambda l, j: (l, 0, j)),
        out_shape=jax.ShapeDtypeStruct((DEPTH, MOD_ROWS, N_MOD * D_MODEL), F32),
        compiler_params=_cparams(("arbitrary", "arbitrary")),
        name="modulation",
    )(cc, w_ada, b_ada.reshape(DEPTH, 1, N_MOD * D_MODEL))


def _lam_body(d0_ref, d1_ref, d2_ref, d3_ref, li_ref, o_ref):
    a = jnp.sum(d0_ref[...] * d1_ref[...], axis=-1, keepdims=True)
    b = jnp.sum(d2_ref[...] * d3_ref[...], axis=-1, keepdims=True)
    o_ref[...] = jnp.exp(a) - jnp.exp(b) + li_ref[...]


def _diff_lambda(diff_lambda, lam_init):
    dl = diff_lambda.astype(F32)
    return pl.pallas_call(
        _lam_body,
        out_shape=jax.ShapeDtypeStruct((DEPTH, LANES), F32),
        name="diff_lambda",
    )(dl[:, 0], dl[:, 1], dl[:, 2], dl[:, 3], lam_init)


def _rope(xb, c, sp, sm, shift):
    return xb * c + pltpu.roll(xb, shift, 1) * sp + pltpu.roll(xb, LANES - shift, 1) * sm


def _qkv_body(has_peer, *refs):
    if has_peer:
        x_ref, peer_ref, g2_ref = refs[:3]
        refs = refs[3:]
    else:
        x_ref = refs[0]
        refs = refs[1:]
    (ng_ref, sc_ref, sh_ref, win_ref, gq_ref, gkv_ref, wq_ref, wk_ref, wv_ref, tab_ref) = refs[:10]
    outs = refs[10:]
    if has_peer:
        xo_ref, outs = outs[0], outs[1:]
    qm_ref, km_ref, vm_ref, qd_ref, kd_ref, vd_ref = outs

    x = x_ref[...]
    if has_peer:
        x = x + g2_ref[0] * _tile_rows_load(peer_ref)
        xo_ref[...] = x
    h = _rms(x) * ng_ref[...] * (1.0 + sc_ref[0]) + sh_ref[0]
    proj = jnp.dot(h.astype(BF16), win_ref[...], preferred_element_type=F32)
    cq = _rms(proj[:, 0:384]) * gq_ref[...]
    ckv = _rms(proj[:, 384:640]) * gkv_ref[...]
    kr = proj[:, 640:768]
    q = jnp.dot(cq.astype(BF16), wq_ref[...], preferred_element_type=F32)
    ckv16 = ckv.astype(BF16)
    k = jnp.dot(ckv16, wk_ref[...], preferred_element_type=F32)
    v = jnp.dot(ckv16, wv_ref[...], preferred_element_type=F32)
    tab = tab_ref[...]
    cm, spm, smm = tab[:, 0:128], tab[:, 128:256], tab[:, 256:384]
    cd, spd, smd = tab[:, 384:512], tab[:, 512:640], tab[:, 640:768]
    kr_rot = _rope(kr, cm, spm, smm, MLA_ROPE_DIM // 2)
    scale_m = LOG2_E * (MLA_HEAD_DIM + MLA_ROPE_DIM) ** -0.5
    for hh in range(MLA_HEADS):
        sl = slice(hh * LANES, (hh + 1) * LANES)
        qm_ref[:, sl] = (_rope(q[:, sl], cm, spm, smm, MLA_ROPE_DIM // 2) * scale_m).astype(BF16)
        km_ref[:, sl] = (k[:, sl] + kr_rot).astype(BF16)
    vm_ref[...] = v.T.astype(BF16)
    scale_d = LOG2_E * DIFF_HEAD_DIM ** -0.5
    for hh in range(DIFF_HEADS):
        sl = slice(hh * LANES, (hh + 1) * LANES)
        qd = proj[:, 768 + hh * LANES:768 + (hh + 1) * LANES]
        kd = proj[:, 1280 + hh * LANES:1280 + (hh + 1) * LANES]
        qd_ref[:, sl] = (_rope(qd, cd, spd, smd, DIFF_HEAD_DIM // 2) * scale_d).astype(BF16)
        kd_ref[:, sl] = _rope(kd, cd, spd, smd, DIFF_HEAD_DIM // 2).astype(BF16)
    vd_ref[...] = proj[:, 1792:2304].T.astype(BF16)


def _qkv(x, peer, g2, ng, sc, sh, win, gq, gkv, wq, wk, wv, tab):
    has_peer = peer is not None
    row = lambda i: (i, 0)
    modrow = lambda i: (_bid(i), 0, 0)
    const = lambda i: (0, 0)
    in_specs = [pl.BlockSpec((TM, D_MODEL), row)]
    args = [x]
    if has_peer:
        in_specs += [pl.BlockSpec((TM * SUBLANES, LANES), row), pl.BlockSpec((1, 1, D_MODEL), modrow)]
        args += [peer, g2]
    in_specs += [
        pl.BlockSpec((1, D_MODEL), const),
        pl.BlockSpec((1, 1, D_MODEL), modrow),
        pl.BlockSpec((1, 1, D_MODEL), modrow),
        pl.BlockSpec((D_MODEL, PROJ_PAD), const),
        pl.BlockSpec((1, MLA_Q_RANK), const),
        pl.BlockSpec((1, MLA_KV_RANK), const),
        pl.BlockSpec((MLA_Q_RANK, MLA_HEADS * LANES), const),
        pl.BlockSpec((MLA_KV_RANK, MLA_HEADS * LANES), const),
        pl.BlockSpec((MLA_KV_RANK, 512), const),
        pl.BlockSpec((TM, 768), lambda i: (_posblk(i), 0)),
    ]
    args += [ng, sc, sh, win, gq, gkv, wq, wk, wv, tab]
    out_specs, out_shape = [], []
    if has_peer:
        out_specs.append(pl.BlockSpec((TM, D_MODEL), row))
        out_shape.append(jax.ShapeDtypeStruct((TT, D_MODEL), F32))
    for width, transposed in ((1024, False), (1024, False), (512, True),
                              (512, False), (512, False), (512, True)):
        if transposed:
            out_specs.append(pl.BlockSpec((width, TM), lambda i: (0, i)))
            out_shape.append(jax.ShapeDtypeStruct((width, TT), BF16))
        else:
            out_specs.append(pl.BlockSpec((TM, width), row))
            out_shape.append(jax.ShapeDtypeStruct((TT, width), BF16))
    res = pl.pallas_call(
        functools.partial(_qkv_body, has_peer),
        grid=(NB_ALL,),
        in_specs=in_specs,
        out_specs=out_specs,
        out_shape=out_shape,
        compiler_params=_cparams(("parallel",)),
        name="qkv",
    )(*args)
    if has_peer:
        return res[0], res[1:]
    return x, res


def _flash_pair(tq, qs, ksls, vsls, kc_ref, kl_ref, vtc_ref, vtl_ref, n_pairs, sa_scr, sb_scr, p_scr):
    n_lat = SEQ // TK

    def col_reduce(parts, op, final):
        while len(parts) > 1:
            parts = [op(parts[i], parts[i + 1]) for i in range(0, len(parts), 2)]
        return final(parts[0], axis=0, keepdims=True)

    def row_groups(x):
        return [x[i:i + SUBLANES] for i in range(0, x.shape[0], SUBLANES)]

    def scores(k_of):
        return [lax.dot_general(k_of(ksl), q, (((1,), (1,)), ((), ())), preferred_element_type=F32)
                for q, ksl in zip(qs, ksls)]

    def softmax_pv(read_s, vt_of, carries, n_keys, par):
        mid = []
        for h, (m, l, acc) in enumerate(carries):
            m_new = jnp.maximum(m, col_reduce(row_groups(read_s(h, 0, n_keys)), jnp.maximum, jnp.max))
            alpha = jnp.exp2(m - m_new)
            for r0 in range(0, n_keys, P_ROWS):
                p = jnp.exp2(read_s(h, r0, P_ROWS) - m_new)
                l = alpha * l + col_reduce(row_groups(p), jnp.add, jnp.sum) if r0 == 0 else (
                    l + col_reduce(row_groups(p), jnp.add, jnp.sum))
                p_scr[par, h, r0:r0 + P_ROWS, :] = p.astype(BF16)
            mid.append((m_new, l, alpha * acc))
        return tuple((m_new, l, acc + jnp.dot(vt_of(vsl), p_scr[par, h, 0:n_keys, :], preferred_element_type=F32))
                     for h, ((m_new, l, acc), vsl) in enumerate(zip(mid, vsls)))

    def k_lat(c):
        return lambda ksl: kl_ref[c * TK:(c + 1) * TK, ksl]

    def vt_lat(c):
        return lambda vsl: vtl_ref[vsl, c * TK:(c + 1) * TK]

    def stage(scr, sc):
        for h, x in enumerate(sc):
            scr[h] = x

    init = tuple((jnp.full((1, tq), -jnp.inf, F32), jnp.zeros((1, tq), F32),
                  jnp.zeros((vsl.stop - vsl.start, tq), F32)) for vsl in vsls)
    ctx_scores = scores(lambda ksl: kc_ref[:, ksl])
    if n_pairs:
        stage(sa_scr, scores(k_lat(0)))
    carries = softmax_pv(lambda h, r0, n: ctx_scores[h][r0:r0 + n], lambda vsl: vtc_ref[vsl, :], init, CTX_LEN, 1)

    def pair(j, carries):
        c = 2 * j
        stage(sb_scr, scores(k_lat(c + 1)))
        carries = softmax_pv(lambda h, r0, n: sa_scr[h, r0:r0 + n, :], vt_lat(c), carries, TK, 0)
        if c + 2 < n_lat:
            stage(sa_scr, scores(k_lat(c + 2)))
        return softmax_pv(lambda h, r0, n: sb_scr[h, r0:r0 + n, :], vt_lat(c + 1), carries, TK, 1)

    for j in range(n_pairs):
        carries = pair(j, carries)
    return [acc / l for (m, l, acc) in carries]


def _attn_mla_body(tq, n_pairs, *refs):
    if n_pairs:
        q_ref, kc_ref, kl_ref, vtc_ref, vtl_ref, o_ref, sa_scr, sb_scr, p_scr = refs
    else:
        q_ref, kc_ref, vtc_ref, o_ref, p_scr = refs
        kl_ref = vtl_ref = sa_scr = sb_scr = None
    qs = [q_ref[:, 0:LANES], q_ref[:, LANES:2 * LANES]]
    ksls = [slice(0, LANES), slice(LANES, 2 * LANES)]
    vsls = [slice(0, MLA_HEAD_DIM), slice(MLA_HEAD_DIM, 2 * MLA_HEAD_DIM)]
    ot0, ot1 = _flash_pair(tq, qs, ksls, vsls, kc_ref, kl_ref, vtc_ref, vtl_ref, n_pairs, sa_scr, sb_scr, p_scr)
    o_ref[...] = jnp.concatenate([ot0, ot1], axis=0).T.astype(o_ref.dtype)


def _attn_diff_body(tq, n_pairs, lam_scale, *refs):
    if n_pairs:
        q_ref, kc_ref, kl_ref, vtc_ref, vtl_ref, lam_ref, g_ref, o_ref, sa_scr, sb_scr, p_scr = refs
    else:
        q_ref, kc_ref, vtc_ref, lam_ref, g_ref, o_ref, p_scr = refs
        kl_ref = vtl_ref = sa_scr = sb_scr = None
    lane = lax.broadcasted_iota(jnp.int32, (tq, LANES), 1)
    q = q_ref[...]
    zero = jnp.zeros_like(q)
    qs = [jnp.where(lane < DIFF_HEAD_DIM, q, zero), jnp.where(lane < DIFF_HEAD_DIM, zero, q)]
    ksls = [slice(0, LANES), slice(0, LANES)]
    vsls = [slice(0, LANES), slice(0, LANES)]
    ot0, ot1 = _flash_pair(tq, qs, ksls, vsls, kc_ref, kl_ref, vtc_ref, vtl_ref, n_pairs, sa_scr, sb_scr, p_scr)
    o = ot0.T - lam_ref[...] * ot1.T
    o_ref[...] = (_rms(o) * g_ref[...] * lam_scale).astype(o_ref.dtype)


def _attention(kind, q, k, vt, ctx_queries, lam=None, g=None, lam_scale=None):
    qw = 2 * LANES if kind == "mla" else LANES
    ctx0 = T_LAT // CTX_LEN
    if ctx_queries:
        tq, n_pairs, steps, rows = CTX_LEN, 0, 1, T_CTX
        q_map = lambda b, p, i: (ctx0 + b, p)
        o_map = lambda b, p, i: (b, p)
    else:
        tq, n_pairs, steps, rows = TQ, SEQ // (2 * TK), SEQ // TQ, T_LAT
        q_map = lambda b, p, i: (b * steps + i, p)
        o_map = q_map
    kc_spec = pl.BlockSpec((CTX_LEN, qw), lambda b, p, i: (ctx0 + b, p))
    vtc_spec = pl.BlockSpec((LANES, CTX_LEN), lambda b, p, i: (p, ctx0 + b))
    in_specs = [pl.BlockSpec((tq, qw), q_map), kc_spec]
    args = [q, k]
    if n_pairs:
        in_specs += [pl.BlockSpec((SEQ, qw), lambda b, p, i: (b, p)), vtc_spec,
                     pl.BlockSpec((LANES, SEQ), lambda b, p, i: (p, b))]
        args += [k, vt, vt]
    else:
        in_specs += [vtc_spec]
        args += [vt]
    if kind == "mla":
        body = functools.partial(_attn_mla_body, tq, n_pairs)
    else:
        body = functools.partial(_attn_diff_body, tq, n_pairs, lam_scale)
        in_specs += [pl.BlockSpec((1, LANES), lambda b, p, i: (0, 0)),
                     pl.BlockSpec((1, LANES), lambda b, p, i: (0, 0))]
        args += [lam, g]
    scratch = [pltpu.VMEM((2, TK, tq), F32), pltpu.VMEM((2, TK, tq), F32)] if n_pairs else []
    scratch.append(pltpu.VMEM((2, 2, TK, tq), BF16))
    return pl.pallas_call(
        body,
        grid=(BATCH, 4, steps),
        in_specs=in_specs,
        out_specs=pl.BlockSpec((tq, LANES), o_map),
        out_shape=jax.ShapeDtypeStruct((rows, 512), BF16),
        scratch_shapes=scratch,
        compiler_params=_cparams(("parallel", "parallel", "arbitrary")),
        name="attn_" + kind + ("_ctx" if ctx_queries else ""),
    )(*args)


HEADS_PER_TRIP = 4


def _top16(s, payload=None):
    n_rows = s.shape[0]
    rowf = lax.broadcasted_iota(jnp.int32, s.shape, 0).astype(F32)
    slot = lax.broadcasted_iota(jnp.int32, (PEER_TOPK, s.shape[1]), 0)
    vals = jnp.zeros((PEER_TOPK, s.shape[1]), F32)
    picks = jnp.zeros((PEER_TOPK, s.shape[1]), F32)
    for r in range(PEER_TOPK):
        m = jnp.max(s, axis=0, keepdims=True)
        am = jnp.min(jnp.where(s == m, rowf, float(n_rows)), axis=0, keepdims=True)
        hit = rowf == am
        pick = am if payload is None else jnp.max(jnp.where(hit, payload, -1.0), axis=0, keepdims=True)
        vals = jnp.where(slot == r, m, vals)
        picks = jnp.where(slot == r, pick, picks)
        s = jnp.where(hit, -jnp.inf, s)
    return vals, picks


def _staircase(a16, b16, combine, pad):
    tm = a16.shape[1]
    sub = lax.broadcasted_iota(jnp.int32, (SUBLANES, tm), 0)
    a_lo, a_hi = a16[0:SUBLANES], a16[SUBLANES:]
    b_lo, b_hi = b16[0:SUBLANES], b16[SUBLANES:]
    row = lambda x, r: jnp.broadcast_to(x[r:r + 1], (SUBLANES, tm))
    take = lambda x, idx: jnp.take_along_axis(x, idx, axis=0)
    a3 = jnp.where(sub < 5, 2, 3)
    b3 = jnp.where(sub < 5, sub, sub - 5)
    a4 = jnp.where(sub < 1, 3, jnp.where(sub < 4, 4, jnp.where(sub < 6, 5, 6)))
    b4 = jnp.where(sub < 1, 3, jnp.where(sub < 4, sub - 1, jnp.where(sub < 6, sub - 4, sub - 6)))
    pieces = [
        combine(row(a_lo, 0), b_lo),
        combine(row(a_lo, 0), b_hi),
        combine(row(a_lo, 1), b_lo),
        combine(take(a_lo, a3), take(b_lo, b3)),
        combine(take(a_lo, a4), take(b_lo, b4)),
        jnp.where(sub < 2, combine(row(a_lo, 7), b_lo), pad),
        combine(a_hi, row(b_lo, 0)),
    ]
    return jnp.concatenate(pieces, axis=0)


def _mid_body(x_ref, om_ref, od_ref, wo_m_ref, wo_d_ref, g1_ref, ng_ref, sc_ref, sh_ref,
              wq_ref, sk_ref, xo_ref, h_ref, eidx_ref, gate_ref, q_scr):
    y = (jnp.dot(om_ref[...], wo_m_ref[...], preferred_element_type=F32)
         + jnp.dot(od_ref[...], wo_d_ref[...], preferred_element_type=F32))
    x = x_ref[...] + g1_ref[0] * y
    xo_ref[...] = x
    h = _rms(x) * ng_ref[...] * (1.0 + sc_ref[0]) + sh_ref[0]
    _tile_rows_store(h_ref, h)
    q_scr[...] = jnp.dot(h.astype(BF16), wq_ref[...], preferred_element_type=F32).astype(BF16)

    def head(hh):
        sv, si = [], []
        for c in range(2):
            c0 = pl.multiple_of(hh * 2 * N_KEYS + c * N_KEYS, N_KEYS)
            qh = q_scr[:, pl.ds(c0, N_KEYS)]
            s = lax.dot_general(sk_ref[c], qh, (((1,), (1,)), ((), ())), preferred_element_type=F32)
            vals, keys = _top16(s)
            sv.append(vals)
            si.append(keys)
        cand_s = _staircase(sv[0], sv[1], lambda a, b: a + b, -jnp.inf)
        cand_e = _staircase(si[0], si[1], lambda a, b: a * float(N_KEYS) + b, -1.0)
        top_s, top_e = _top16(cand_s, cand_e)
        ex = jnp.exp(top_s - jnp.max(top_s, axis=0, keepdims=True))
        gate = ex / jnp.sum(ex, axis=0, keepdims=True)
        r0 = pl.multiple_of(hh * PEER_TOPK, PEER_TOPK)
        eidx_ref[pl.ds(r0, PEER_TOPK), :] = top_e.astype(jnp.int32) * ROWS_PER_EXPERT
        gate_ref[pl.ds(r0, PEER_TOPK), :] = gate

    def head_group(j, _):
        for i in range(HEADS_PER_TRIP):
            head(HEADS_PER_TRIP * j + i)
        return 0

    lax.fori_loop(0, PEER_HEADS // HEADS_PER_TRIP, head_group, 0)


def _mid(n_blocks, x, om, od, wo_m, wo_d, g1, ng, sc, sh, wq, sk):
    n = n_blocks * TM
    row = lambda i: (i, 0)
    modrow = lambda i: (_bid(i), 0, 0)
    const = lambda i: (0, 0)
    return pl.pallas_call(
        _mid_body,
        grid=(n_blocks,),
        in_specs=[
            pl.BlockSpec((TM, D_MODEL), row),
            pl.BlockSpec((TM, 512), row),
            pl.BlockSpec((TM, 512), row),
            pl.BlockSpec((512, D_MODEL), const),
            pl.BlockSpec((512, D_MODEL), const),
            pl.BlockSpec((1, 1, D_MODEL), modrow),
            pl.BlockSpec((1, D_MODEL), const),
            pl.BlockSpec((1, 1, D_MODEL), modrow),
            pl.BlockSpec((1, 1, D_MODEL), modrow),
            pl.BlockSpec((D_MODEL, PEER_HEADS * 2 * N_KEYS), const),
            pl.BlockSpec((2, N_KEYS, N_KEYS), lambda i: (0, 0, 0)),
        ],
        out_specs=[
            pl.BlockSpec((TM, D_MODEL), row),
            pl.BlockSpec((TM * SUBLANES, LANES), row),
            pl.BlockSpec((PEER_HEADS * PEER_TOPK, TM), lambda i: (0, i)),
            pl.BlockSpec((PEER_HEADS * PEER_TOPK, TM), lambda i: (0, i)),
        ],
        out_shape=[
            jax.ShapeDtypeStruct((n, D_MODEL), F32),
            jax.ShapeDtypeStruct((n * SUBLANES, LANES), F32),
            jax.ShapeDtypeStruct((PEER_HEADS * PEER_TOPK, n), jnp.int32),
            jax.ShapeDtypeStruct((PEER_HEADS * PEER_TOPK, n), F32),
        ],
        scratch_shapes=[pltpu.VMEM((TM, PEER_HEADS * 2 * N_KEYS), BF16)],
        compiler_params=_cparams(("parallel",)),
        name="mid",
    )(x, om, od, wo_m, wo_d, g1, ng, sc, sh, wq, sk)


N_SEL = PEER_HEADS * PEER_TOPK
HALF = D_MODEL // 2
ROWS_PER_EXPERT = HALF // LANES


PACK_BLOCK = 512


def _pack_body(t_ref, o_ref):
    u = pltpu.bitcast(t_ref[0], jnp.uint32)
    r = u + jnp.uint32(0x7FFF) + ((u >> 16) & jnp.uint32(1))
    w = (r[:, :HALF] >> 16) | (r[:, HALF:] & jnp.uint32(0xFFFF0000))
    for s in range(ROWS_PER_EXPERT):
        o_ref[pl.ds(s, PACK_BLOCK, stride=ROWS_PER_EXPERT), :] = w[:, s * LANES:(s + 1) * LANES]


def _pack_table(tabs, layer):
    n = tabs.shape[1]
    return pl.pallas_call(
        _pack_body,
        grid=(n // PACK_BLOCK,),
        in_specs=[pl.BlockSpec((1, PACK_BLOCK, D_MODEL), lambda i: (layer, i, 0))],
        out_specs=pl.BlockSpec((PACK_BLOCK * ROWS_PER_EXPERT, LANES), lambda i: (i, 0)),
        out_shape=jax.ShapeDtypeStruct((n * ROWS_PER_EXPERT, LANES), jnp.uint32),
        compiler_params=_cparams(("parallel",)),
        name="pack_table",
    )(tabs)


def _load_pair(tab_ref, ra, rb):
    w = jnp.concatenate([tab_ref[pl.ds(pl.multiple_of(ra, ROWS_PER_EXPERT), ROWS_PER_EXPERT), :],
                         tab_ref[pl.ds(pl.multiple_of(rb, ROWS_PER_EXPERT), ROWS_PER_EXPERT), :]], axis=0)
    lo = pltpu.bitcast(w << 16, F32)
    hi = pltpu.bitcast(w & jnp.uint32(0xFFFF0000), F32)
    return lo, hi


def _with_index_buffers(idx_hbm, idx_a, idx_b, sem, first, second):
    step = pl.program_id(0)
    words = PEER_SB * N_SEL

    def idx_copy(block, dst, k):
        return pltpu.make_async_copy(idx_hbm.at[pl.ds(block * words, words)], dst, sem.at[k])

    @pl.when(step == 0)
    def _():
        idx_copy(0, idx_a, 0).start()

    idx_copy(2 * step + 1, idx_b, 1).start()
    idx_copy(2 * step, idx_a, 0).wait()
    first(idx_a)

    @pl.when(step + 1 < pl.num_programs(0))
    def _():
        idx_copy(2 * step + 2, idx_a, 0).start()

    idx_copy(2 * step + 1, idx_b, 1).wait()
    second(idx_b)


def _peer_u_body(idx_hbm, x_ref, gate_ref, tab_ref, act_ref, idx_a, idx_b, r_scr, score_scr, sem):
    sub = lax.broadcasted_iota(jnp.int32, (SUBLANES, LANES), 0)
    lane = lax.broadcasted_iota(jnp.int32, (SUBLANES, LANES), 1)
    top = sub < 4
    n_pair = N_SEL // 2
    quarter = pl.program_id(0) % STEPS_PER_ACT_BLOCK

    def gather_token(idx_s, tok, tok_in_buf):
        xt = x_ref[tok]
        xsw = pltpu.roll(xt, 4, 0)
        xlo = jnp.where(top, xt, xsw)
        xhi = jnp.where(top, xsw, xt)
        for p in range(n_pair):
            ra = idx_s[tok_in_buf * N_SEL + 2 * p]
            rb = idx_s[tok_in_buf * N_SEL + 2 * p + 1]
            lo, hi = _load_pair(tab_ref, ra, rb)
            r_scr[tok % R_SLOTS * n_pair + p] = lo * xlo + hi * xhi

    def reduce_token(tok):
        token_lane = quarter * (2 * PEER_SB) + tok
        for p in range(n_pair):
            col = jnp.sum(r_scr[tok % R_SLOTS * n_pair + p], axis=1, keepdims=True)
            score_scr[p] = jnp.where(lane == token_lane, col, score_scr[p])

    @pl.when(quarter == 0)
    def _():
        score_scr[...] = jnp.zeros(score_scr.shape, F32)

    def first(idx_s):
        for i in range(PEER_SB):
            gather_token(idx_s, i, i)
            if i:
                reduce_token(i - 1)

    def second(idx_s):
        for i in range(PEER_SB):
            gather_token(idx_s, PEER_SB + i, i)
            reduce_token(PEER_SB + i - 1)
        reduce_token(2 * PEER_SB - 1)

    _with_index_buffers(idx_hbm, idx_a, idx_b, sem, first, second)

    @pl.when(quarter == STEPS_PER_ACT_BLOCK - 1)
    def _():
        for g in range(N_SEL // SUBLANES):
            scores = jnp.zeros((SUBLANES, LANES), F32)
            for pp in range(4):
                s = score_scr[4 * g + pp]
                t = s + pltpu.roll(s, 6, 0)
                t = t + pltpu.roll(t, 7, 0)
                scores = jnp.where(sub == 2 * pp, pltpu.roll(t, 2 * pp, 0), scores)
                scores = jnp.where(sub == 2 * pp + 1, pltpu.roll(t, (2 * pp + 5) % SUBLANES, 0), scores)
            act_ref[g] = 0.5 * scores * (1.0 + lax.erf(scores * (2.0 ** -0.5))) * gate_ref[g]


def _peer_v_body(idx_hbm, act_ref, tab_ref, o_ref, idx_a, idx_b, sem):
    sub = lax.broadcasted_iota(jnp.int32, (SUBLANES, LANES), 0)
    lane = lax.broadcasted_iota(jnp.int32, (SUBLANES, LANES), 1)
    top = sub < 4
    n_grp = N_SEL // SUBLANES
    pair_rows = [jnp.where(top, 2 * p, 2 * p + 1) for p in range(4)]

    quarter = pl.program_id(0) % STEPS_PER_ACT_BLOCK

    def process(idx_s, half):
        for tile in range(PEER_SB // SUBLANES):
            for tt in range(SUBLANES):
                tl = tile * SUBLANES + tt
                token_lane = quarter * (2 * PEER_SB) + half * PEER_SB + tl
                acc = [jnp.zeros((SUBLANES, LANES), F32) for _ in range(4)]
                for g in range(n_grp):
                    col = jnp.sum(jnp.where(lane == token_lane, act_ref[g], 0.0), axis=1, keepdims=True)
                    actg = jnp.broadcast_to(col, (SUBLANES, LANES))
                    for p in range(4):
                        ra = idx_s[tl * N_SEL + g * SUBLANES + 2 * p]
                        rb = idx_s[tl * N_SEL + g * SUBLANES + 2 * p + 1]
                        lo, hi = _load_pair(tab_ref, ra, rb)
                        av = jnp.take_along_axis(actg, pair_rows[p], axis=0)
                        k = 2 * (p % 2)
                        acc[k] = acc[k] + av * lo
                        acc[k + 1] = acc[k + 1] + av * hi
                lo = acc[0] + acc[2]
                hi = acc[1] + acc[3]
                lo = lo + pltpu.roll(lo, 4, 0)
                hi = hi + pltpu.roll(hi, 4, 0)
                o_ref[half * PEER_SB + tl] = jnp.where(top, lo, hi)

    _with_index_buffers(idx_hbm, idx_a, idx_b, sem,
                        lambda idx_s: process(idx_s, 0), lambda idx_s: process(idx_s, 1))


def _table_spec():
    return pl.BlockSpec((N_EXPERTS * ROWS_PER_EXPERT, LANES), lambda i: (0, 0),
                        pipeline_mode=pl.Buffered(1))


R_SLOTS = 3
STEPS_PER_ACT_BLOCK = LANES // (2 * PEER_SB)


def _act_block_spec():
    return pl.BlockSpec((N_SEL // SUBLANES, SUBLANES, LANES), lambda i: (0, 0, i // STEPS_PER_ACT_BLOCK))


def _peer_idx_scratch():
    return [pltpu.SMEM((PEER_SB * N_SEL,), jnp.int32), pltpu.SMEM((PEER_SB * N_SEL,), jnp.int32)]


def _peer_u(n, idx, x3, gate3, tab):
    tb = 2 * PEER_SB
    return pl.pallas_call(
        _peer_u_body,
        grid=(n // tb,),
        in_specs=[
            pl.BlockSpec(memory_space=pl.ANY),
            pl.BlockSpec((tb, SUBLANES, LANES), lambda i: (i, 0, 0)),
            _act_block_spec(),
            _table_spec(),
        ],
        out_specs=_act_block_spec(),
        out_shape=jax.ShapeDtypeStruct((N_SEL // SUBLANES, SUBLANES, n), F32),
        scratch_shapes=_peer_idx_scratch() + [
            pltpu.VMEM((R_SLOTS * (N_SEL // 2), SUBLANES, LANES), F32),
            pltpu.VMEM((N_SEL // 2, SUBLANES, LANES), F32),
            pltpu.SemaphoreType.DMA((2,))],
        compiler_params=_cparams(("arbitrary",)),
        name="peer_u",
    )(idx.reshape(-1), x3, gate3, tab)


def _peer_v(n, idx, act3, tab):
    tb = 2 * PEER_SB
    return pl.pallas_call(
        _peer_v_body,
        grid=(n // tb,),
        in_specs=[
            pl.BlockSpec(memory_space=pl.ANY),
            _act_block_spec(),
            _table_spec(),
        ],
        out_specs=pl.BlockSpec((tb, SUBLANES, LANES), lambda i: (i, 0, 0)),
        out_shape=jax.ShapeDtypeStruct((n, SUBLANES, LANES), F32),
        scratch_shapes=_peer_idx_scratch() + [pltpu.SemaphoreType.DMA((2,))],
        compiler_params=_cparams(("arbitrary",)),
        name="peer_v",
    )(idx.reshape(-1), act3, tab)


def _final_body(x_ref, peer_ref, g2_ref, g_ref, o_ref):
    x = x_ref[...] + g2_ref[0] * _tile_rows_load(peer_ref)
    o_ref[...] = _rms(x) * g_ref[...]


def _final(x, peer, g2, g):
    row = lambda i: (i, 0)
    return pl.pallas_call(
        _final_body,
        grid=(NB_LAT,),
        in_specs=[
            pl.BlockSpec((TM, D_MODEL), row),
            pl.BlockSpec((TM * SUBLANES, LANES), row),
            pl.BlockSpec((1, 1, D_MODEL), lambda i: (_bid(i), 0, 0)),
            pl.BlockSpec((1, D_MODEL), lambda i: (0, 0)),
        ],
        out_specs=pl.BlockSpec((TM, D_MODEL), row),
        out_shape=jax.ShapeDtypeStruct((T_LAT, D_MODEL), F32),
        compiler_params=_cparams(("parallel",)),
        name="final_norm",
    )(x, peer, g2, g)


def _deinterleave(width):
    return np.concatenate([np.arange(0, width, 2), np.arange(1, width, 2)])


def _rope_tables():
    pos = np.arange(SEQ)
    rowp = jnp.asarray(pos // GRID_W, F32)
    colp = jnp.asarray(pos % GRID_W, F32)

    def angles(dim):
        quarter = dim // 4
        inv = ROPE_BASE ** (-jnp.arange(quarter, dtype=F32) / quarter)
        return jnp.concatenate([rowp[:, None] * inv, colp[:, None] * inv], axis=-1)

    am = angles(MLA_ROPE_DIM)
    ad = angles(DIFF_HEAD_DIM)
    cm, sm_ = jnp.cos(am), jnp.sin(am)
    cd, sd = jnp.cos(ad), jnp.sin(ad)
    one = lambda w: jnp.ones((SEQ, w), F32)
    zero = lambda w: jnp.zeros((SEQ, w), F32)
    t_cm = jnp.concatenate([one(64), cm, cm, one(32)], axis=1)
    t_spm = jnp.concatenate([zero(80), sm_, zero(32)], axis=1)
    t_smm = jnp.concatenate([zero(64), -sm_, zero(48)], axis=1)
    t_cd = jnp.concatenate([cd, cd, cd, cd], axis=1)
    t_spd = jnp.concatenate([zero(32), sd, zero(32), sd], axis=1)
    t_smd = jnp.concatenate([-sd, zero(32), -sd, zero(32)], axis=1)
    lat = jnp.concatenate([t_cm, t_spm, t_smm, t_cd, t_spd, t_smd], axis=1)
    ident = jnp.concatenate([jnp.ones((CTX_LEN, 128), F32), jnp.zeros((CTX_LEN, 256), F32),
                             jnp.ones((CTX_LEN, 128), F32), jnp.zeros((CTX_LEN, 256), F32)], axis=1)
    return jnp.concatenate([lat, ident], axis=0)


def _prep_layer_weights(w_in, wq_up, wq_rope, wk_up, wv_up, w_out, peer_wq, peer_subkeys):
    p32 = _deinterleave(MLA_ROPE_DIM)
    p64 = _deinterleave(DIFF_HEAD_DIM)
    z = lambda w: jnp.zeros((D_MODEL, w), F32)
    kr = w_in[:, 640:672][:, p32]
    qd = w_in[:, 672:1184].reshape(D_MODEL, 8, 64)[:, :, p64].reshape(D_MODEL, 512)
    kd = w_in[:, 1184:1696].reshape(D_MODEL, 8, 64)[:, :, p64].reshape(D_MODEL, 512)
    win = jnp.concatenate([w_in[:, 0:640], z(64), kr, z(32), qd, kd, w_in[:, 1696:2208]], axis=1)
    qn = wq_up.reshape(MLA_Q_RANK, MLA_HEADS, MLA_HEAD_DIM)
    qr = wq_rope.reshape(MLA_Q_RANK, MLA_HEADS, MLA_ROPE_DIM)[:, :, p32]
    wq = jnp.concatenate([qn, qr, jnp.zeros((MLA_Q_RANK, MLA_HEADS, 32), F32)], axis=2)
    kn = wk_up.reshape(MLA_KV_RANK, MLA_HEADS, MLA_HEAD_DIM)
    wk = jnp.concatenate([kn, jnp.zeros((MLA_KV_RANK, MLA_HEADS, 64), F32)], axis=2)
    return dict(
        win=win.astype(BF16),
        wq=wq.reshape(MLA_Q_RANK, MLA_HEADS * LANES).astype(BF16),
        wk=wk.reshape(MLA_KV_RANK, MLA_HEADS * LANES).astype(BF16),
        wv=wv_up.astype(BF16),
        wo_m=w_out[:512].astype(BF16),
        wo_d=w_out[512:].astype(BF16),
        pwq=peer_wq.astype(BF16),
        sk=peer_subkeys.astype(BF16),
    )


def kernel(x, c, ctx, c_ctx, norm_attn_g, norm_ffn_g, w_ada, b_ada, w_in, mla_q_norm_g, mla_wq_up, mla_wq_rope, mla_kv_norm_g, mla_wk_up, mla_wv_up, diff_lambda, diff_subnorm_g, w_out, peer_wq, peer_subkeys, peer_u, peer_v, final_norm_g):
    xs = jnp.concatenate([x.reshape(T_LAT, D_MODEL), ctx.reshape(T_CTX, D_MODEL)], axis=0)
    cc = jnp.concatenate([c, c_ctx[None, :], jnp.zeros((MOD_ROWS - BATCH - 1, D_MODEL), F32)], axis=0)
    mod = _modulation(cc, w_ada, b_ada)
    lam_inits = [0.8 - 0.6 * math.exp(-0.3 * l) for l in range(DEPTH)]
    lam_all = _diff_lambda(diff_lambda, jnp.broadcast_to(jnp.asarray(lam_inits, F32)[:, None], (DEPTH, LANES)))
    tab = _rope_tables()

    peer_out = None
    g2_prev = None
    for l in range(DEPTH):
        last = l == DEPTH - 1
        w = _prep_layer_weights(w_in[l], mla_wq_up[l], mla_wq_rope[l], mla_wk_up[l], mla_wv_up[l],
                                w_out[l], peer_wq[l], peer_subkeys[l])
        m = mod[l].reshape(MOD_ROWS, N_MOD, 1, D_MODEL)
        sh1, sc1, g1, sh2, sc2, g2 = (m[:, j] for j in range(N_MOD))
        xs, (qm, km, vm, qd, kd, vd) = _qkv(
            xs, peer_out, g2_prev, norm_attn_g[l][None, :], sc1, sh1, w["win"],
            mla_q_norm_g[l][None, :], mla_kv_norm_g[l][None, :], w["wq"], w["wk"], w["wv"], tab)
        diff_args = dict(lam=lam_all[l][None, :], g=diff_subnorm_g[l][None, :],
                         lam_scale=1.0 - lam_inits[l])
        om = _attention("mla", qm, km, vm, False)
        od = _attention("diff", qd, kd, vd, False, **diff_args)
        if not last:
            om = jnp.concatenate([om, _attention("mla", qm, km, vm, True)], axis=0)
            od = jnp.concatenate([od, _attention("diff", qd, kd, vd, True, **diff_args)], axis=0)
        n_blocks = NB_LAT if last else NB_ALL
        n = n_blocks * TM
        xs, h2, eidx_t, gate_t = _mid(n_blocks, xs, om, od, w["wo_m"], w["wo_d"], g1,
                                      norm_ffn_g[l][None, :], sc2, sh2, w["pwq"], w["sk"])
        idx = eidx_t.T
        gate3 = gate_t.reshape(N_SEL // SUBLANES, SUBLANES, n)
        act3 = _peer_u(n, idx, h2.reshape(n, SUBLANES, LANES), gate3, _pack_table(peer_u, l))
        peer_out = _peer_v(n, idx, act3, _pack_table(peer_v, l)).reshape(n * SUBLANES, LANES)
        g2_prev = g2
    out = _final(xs, peer_out, g2_prev, final_norm_g[None, :])
    return out.reshape(BATCH, SEQ, D_MODEL)
```

```python
import functools
import math

import jax
import jax.numpy as jnp
import numpy as np
from jax import lax
from jax.experimental import pallas as pl
from jax.experimental.pallas import tpu as pltpu

F32 = jnp.float32
BF16 = jnp.bfloat16

D_MODEL = 1024
BATCH = 8
SEQ = 4096
DEPTH = 4
GRID_W = 64
CTX_LEN = 256
N_MOD = 6
NORM_EPS = 1e-6
ROPE_BASE = 10000.0
MLA_HEADS = 8
MLA_HEAD_DIM = 64
MLA_ROPE_DIM = 32
MLA_Q_RANK = 384
MLA_KV_RANK = 256
DIFF_HEADS = 4
DIFF_HEAD_DIM = 64
PEER_HEADS = 8
PEER_TOPK = 16
N_KEYS = 128
N_EXPERTS = N_KEYS * N_KEYS

T_LAT = BATCH * SEQ
T_CTX = BATCH * CTX_LEN
TT = T_LAT + T_CTX
TM = 256
NB_LAT = T_LAT // TM
NB_ALL = TT // TM
BLK_PER_BATCH = SEQ // TM
MOD_ROWS = 16
PROJ_PAD = 2304
TQ = 256
TK = 512
P_ROWS = 128
PEER_SB = 32
LANES = 128
SUBLANES = 8
VMEM_LIMIT = 56 * 1024 * 1024
LOG2_E = math.log2(math.e)


def _cparams(sem, vmem=VMEM_LIMIT):
    return pltpu.CompilerParams(dimension_semantics=sem, vmem_limit_bytes=vmem)


def _rms(x):
    return x * lax.rsqrt(jnp.mean(x * x, axis=-1, keepdims=True) + NORM_EPS)


def _tile_rows_load(ref):
    return jnp.concatenate([ref[pl.ds(s, TM, stride=SUBLANES), :] for s in range(SUBLANES)], axis=1)


def _tile_rows_store(ref, x):
    for s in range(SUBLANES):
        ref[pl.ds(s, TM, stride=SUBLANES), :] = x[:, s * LANES:(s + 1) * LANES]


def _bid(i):
    return jnp.where(i < NB_LAT, i // BLK_PER_BATCH, BATCH)


def _posblk(i):
    return jnp.where(i < NB_LAT, i % BLK_PER_BATCH, BLK_PER_BATCH)


def _mod_body(cc_ref, w_ref, b_ref, o_ref):
    cc = cc_ref[...]
    s = cc / (1.0 + jnp.exp(-cc))
    o_ref[0] = jnp.dot(s, w_ref[0], precision=lax.Precision.HIGHEST,
                       preferred_element_type=F32) + b_ref[0]


def _modulation(cc, w_ada, b_ada):
    nj = N_MOD
    return pl.pallas_call(
        _mod_body,
        grid=(DEPTH, nj),
        in_specs=[
            pl.BlockSpec((MOD_ROWS, D_MODEL), lambda l, j: (0, 0)),
            pl.BlockSpec((1, D_MODEL, D_MODEL), lambda l, j: (l, 0, j)),
            pl.BlockSpec((1, 1, D_MODEL), lambda l, j: (l, 0, j)),
        ],
        out_specs=pl.BlockSpec((1, MOD_ROWS, D_MODEL), lambda l, j: (l, 0, j)),
        out_shape=jax.ShapeDtypeStruct((DEPTH, MOD_ROWS, N_MOD * D_MODEL), F32),
        compiler_params=_cparams(("arbitrary", "arbitrary")),
        name="modulation",
    )(cc, w_ada, b_ada.reshape(DEPTH, 1, N_MOD * D_MODEL))


def _lam_body(d0_ref, d1_ref, d2_ref, d3_ref, li_ref, o_ref):
    a = jnp.sum(d0_ref[...] * d1_ref[...], axis=-1, keepdims=True)
    b = jnp.sum(d2_ref[...] * d3_ref[...], axis=-1, keepdims=True)
    o_ref[...] = jnp.exp(a) - jnp.exp(b) + li_ref[...]


def _diff_lambda(diff_lambda, lam_init):
    dl = diff_lambda.astype(F32)
    return pl.pallas_call(
        _lam_body,
        out_shape=jax.ShapeDtypeStruct((DEPTH, LANES), F32),
        name="diff_lambda",
    )(dl[:, 0], dl[:, 1], dl[:, 2], dl[:, 3], lam_init)


def _rope(xb, c, sp, sm, shift):
    return xb * c + pltpu.roll(xb, shift, 1) * sp + pltpu.roll(xb, LANES - shift, 1) * sm


def _qkv_body(has_peer, *refs):
    if has_peer:
        x_ref, peer_ref, g2_ref = refs[:3]
        refs = refs[3:]
    else:
        x_ref = refs[0]
        refs = refs[1:]
    (ng_ref, sc_ref, sh_ref, win_ref, gq_ref, gkv_ref, wq_ref, wk_ref, wv_ref, tab_ref) = refs[:10]
    outs = refs[10:]
    if has_peer:
        xo_ref, outs = outs[0], outs[1:]
    qm_ref, km_ref, vm_ref, qd_ref, kd_ref, vd_ref = outs

    x = x_ref[...]
    if has_peer:
        x = x + g2_ref[0] * _tile_rows_load(peer_ref)
        xo_ref[...] = x
    h = _rms(x) * ng_ref[...] * (1.0 + sc_ref[0]) + sh_ref[0]
    proj = jnp.dot(h.astype(BF16), win_ref[...], preferred_element_type=F32)
    cq = _rms(proj[:, 0:384]) * gq_ref[...]
    ckv = _rms(proj[:, 384:640]) * gkv_ref[...]
    kr = proj[:, 640:768]
    q = jnp.dot(cq.astype(BF16), wq_ref[...], preferred_element_type=F32)
    ckv16 = ckv.astype(BF16)
    k = jnp.dot(ckv16, wk_ref[...], preferred_element_type=F32)
    v = jnp.dot(ckv16, wv_ref[...], preferred_element_type=F32)
    tab = tab_ref[...]
    cm, spm, smm = tab[:, 0:128], tab[:, 128:256], tab[:, 256:384]
    cd, spd, smd = tab[:, 384:512], tab[:, 512:640], tab[:, 640:768]
    kr_rot = _rope(kr, cm, spm, smm, MLA_ROPE_DIM // 2)
    scale_m = LOG2_E * (MLA_HEAD_DIM + MLA_ROPE_DIM) ** -0.5
    for hh in range(MLA_HEADS):
        sl = slice(hh * LANES, (hh + 1) * LANES)
        qm_ref[:, sl] = (_rope(q[:, sl], cm, spm, smm, MLA_ROPE_DIM // 2) * scale_m).astype(BF16)
        km_ref[:, sl] = (k[:, sl] + kr_rot).astype(BF16)
    vm_ref[...] = v.T.astype(BF16)
    scale_d = LOG2_E * DIFF_HEAD_DIM ** -0.5
    for hh in range(DIFF_HEADS):
        sl = slice(hh * LANES, (hh + 1) * LANES)
        qd = proj[:, 768 + hh * LANES:768 + (hh + 1) * LANES]
        kd = proj[:, 1280 + hh * LANES:1280 + (hh + 1) * LANES]
        qd_ref[:, sl] = (_rope(qd, cd, spd, smd, DIFF_HEAD_DIM // 2) * scale_d).astype(BF16)
        kd_ref[:, sl] = _rope(kd, cd, spd, smd, DIFF_HEAD_DIM // 2).astype(BF16)
    vd_ref[...] = proj[:, 1792:2304].T.astype(BF16)


def _qkv(x, peer, g2, ng, sc, sh, win, gq, gkv, wq, wk, wv, tab):
    has_peer = peer is not None
    row = lambda i: (i, 0)
    modrow = lambda i: (_bid(i), 0, 0)
    const = lambda i: (0, 0)
    in_specs = [pl.BlockSpec((TM, D_MODEL), row)]
    args = [x]
    if has_peer:
        in_specs += [pl.BlockSpec((TM * SUBLANES, LANES), row), pl.BlockSpec((1, 1, D_MODEL), modrow)]
        args += [peer, g2]
    in_specs += [
        pl.BlockSpec((1, D_MODEL), const),
        pl.BlockSpec((1, 1, D_MODEL), modrow),
        pl.BlockSpec((1, 1, D_MODEL), modrow),
        pl.BlockSpec((D_MODEL, PROJ_PAD), const),
        pl.BlockSpec((1, MLA_Q_RANK), const),
        pl.BlockSpec((1, MLA_KV_RANK), const),
        pl.BlockSpec((MLA_Q_RANK, MLA_HEADS * LANES), const),
        pl.BlockSpec((MLA_KV_RANK, MLA_HEADS * LANES), const),
        pl.BlockSpec((MLA_KV_RANK, 512), const),
        pl.BlockSpec((TM, 768), lambda i: (_posblk(i), 0)),
    ]
    args += [ng, sc, sh, win, gq, gkv, wq, wk, wv, tab]
    out_specs, out_shape = [], []
    if has_peer:
        out_specs.append(pl.BlockSpec((TM, D_MODEL), row))
        out_shape.append(jax.ShapeDtypeStruct((TT, D_MODEL), F32))
    for width, transposed in ((1024, False), (1024, False), (512, True),
                              (512, False), (512, False), (512, True)):
        if transposed:
            out_specs.append(pl.BlockSpec((width, TM), lambda i: (0, i)))
            out_shape.append(jax.ShapeDtypeStruct((width, TT), BF16))
        else:
            out_specs.append(pl.BlockSpec((TM, width), row))
            out_shape.append(jax.ShapeDtypeStruct((TT, width), BF16))
    res = pl.pallas_call(
        functools.partial(_qkv_body, has_peer),
        grid=(NB_ALL,),
        in_specs=in_specs,
        out_specs=out_specs,
        out_shape=out_shape,
        compiler_params=_cparams(("parallel",)),
        name="qkv",
    )(*args)
    if has_peer:
        return res[0], res[1:]
    return x, res


def _flash_pair(tq, qs, ksls, vsls, kc_ref, kl_ref, vtc_ref, vtl_ref, n_pairs, sa_scr, sb_scr, p_scr):
    n_lat = SEQ // TK

    def col_reduce(parts, op, final):
        while len(parts) > 1:
            parts = [op(parts[i], parts[i + 1]) for i in range(0, len(parts), 2)]
        return final(parts[0], axis=0, keepdims=True)

    def row_groups(x):
        return [x[i:i + SUBLANES] for i in range(0, x.shape[0], SUBLANES)]

    def scores(k_of):
        return [lax.dot_general(k_of(ksl), q, (((1,), (1,)), ((), ())), preferred_element_type=F32)
                for q, ksl in zip(qs, ksls)]

    def softmax_pv(read_s, vt_of, carries, n_keys, par):
        mid = []
        for h, (m, l, acc) in enumerate(carries):
            m_new = jnp.maximum(m, col_reduce(row_groups(read_s(h, 0, n_keys)), jnp.maximum, jnp.max))
            alpha = jnp.exp2(m - m_new)
            for r0 in range(0, n_keys, P_ROWS):
                p = jnp.exp2(read_s(h, r0, P_ROWS) - m_new)
                l = alpha * l + col_reduce(row_groups(p), jnp.add, jnp.sum) if r0 == 0 else (
                    l + col_reduce(row_groups(p), jnp.add, jnp.sum))
                p_scr[par, h, r0:r0 + P_ROWS, :] = p.astype(BF16)
            mid.append((m_new, l, alpha * acc))
        return tuple((m_new, l, acc + jnp.dot(vt_of(vsl), p_scr[par, h, 0:n_keys, :], preferred_element_type=F32))
                     for h, ((m_new, l, acc), vsl) in enumerate(zip(mid, vsls)))

    def k_lat(c):
        return lambda ksl: kl_ref[c * TK:(c + 1) * TK, ksl]

    def vt_lat(c):
        return lambda vsl: vtl_ref[vsl, c * TK:(c + 1) * TK]

    def stage(scr, sc):
        for h, x in enumerate(sc):
            scr[h] = x

    init = tuple((jnp.full((1, tq), -jnp.inf, F32), jnp.zeros((1, tq), F32),
                  jnp.zeros((vsl.stop - vsl.start, tq), F32)) for vsl in vsls)
    ctx_scores = scores(lambda ksl: kc_ref[:, ksl])
    if n_pairs:
        stage(sa_scr, scores(k_lat(0)))
    carries = softmax_pv(lambda h, r0, n: ctx_scores[h][r0:r0 + n], lambda vsl: vtc_ref[vsl, :], init, CTX_LEN, 1)

    def pair(j, carries):
        c = 2 * j
        stage(sb_scr, scores(k_lat(c + 1)))
        carries = softmax_pv(lambda h, r0, n: sa_scr[h, r0:r0 + n, :], vt_lat(c), carries, TK, 0)
        if c + 2 < n_lat:
            stage(sa_scr, scores(k_lat(c + 2)))
        return softmax_pv(lambda h, r0, n: sb_scr[h, r0:r0 + n, :], vt_lat(c + 1), carries, TK, 1)

    for j in range(n_pairs):
        carries = pair(j, carries)
    return [acc / l for (m, l, acc) in carries]


def _attn_mla_body(tq, n_pairs, *refs):
    if n_pairs:
        q_ref, kc_ref, kl_ref, vtc_ref, vtl_ref, o_ref, sa_scr, sb_scr, p_scr = refs
    else:
        q_ref, kc_ref, vtc_ref, o_ref, p_scr = refs
        kl_ref = vtl_ref = sa_scr = sb_scr = None
    qs = [q_ref[:, 0:LANES], q_ref[:, LANES:2 * LANES]]
    ksls = [slice(0, LANES), slice(LANES, 2 * LANES)]
    vsls = [slice(0, MLA_HEAD_DIM), slice(MLA_HEAD_DIM, 2 * MLA_HEAD_DIM)]
    ot0, ot1 = _flash_pair(tq, qs, ksls, vsls, kc_ref, kl_ref, vtc_ref, vtl_ref, n_pairs, sa_scr, sb_scr, p_scr)
    o_ref[...] = jnp.concatenate([ot0, ot1], axis=0).T.astype(o_ref.dtype)


def _attn_diff_body(tq, n_pairs, lam_scale, *refs):
    if n_pairs:
        q_ref, kc_ref, kl_ref, vtc_ref, vtl_ref, lam_ref, g_ref, o_ref, sa_scr, sb_scr, p_scr = refs
    else:
        q_ref, kc_ref, vtc_ref, lam_ref, g_ref, o_ref, p_scr = refs
        kl_ref = vtl_ref = sa_scr = sb_scr = None
    lane = lax.broadcasted_iota(jnp.int32, (tq, LANES), 1)
    q = q_ref[...]
    zero = jnp.zeros_like(q)
    qs = [jnp.where(lane < DIFF_HEAD_DIM, q, zero), jnp.where(lane < DIFF_HEAD_DIM, zero, q)]
    ksls = [slice(0, LANES), slice(0, LANES)]
    vsls = [slice(0, LANES), slice(0, LANES)]
    ot0, ot1 = _flash_pair(tq, qs, ksls, vsls, kc_ref, kl_ref, vtc_ref, vtl_ref, n_pairs, sa_scr, sb_scr, p_scr)
    o = ot0.T - lam_ref[...] * ot1.T
    o_ref[...] = (_rms(o) * g_ref[...] * lam_scale).astype(o_ref.dtype)


def _attention(kind, q, k, vt, ctx_queries, lam=None, g=None, lam_scale=None):
    qw = 2 * LANES if kind == "mla" else LANES
    ctx0 = T_LAT // CTX_LEN
    if ctx_queries:
        tq, n_pairs, steps, rows = CTX_LEN, 0, 1, T_CTX
        q_map = lambda b, p, i: (ctx0 + b, p)
        o_map = lambda b, p, i: (b, p)
    else:
        tq, n_pairs, steps, rows = TQ, SEQ // (2 * TK), SEQ // TQ, T_LAT
        q_map = lambda b, p, i: (b * steps + i, p)
        o_map = q_map
    kc_spec = pl.BlockSpec((CTX_LEN, qw), lambda b, p, i: (ctx0 + b, p))
    vtc_spec = pl.BlockSpec((LANES, CTX_LEN), lambda b, p, i: (p, ctx0 + b))
    in_specs = [pl.BlockSpec((tq, qw), q_map), kc_spec]
    args = [q, k]
    if n_pairs:
        in_specs += [pl.BlockSpec((SEQ, qw), lambda b, p, i: (b, p)), vtc_spec,
                     pl.BlockSpec((LANES, SEQ), lambda b, p, i: (p, b))]
        args += [k, vt, vt]
    else:
        in_specs += [vtc_spec]
        args += [vt]
    if kind == "mla":
        body = functools.partial(_attn_mla_body, tq, n_pairs)
    else:
        body = functools.partial(_attn_diff_body, tq, n_pairs, lam_scale)
        in_specs += [pl.BlockSpec((1, LANES), lambda b, p, i: (0, 0)),
                     pl.BlockSpec((1, LANES), lambda b, p, i: (0, 0))]
        args += [lam, g]
    scratch = [pltpu.VMEM((2, TK, tq), F32), pltpu.VMEM((2, TK, tq), F32)] if n_pairs else []
    scratch.append(pltpu.VMEM((2, 2, TK, tq), BF16))
    return pl.pallas_call(
        body,
        grid=(BATCH, 4, steps),
        in_specs=in_specs,
        out_specs=pl.BlockSpec((tq, LANES), o_map),
        out_shape=jax.ShapeDtypeStruct((rows, 512), BF16),
        scratch_shapes=scratch,
        compiler_params=_cparams(("parallel", "parallel", "arbitrary")),
        name="attn_" + kind + ("_ctx" if ctx_queries else ""),
    )(*args)


HEADS_PER_TRIP = 4


def _top16(s, payload=None):
    n_rows = s.shape[0]
    rowf = lax.broadcasted_iota(jnp.int32, s.shape, 0).astype(F32)
    slot = lax.broadcasted_iota(jnp.int32, (PEER_TOPK, s.shape[1]), 0)
    vals = jnp.zeros((PEER_TOPK, s.shape[1]), F32)
    picks = jnp.zeros((PEER_TOPK, s.shape[1]), F32)
    for r in range(PEER_TOPK):
        m = jnp.max(s, axis=0, keepdims=True)
        am = jnp.min(jnp.where(s == m, rowf, float(n_rows)), axis=0, keepdims=True)
        hit = rowf == am
        pick = am if payload is None else jnp.max(jnp.where(hit, payload, -1.0), axis=0, keepdims=True)
        vals = jnp.where(slot == r, m, vals)
        picks = jnp.where(slot == r, pick, picks)
        s = jnp.where(hit, -jnp.inf, s)
    return vals, picks


def _staircase(a16, b16, combine, pad):
    tm = a16.shape[1]
    sub = lax.broadcasted_iota(jnp.int32, (SUBLANES, tm), 0)
    a_lo, a_hi = a16[0:SUBLANES], a16[SUBLANES:]
    b_lo, b_hi = b16[0:SUBLANES], b16[SUBLANES:]
    row = lambda x, r: jnp.broadcast_to(x[r:r + 1], (SUBLANES, tm))
    take = lambda x, idx: jnp.take_along_axis(x, idx, axis=0)
    a3 = jnp.where(sub < 5, 2, 3)
    b3 = jnp.where(sub < 5, sub, sub - 5)
    a4 = jnp.where(sub < 1, 3, jnp.where(sub < 4, 4, jnp.where(sub < 6, 5, 6)))
    b4 = jnp.where(sub < 1, 3, jnp.where(sub < 4, sub - 1, jnp.where(sub < 6, sub - 4, sub - 6)))
    pieces = [
        combine(row(a_lo, 0), b_lo),
        combine(row(a_lo, 0), b_hi),
        combine(row(a_lo, 1), b_lo),
        combine(take(a_lo, a3), take(b_lo, b3)),
        combine(take(a_lo, a4), take(b_lo, b4)),
        jnp.where(sub < 2, combine(row(a_lo, 7), b_lo), pad),
        combine(a_hi, row(b_lo, 0)),
    ]
    return jnp.concatenate(pieces, axis=0)


def _mid_body(x_ref, om_ref, od_ref, wo_m_ref, wo_d_ref, g1_ref, ng_ref, sc_ref, sh_ref,
              wq_ref, sk_ref, xo_ref, h_ref, eidx_ref, gate_ref, q_scr):
    y = (jnp.dot(om_ref[...], wo_m_ref[...], preferred_element_type=F32)
         + jnp.dot(od_ref[...], wo_d_ref[...], preferred_element_type=F32))
    x = x_ref[...] + g1_ref[0] * y
    xo_ref[...] = x
    h = _rms(x) * ng_ref[...] * (1.0 + sc_ref[0]) + sh_ref[0]
    _tile_rows_store(h_ref, h)
    q_scr[...] = jnp.dot(h.astype(BF16), wq_ref[...], preferred_element_type=F32).astype(BF16)

    def head(hh):
        sv, si = [], []
        for c in range(2):
            c0 = pl.multiple_of(hh * 2 * N_KEYS + c * N_KEYS, N_KEYS)
            qh = q_scr[:, pl.ds(c0, N_KEYS)]
            s = lax.dot_general(sk_ref[c], qh, (((1,), (1,)), ((), ())), preferred_element_type=F32)
            vals, keys = _top16(s)
            sv.append(vals)
            si.append(keys)
        cand_s = _staircase(sv[0], sv[1], lambda a, b: a + b, -jnp.inf)
        cand_e = _staircase(si[0], si[1], lambda a, b: a * float(N_KEYS) + b, -1.0)
        top_s, top_e = _top16(cand_s, cand_e)
        ex = jnp.exp(top_s - jnp.max(top_s, axis=0, keepdims=True))
        gate = ex / jnp.sum(ex, axis=0, keepdims=True)
        r0 = pl.multiple_of(hh * PEER_TOPK, PEER_TOPK)
        eidx_ref[pl.ds(r0, PEER_TOPK), :] = top_e.astype(jnp.int32) * ROWS_PER_EXPERT
        gate_ref[pl.ds(r0, PEER_TOPK), :] = gate

    def head_group(j, _):
        for i in range(HEADS_PER_TRIP):
            head(HEADS_PER_TRIP * j + i)
        return 0

    lax.fori_loop(0, PEER_HEADS // HEADS_PER_TRIP, head_group, 0)


def _mid(n_blocks, x, om, od, wo_m, wo_d, g1, ng, sc, sh, wq, sk):
    n = n_blocks * TM
    row = lambda i: (i, 0)
    modrow = lambda i: (_bid(i), 0, 0)
    const = lambda i: (0, 0)
    return pl.pallas_call(
        _mid_body,
        grid=(n_blocks,),
        in_specs=[
            pl.BlockSpec((TM, D_MODEL), row),
            pl.BlockSpec((TM, 512), row),
            pl.BlockSpec((TM, 512), row),
            pl.BlockSpec((512, D_MODEL), const),
            pl.BlockSpec((512, D_MODEL), const),
            pl.BlockSpec((1, 1, D_MODEL), modrow),
            pl.BlockSpec((1, D_MODEL), const),
            pl.BlockSpec((1, 1, D_MODEL), modrow),
            pl.BlockSpec((1, 1, D_MODEL), modrow),
            pl.BlockSpec((D_MODEL, PEER_HEADS * 2 * N_KEYS), const),
            pl.BlockSpec((2, N_KEYS, N_KEYS), lambda i: (0, 0, 0)),
        ],
        out_specs=[
            pl.BlockSpec((TM, D_MODEL), row),
            pl.BlockSpec((TM * SUBLANES, LANES), row),
            pl.BlockSpec((PEER_HEADS * PEER_TOPK, TM), lambda i: (0, i)),
            pl.BlockSpec((PEER_HEADS * PEER_TOPK, TM), lambda i: (0, i)),
        ],
        out_shape=[
            jax.ShapeDtypeStruct((n, D_MODEL), F32),
            jax.ShapeDtypeStruct((n * SUBLANES, LANES), F32),
            jax.ShapeDtypeStruct((PEER_HEADS * PEER_TOPK, n), jnp.int32),
            jax.ShapeDtypeStruct((PEER_HEADS * PEER_TOPK, n), F32),
        ],
        scratch_shapes=[pltpu.VMEM((TM, PEER_HEADS * 2 * N_KEYS), BF16)],
        compiler_params=_cparams(("parallel",)),
        name="mid",
    )(x, om, od, wo_m, wo_d, g1, ng, sc, sh, wq, sk)


N_SEL = PEER_HEADS * PEER_TOPK
HALF = D_MODEL // 2
ROWS_PER_EXPERT = HALF // LANES


PACK_BLOCK = 512


def _pack_body(t_ref, o_ref):
    u = pltpu.bitcast(t_ref[0], jnp.uint32)
    r = u + jnp.uint32(0x7FFF) + ((u >> 16) & jnp.uint32(1))
    w = (r[:, :HALF] >> 16) | (r[:, HALF:] & jnp.uint32(0xFFFF0000))
    for s in range(ROWS_PER_EXPERT):
        o_ref[pl.ds(s, PACK_BLOCK, stride=ROWS_PER_EXPERT), :] = w[:, s * LANES:(s + 1) * LANES]


def _pack_table(tabs, layer):
    n = tabs.shape[1]
    return pl.pallas_call(
        _pack_body,
        grid=(n // PACK_BLOCK,),
        in_specs=[pl.BlockSpec((1, PACK_BLOCK, D_MODEL), lambda i: (layer, i, 0))],
        out_specs=pl.BlockSpec((PACK_BLOCK * ROWS_PER_EXPERT, LANES), lambda i: (i, 0)),
        out_shape=jax.ShapeDtypeStruct((n * ROWS_PER_EXPERT, LANES), jnp.uint32),
        compiler_params=_cparams(("parallel",)),
        name="pack_table",
    )(tabs)


def _load_pair(tab_ref, ra, rb):
    w = jnp.concatenate([tab_ref[pl.ds(pl.multiple_of(ra, ROWS_PER_EXPERT), ROWS_PER_EXPERT), :],
                         tab_ref[pl.ds(pl.multiple_of(rb, ROWS_PER_EXPERT), ROWS_PER_EXPERT), :]], axis=0)
    lo = pltpu.bitcast(w << 16, F32)
    hi = pltpu.bitcast(w & jnp.uint32(0xFFFF0000), F32)
    return lo, hi


def _with_index_buffers(idx_hbm, idx_a, idx_b, sem, first, second):
    step = pl.program_id(0)
    words = PEER_SB * N_SEL

    def idx_copy(block, dst, k):
        return pltpu.make_async_copy(idx_hbm.at[pl.ds(block * words, words)], dst, sem.at[k])

    @pl.when(step == 0)
    def _():
        idx_copy(0, idx_a, 0).start()

    idx_copy(2 * step + 1, idx_b, 1).start()
    idx_copy(2 * step, idx_a, 0).wait()
    first(idx_a)

    @pl.when(step + 1 < pl.num_programs(0))
    def _():
        idx_copy(2 * step + 2, idx_a, 0).start()

    idx_copy(2 * step + 1, idx_b, 1).wait()
    second(idx_b)


def _peer_u_body(idx_hbm, x_ref, gate_ref, tab_ref, act_ref, idx_a, idx_b, r_scr, score_scr, sem):
    sub = lax.broadcasted_iota(jnp.int32, (SUBLANES, LANES), 0)
    lane = lax.broadcasted_iota(jnp.int32, (SUBLANES, LANES), 1)
    top = sub < 4
    n_pair = N_SEL // 2
    quarter = pl.program_id(0) % STEPS_PER_ACT_BLOCK

    def gather_token(idx_s, tok, tok_in_buf):
        xt = x_ref[tok]
        xsw = pltpu.roll(xt, 4, 0)
        xlo = jnp.where(top, xt, xsw)
        xhi = jnp.where(top, xsw, xt)
        for p in range(n_pair):
            ra = idx_s[tok_in_buf * N_SEL + 2 * p]
            rb = idx_s[tok_in_buf * N_SEL + 2 * p + 1]
            lo, hi = _load_pair(tab_ref, ra, rb)
            r_scr[tok % R_SLOTS * n_pair + p] = lo * xlo + hi * xhi

    def reduce_token(tok):
        token_lane = quarter * (2 * PEER_SB) + tok
        for p in range(n_pair):
            col = jnp.sum(r_scr[tok % R_SLOTS * n_pair + p], axis=1, keepdims=True)
            score_scr[p] = jnp.where(lane == token_lane, col, score_scr[p])

    @pl.when(quarter == 0)
    def _():
        score_scr[...] = jnp.zeros(score_scr.shape, F32)

    def first(idx_s):
        for i in range(PEER_SB):
            gather_token(idx_s, i, i)
            if i:
                reduce_token(i - 1)

    def second(idx_s):
        for i in range(PEER_SB):
            gather_token(idx_s, PEER_SB + i, i)
            reduce_token(PEER_SB + i - 1)
        reduce_token(2 * PEER_SB - 1)

    _with_index_buffers(idx_hbm, idx_a, idx_b, sem, first, second)

    @pl.when(quarter == STEPS_PER_ACT_BLOCK - 1)
    def _():
        for g in range(N_SEL // SUBLANES):
            scores = jnp.zeros((SUBLANES, LANES), F32)
            for pp in range(4):
                s = score_scr[4 * g + pp]
                t = s + pltpu.roll(s, 6, 0)
                t = t + pltpu.roll(t, 7, 0)
                scores = jnp.where(sub == 2 * pp, pltpu.roll(t, 2 * pp, 0), scores)
                scores = jnp.where(sub == 2 * pp + 1, pltpu.roll(t, (2 * pp + 5) % SUBLANES, 0), scores)
            act_ref[g] = 0.5 * scores * (1.0 + lax.erf(scores * (2.0 ** -0.5))) * gate_ref[g]


def _peer_v_body(idx_hbm, act_ref, tab_ref, o_ref, idx_a, idx_b, sem):
    sub = lax.broadcasted_iota(jnp.int32, (SUBLANES, LANES), 0)
    lane = lax.broadcasted_iota(jnp.int32, (SUBLANES, LANES), 1)
    top = sub < 4
    n_grp = N_SEL // SUBLANES
    pair_rows = [jnp.where(top, 2 * p, 2 * p + 1) for p in range(4)]

    quarter = pl.program_id(0) % STEPS_PER_ACT_BLOCK

    def process(idx_s, half):
        for tile in range(PEER_SB // SUBLANES):
            for tt in range(SUBLANES):
                tl = tile * SUBLANES + tt
                token_lane = quarter * (2 * PEER_SB) + half * PEER_SB + tl
                acc = [jnp.zeros((SUBLANES, LANES), F32) for _ in range(4)]
                for g in range(n_grp):
                    col = jnp.sum(jnp.where(lane == token_lane, act_ref[g], 0.0), axis=1, keepdims=True)
                    actg = jnp.broadcast_to(col, (SUBLANES, LANES))
                    for p in range(4):
                        ra = idx_s[tl * N_SEL + g * SUBLANES + 2 * p]
                        rb = idx_s[tl * N_SEL + g * SUBLANES + 2 * p + 1]
                        lo, hi = _load_pair(tab_ref, ra, rb)
                        av = jnp.take_along_axis(actg, pair_rows[p], axis=0)
                        k = 2 * (p % 2)
                        acc[k] = acc[k] + av * lo
                        acc[k + 1] = acc[k + 1] + av * hi
                lo = acc[0] + acc[2]
                hi = acc[1] + acc[3]
                lo = lo + pltpu.roll(lo, 4, 0)
                hi = hi + pltpu.roll(hi, 4, 0)
                o_ref[half * PEER_SB + tl] = jnp.where(top, lo, hi)

    _with_index_buffers(idx_hbm, idx_a, idx_b, sem,
                        lambda idx_s: process(idx_s, 0), lambda idx_s: process(idx_s, 1))


def _table_spec():
    return pl.BlockSpec((N_EXPERTS * ROWS_PER_EXPERT, LANES), lambda i: (0, 0),
                        pipeline_mode=pl.Buffered(1))


R_SLOTS = 2
STEPS_PER_ACT_BLOCK = LANES // (2 * PEER_SB)


def _act_block_spec():
    return pl.BlockSpec((N_SEL // SUBLANES, SUBLANES, LANES), lambda i: (0, 0, i // STEPS_PER_ACT_BLOCK))


def _peer_idx_scratch():
    return [pltpu.SMEM((PEER_SB * N_SEL,), jnp.int32), pltpu.SMEM((PEER_SB * N_SEL,), jnp.int32)]


def _peer_u(n, idx, x3, gate3, tab):
    tb = 2 * PEER_SB
    return pl.pallas_call(
        _peer_u_body,
        grid=(n // tb,),
        in_specs=[
            pl.BlockSpec(memory_space=pl.ANY),
            pl.BlockSpec((tb, SUBLANES, LANES), lambda i: (i, 0, 0)),
            _act_block_spec(),
            _table_spec(),
        ],
        out_specs=_act_block_spec(),
        out_shape=jax.ShapeDtypeStruct((N_SEL // SUBLANES, SUBLANES, n), F32),
        scratch_shapes=_peer_idx_scratch() + [
            pltpu.VMEM((R_SLOTS * (N_SEL // 2), SUBLANES, LANES), F32),
            pltpu.VMEM((N_SEL // 2, SUBLANES, LANES), F32),
            pltpu.SemaphoreType.DMA((2,))],
        compiler_params=_cparams(("arbitrary",)),
        name="peer_u",
    )(idx.reshape(-1), x3, gate3, tab)


def _peer_v(n, idx, act3, tab):
    tb = 2 * PEER_SB
    return pl.pallas_call(
        _peer_v_body,
        grid=(n // tb,),
        in_specs=[
            pl.BlockSpec(memory_space=pl.ANY),
            _act_block_spec(),
            _table_spec(),
        ],
        out_specs=pl.BlockSpec((tb, SUBLANES, LANES), lambda i: (i, 0, 0)),
        out_shape=jax.ShapeDtypeStruct((n, SUBLANES, LANES), F32),
        scratch_shapes=_peer_idx_scratch() + [pltpu.SemaphoreType.DMA((2,))],
        compiler_params=_cparams(("arbitrary",)),
        name="peer_v",
    )(idx.reshape(-1), act3, tab)


def _final_body(x_ref, peer_ref, g2_ref, g_ref, o_ref):
    x = x_ref[...] + g2_ref[0] * _tile_rows_load(peer_ref)
    o_ref[...] = _rms(x) * g_ref[...]


def _final(x, peer, g2, g):
    row = lambda i: (i, 0)
    return pl.pallas_call(
        _final_body,
        grid=(NB_LAT,),
        in_specs=[
            pl.BlockSpec((TM, D_MODEL), row),
            pl.BlockSpec((TM * SUBLANES, LANES), row),
            pl.BlockSpec((1, 1, D_MODEL), lambda i: (_bid(i), 0, 0)),
            pl.BlockSpec((1, D_MODEL), lambda i: (0, 0)),
        ],
        out_specs=pl.BlockSpec((TM, D_MODEL), row),
        out_shape=jax.ShapeDtypeStruct((T_LAT, D_MODEL), F32),
        compiler_params=_cparams(("parallel",)),
        name="final_norm",
    )(x, peer, g2, g)


def _deinterleave(width):
    return np.concatenate([np.arange(0, width, 2), np.arange(1, width, 2)])


def _rope_tables():
    pos = np.arange(SEQ)
    rowp = jnp.asarray(pos // GRID_W, F32)
    colp = jnp.asarray(pos % GRID_W, F32)

    def angles(dim):
        quarter = dim // 4
        inv = ROPE_BASE ** (-jnp.arange(quarter, dtype=F32) / quarter)
        return jnp.concatenate([rowp[:, None] * inv, colp[:, None] * inv], axis=-1)

    am = angles(MLA_ROPE_DIM)
    ad = angles(DIFF_HEAD_DIM)
    cm, sm_ = jnp.cos(am), jnp.sin(am)
    cd, sd = jnp.cos(ad), jnp.sin(ad)
    one = lambda w: jnp.ones((SEQ, w), F32)
    zero = lambda w: jnp.zeros((SEQ, w), F32)
    t_cm = jnp.concatenate([one(64), cm, cm, one(32)], axis=1)
    t_spm = jnp.concatenate([zero(80), sm_, zero(32)], axis=1)
    t_smm = jnp.concatenate([zero(64), -sm_, zero(48)], axis=1)
    t_cd = jnp.concatenate([cd, cd, cd, cd], axis=1)
    t_spd = jnp.concatenate([zero(32), sd, zero(32), sd], axis=1)
    t_smd = jnp.concatenate([-sd, zero(32), -sd, zero(32)], axis=1)
    lat = jnp.concatenate([t_cm, t_spm, t_smm, t_cd, t_spd, t_smd], axis=1)
    ident = jnp.concatenate([jnp.ones((CTX_LEN, 128), F32), jnp.zeros((CTX_LEN, 256), F32),
                             jnp.ones((CTX_LEN, 128), F32), jnp.zeros((CTX_LEN, 256), F32)], axis=1)
    return jnp.concatenate([lat, ident], axis=0)


def _prep_layer_weights(w_in, wq_up, wq_rope, wk_up, wv_up, w_out, peer_wq, peer_subkeys):
    p32 = _deinterleave(MLA_ROPE_DIM)
    p64 = _deinterleave(DIFF_HEAD_DIM)
    z = lambda w: jnp.zeros((D_MODEL, w), F32)
    kr = w_in[:, 640:672][:, p32]
    qd = w_in[:, 672:1184].reshape(D_MODEL, 8, 64)[:, :, p64].reshape(D_MODEL, 512)
    kd = w_in[:, 1184:1696].reshape(D_MODEL, 8, 64)[:, :, p64].reshape(D_MODEL, 512)
    win = jnp.concatenate([w_in[:, 0:640], z(64), kr, z(32), qd, kd, w_in[:, 1696:2208]], axis=1)
    qn = wq_up.reshape(MLA_Q_RANK, MLA_HEADS, MLA_HEAD_DIM)
    qr = wq_rope.reshape(MLA_Q_RANK, MLA_HEADS, MLA_ROPE_DIM)[:, :, p32]
    wq = jnp.concatenate([qn, qr, jnp.zeros((MLA_Q_RANK, MLA_HEADS, 32), F32)], axis=2)
    kn = wk_up.reshape(MLA_KV_RANK, MLA_HEADS, MLA_HEAD_DIM)
    wk = jnp.concatenate([kn, jnp.zeros((MLA_KV_RANK, MLA_HEADS, 64), F32)], axis=2)
    return dict(
        win=win.astype(BF16),
        wq=wq.reshape(MLA_Q_RANK, MLA_HEADS * LANES).astype(BF16),
        wk=wk.reshape(MLA_KV_RANK, MLA_HEADS * LANES).astype(BF16),
        wv=wv_up.astype(BF16),
        wo_m=w_out[:512].astype(BF16),
        wo_d=w_out[512:].astype(BF16),
        pwq=peer_wq.astype(BF16),
        sk=peer_subkeys.astype(BF16),
    )


def kernel(x, c, ctx, c_ctx, norm_attn_g, norm_ffn_g, w_ada, b_ada, w_in, mla_q_norm_g, mla_wq_up, mla_wq_rope, mla_kv_norm_g, mla_wk_up, mla_wv_up, diff_lambda, diff_subnorm_g, w_out, peer_wq, peer_subkeys, peer_u, peer_v, final_norm_g):
    xs = jnp.concatenate([x.reshape(T_LAT, D_MODEL), ctx.reshape(T_CTX, D_MODEL)], axis=0)
    cc = jnp.concatenate([c, c_ctx[None, :], jnp.zeros((MOD_ROWS - BATCH - 1, D_MODEL), F32)], axis=0)
    mod = _modulation(cc, w_ada, b_ada)
    lam_inits = [0.8 - 0.6 * math.exp(-0.3 * l) for l in range(DEPTH)]
    lam_all = _diff_lambda(diff_lambda, jnp.broadcast_to(jnp.asarray(lam_inits, F32)[:, None], (DEPTH, LANES)))
    tab = _rope_tables()

    peer_out = None
    g2_prev = None
    for l in range(DEPTH):
        last = l == DEPTH - 1
        w = _prep_layer_weights(w_in[l], mla_wq_up[l], mla_wq_rope[l], mla_wk_up[l], mla_wv_up[l],
                                w_out[l], peer_wq[l], peer_subkeys[l])
        m = mod[l].reshape(MOD_ROWS, N_MOD, 1, D_MODEL)
        sh1, sc1, g1, sh2, sc2, g2 = (m[:, j] for j in range(N_MOD))
        xs, (qm, km, vm, qd, kd, vd) = _qkv(
            xs, peer_out, g2_prev, norm_attn_g[l][None, :], sc1, sh1, w["win"],
            mla_q_norm_g[l][None, :], mla_kv_norm_g[l][None, :], w["wq"], w["wk"], w["wv"], tab)
        diff_args = dict(lam=lam_all[l][None, :], g=diff_subnorm_g[l][None, :],
                         lam_scale=1.0 - lam_inits[l])
        om = _attention("mla", qm, km, vm, False)
        od = _attention("diff", qd, kd, vd, False, **diff_args)
        if not last:
            om = jnp.concatenate([om, _attention("mla", qm, km, vm, True)], axis=0)
            od = jnp.concatenate([od, _attention("diff", qd, kd, vd, True, **diff_args)], axis=0)
        n_blocks = NB_LAT if last else NB_ALL
        n = n_blocks * TM
        xs, h2, eidx_t, gate_t = _mid(n_blocks, xs, om, od, w["wo_m"], w["wo_d"], g1,
                                      norm_ffn_g[l][None, :], sc2, sh2, w["pwq"], w["sk"])
        idx = eidx_t.T
        gate3 = gate_t.reshape(N_SEL // SUBLANES, SUBLANES, n)
        act3 = _peer_u(n, idx, h2.reshape(n, SUBLANES, LANES), gate3, _pack_table(peer_u, l))
        peer_out = _peer_v(n, idx, act3, _pack_table(peer_v, l)).reshape(n * SUBLANES, LANES)
        g2_prev = g2
    out = _final(xs, peer_out, g2_prev, final_norm_g[None, :])
    return out.reshape(BATCH, SEQ, D_MODEL)
```
